```python
import jax, jax.numpy as jnp
from jax import lax
import numpy as np

D_MODEL = 1024
BATCH = 4
SEQ = 4096
DEPTH = 2

GRID_W = 64
CTX_LEN = 256
EPS = 1e-6

N_MIXERS = 4
D_MIX = D_MODEL
D_GROUP = D_MIX // N_MIXERS
A_HEADS = 4
A_DH = D_GROUP // A_HEADS
B_HEADS = 4
B_DH = D_GROUP // B_HEADS
WIN_H = 8
WIN_W = 16
C_HEADS = 4
C_DV = D_GROUP // C_HEADS
C_DK = C_DV // 2
C_GATE_RANK = 16
C_GATE_NORM = 16.0
D_HEADS = 4
D_NOPE = D_GROUP // D_HEADS
D_V = D_GROUP // D_HEADS
D_ROPE = D_NOPE // 2
ROPE_FREQS = D_ROPE // 4
ROPE_BASE = 10000.0
D_Q_RANK = 3 * D_GROUP // 4
D_KV_RANK = D_GROUP // 2
MLA_SCALE = (D_NOPE + D_ROPE) ** -0.5
Q_BLOCK = 128
CHUNK = 64
N_GROUPS = 4
EXPERTS_PER_GROUP = 8
N_EXPERTS = N_GROUPS * EXPERTS_PER_GROUP
TOP_K = 2
D_EXPERT = D_MODEL // 4
MOE_BLOCK = 256

IN_SIZES = (
    D_GROUP, D_GROUP, D_GROUP, D_GROUP, D_GROUP,
    D_GROUP, D_GROUP, D_GROUP,
    C_HEADS * C_DK, C_HEADS * C_DK, C_HEADS * C_DV, C_HEADS * C_DV,
    C_GATE_RANK, C_GATE_RANK,
    D_Q_RANK, D_KV_RANK, D_ROPE,
)
D_IN = sum(IN_SIZES)

kernel_name = 'hybrid_parallel_heads_dit_moe'


def rmsnorm(x, g):
    xf = x.astype(jnp.float32)
    y = xf * lax.rsqrt(jnp.mean(xf * xf, axis=-1, keepdims=True) + EPS)
    return (y * g.astype(jnp.float32)).astype(x.dtype)


def split_heads(t, n_heads):
    return t.reshape(t.shape[:-1] + (n_heads, t.shape[-1] // n_heads))


def merge_heads(t):
    return t.reshape(t.shape[:-2] + (t.shape[-2] * t.shape[-1],))


def flip(t):
    return jnp.flip(t, axis=1)


def split_projection(p):
    return jnp.split(p, np.cumsum(IN_SIZES)[:-1].tolist(), axis=-1)


def rope_2d(u, cos, sin):
    v = u.reshape(u.shape[:-1] + (2, 2, ROPE_FREQS))
    u1, u2 = v[..., 0, :], v[..., 1, :]
    out = jnp.stack([u1 * cos - u2 * sin, u1 * sin + u2 * cos], axis=-2)
    return out.reshape(u.shape).astype(u.dtype)


def chunk_gated_recurrence(q, k, v, log_a, s0):
    b, t, h, _ = q.shape
    dv = v.shape[-1]
    n = t // CHUNK

    def chunks(u):
        return jnp.moveaxis(u.astype(jnp.float32).reshape(b, n, CHUNK, h, u.shape[-1]), 1, 0)

    lower = jnp.tril(jnp.ones((CHUNK, CHUNK), dtype=bool))[None, :, :, None, None]

    def step(state, inp):
        qc, kc, vc, gc = inp
        cum = jnp.cumsum(gc, axis=1)
        rel = jnp.exp(jnp.where(lower, cum[:, :, None] - cum[:, None], -jnp.inf))
        att = jnp.einsum('bihk,bjhk,bijhk->bhij', qc, kc, rel)
        o = (jnp.einsum('bhij,bjhv->bihv', att, vc)
             + jnp.einsum('bihk,bhkv->bihv', qc * jnp.exp(cum), state))
        last = cum[:, -1]
        state = (jnp.exp(last)[..., None] * state
                 + jnp.einsum('bjhk,bjhv->bhkv', kc * jnp.exp(last[:, None] - cum), vc))
        return state, o

    state, o = lax.scan(step, s0, (chunks(q), chunks(k), chunks(v), chunks(log_a)))
    o = jnp.moveaxis(o, 0, 1).reshape(b, t, h, dv)
    return o.astype(v.dtype), state


def bidirectional_recurrence(lat, ctx):
    def both_directions(q, v, k_f, la_f, k_b, la_b, s_f, s_b):
        o_f, s_f = chunk_gated_recurrence(q, k_f, v, la_f, s_f)
        o_b, s_b = chunk_gated_recurrence(flip(q), flip(k_b), flip(v), flip(la_b), s_b)
        return o_f + flip(o_b), s_f, s_b

    qc, vc = ctx[0], ctx[1]
    s0 = jnp.zeros((qc.shape[0], qc.shape[2], qc.shape[3], vc.shape[3]), jnp.float32)
    o_ctx, s_f, s_b = both_directions(*ctx, s0, s0)
    o_lat, _, _ = both_directions(*lat, s_f, s_b)
    return o_lat, o_ctx


def hgrn_lower_bounds(logits):
    cum = jnp.cumsum(jax.nn.softmax(logits.astype(jnp.float32), axis=0), axis=0)
    return cum - cum[0]


def hgrn2_mixer(cols_lat, cols_ctx, lb, norm_g, with_ctx_out):
    def prep(q, i, f_fwd, f_bwd):
        q = split_heads(jax.nn.silu(q), A_HEADS) * (A_DH ** -0.5)
        v = split_heads(i, A_HEADS)
        dirs = []
        for f_logit, lbd in ((f_fwd, lb[0]), (f_bwd, lb[1])):
            z = f_logit.astype(jnp.float32)
            log_f = jnp.logaddexp(jnp.log(lbd), jnp.log1p(-lbd) + jax.nn.log_sigmoid(z))
            k = (1.0 - lbd) * jax.nn.sigmoid(-z)
            dirs += [split_heads(k, A_HEADS), split_heads(log_f, A_HEADS)]
        return (q, v, *dirs)

    o_lat, o_ctx = bidirectional_recurrence(prep(*cols_lat[:4]), prep(*cols_ctx[:4]))

    def readout(o, g):
        return merge_heads(rmsnorm(o, norm_g.reshape(A_HEADS, A_DH))) * jax.nn.silu(g)

    return readout(o_lat, cols_lat[4]), (readout(o_ctx, cols_ctx[4]) if with_ctx_out else None)


def gla_mixer(cols_lat, cols_ctx, wg_f, bg_f, wg_b, bg_b, norm_g, with_ctx_out):
    def prep(q, k, v, g, z_f, z_b):
        q = split_heads(q, C_HEADS) * (C_DK ** -0.5)
        k = split_heads(k, C_HEADS)
        v = split_heads(v, C_HEADS)
        la_f = split_heads(jax.nn.log_sigmoid((z_f @ wg_f + bg_f).astype(jnp.float32)) / C_GATE_NORM, C_HEADS)
        la_b = split_heads(jax.nn.log_sigmoid((z_b @ wg_b + bg_b).astype(jnp.float32)) / C_GATE_NORM, C_HEADS)
        return (q, v, k, la_f, k, la_b)

    o_lat, o_ctx = bidirectional_recurrence(prep(*cols_lat), prep(*cols_ctx))

    def readout(o, g):
        return merge_heads(rmsnorm(o, norm_g.reshape(C_HEADS, C_DV))) * jax.nn.silu(g)

    return readout(o_lat, cols_lat[3]), (readout(o_ctx, cols_ctx[3]) if with_ctx_out else None)


def neighbourhood_attention_mixer(cols_lat, cols_ctx, rpb, with_ctx_out):
    q, k, v = (split_heads(u, B_HEADS) for u in cols_lat)
    qc, kc, vc = (split_heads(u, B_HEADS) for u in cols_ctx)
    b, n = q.shape[:2]
    rows_n = n // GRID_W
    kh = min(WIN_H, rows_n)
    scale = B_DH ** -0.5
    r = jnp.arange(rows_n)
    row_idx = jnp.clip(r - kh // 2, 0, rows_n - kh)[:, None] + jnp.arange(kh)[None]
    cidx = jnp.arange(GRID_W)
    c_start = jnp.clip(cidx - WIN_W // 2, 0, GRID_W - WIN_W)
    col_in = (cidx[None] >= c_start[:, None]) & (cidx[None] < c_start[:, None] + WIN_W)
    dr = row_idx - r[:, None] + (WIN_H - 1)
    dc = jnp.clip(cidx[None] - cidx[:, None], -(WIN_W - 1), WIN_W - 1) + (WIN_W - 1)
    bias = rpb[:, dr[:, None, :, None], dc[None, :, None, :]].astype(jnp.float32)

    qg = q.reshape(b, rows_n, GRID_W, B_HEADS, B_DH) * scale
    kg = k.reshape(b, rows_n, GRID_W, B_HEADS, B_DH)[:, row_idx]
    vg = v.reshape(b, rows_n, GRID_W, B_HEADS, B_DH)[:, row_idx]
    s_win = jnp.einsum('brqhd,brjwhd->bhrqjw', qg, kg).astype(jnp.float32) + bias[None]
    s_win = jnp.where(col_in[:, None, :], s_win, -jnp.inf)
    s_ctx = jnp.einsum('brqhd,blhd->bhrql', qg, kc).astype(jnp.float32)
    n_win = kh * GRID_W
    s = jnp.concatenate([s_win.reshape(b, B_HEADS, rows_n, GRID_W, n_win), s_ctx], axis=-1)
    p = jax.nn.softmax(s, axis=-1).astype(v.dtype)
    p_win = p[..., :n_win].reshape(b, B_HEADS, rows_n, GRID_W, kh, GRID_W)
    o = (jnp.einsum('bhrqjw,brjwhd->brqhd', p_win, vg)
         + jnp.einsum('bhrql,blhd->brqhd', p[..., n_win:], vc))
    o_lat = o.reshape(b, n, D_GROUP)
    o_ctx = None
    if with_ctx_out:
        s_cc = jnp.einsum('bqhd,blhd->bhql', qc * scale, kc).astype(jnp.float32)
        o_ctx = merge_heads(jnp.einsum('bhql,blhd->bqhd', jax.nn.softmax(s_cc, axis=-1).astype(vc.dtype), vc))
    return o_lat, o_ctx


def mla_attend(qn, qr, kn, kr, v):
    s = (jnp.einsum('bqhd,bkhd->bhqk', qn, kn) + jnp.einsum('bqhr,bkr->bhqk', qr, kr)).astype(jnp.float32) * MLA_SCALE
    p = jax.nn.softmax(s, axis=-1).astype(v.dtype)
    return jnp.einsum('bhqk,bkhd->bqhd', p, v)


def mla_mixer(cols_lat, cols_ctx, q_norm_g, w_uq, kv_norm_g, w_ukv, cos, sin, with_ctx_out):
    def prep(cq, ckv, k_rope):
        q = split_heads(rmsnorm(cq, q_norm_g) @ w_uq, D_HEADS)
        kv = split_heads(rmsnorm(ckv, kv_norm_g) @ w_ukv, D_HEADS)
        return q[..., :D_NOPE], q[..., D_NOPE:], kv[..., :D_NOPE], k_rope, kv[..., D_NOPE:]

    qn, qr, kn, kr, v = prep(*cols_lat)
    qr = rope_2d(qr, cos[:, None], sin[:, None])
    kr = rope_2d(kr, cos, sin)
    qnc, qrc, knc, krc, vc = prep(*cols_ctx)
    kn_all = jnp.concatenate([knc, kn], axis=1)
    kr_all = jnp.concatenate([krc, kr], axis=1)
    v_all = jnp.concatenate([vc, v], axis=1)
    b, n = qn.shape[:2]
    nb = n // Q_BLOCK

    def blocks(u):
        return jnp.moveaxis(u.reshape((b, nb, Q_BLOCK) + u.shape[2:]), 1, 0)

    o = lax.map(lambda qs: mla_attend(qs[0], qs[1], kn_all, kr_all, v_all), (blocks(qn), blocks(qr)))
    o_lat = jnp.moveaxis(o, 0, 1).reshape(b, n, D_GROUP)
    o_ctx = merge_heads(mla_attend(qnc, qrc, knc, krc, vc)) if with_ctx_out else None
    return o_lat, o_ctx


def hierarchical_moe(h, w_rg, b_rg, w_re, b_re, w_gu, w_dn):
    t_tok, d = h.shape
    hf = h.astype(jnp.float32)
    g_logits = hf @ w_rg.astype(jnp.float32) + b_rg.astype(jnp.float32)
    _, g_sel = lax.top_k(g_logits, 1)
    p_group = jnp.take_along_axis(jax.nn.softmax(g_logits, axis=-1), g_sel, axis=-1)
    e_logits = (hf @ w_re.astype(jnp.float32) + b_re.astype(jnp.float32)).reshape(t_tok, N_GROUPS, EXPERTS_PER_GROUP)
    e_in_group = jnp.take_along_axis(e_logits, g_sel[:, :, None], axis=1)[:, 0]
    w_top, i_top = lax.top_k(jax.nn.softmax(e_in_group, axis=-1), TOP_K)
    w_top = p_group * w_top / jnp.sum(w_top, axis=-1, keepdims=True)
    expert_id = (g_sel * EXPERTS_PER_GROUP + i_top).reshape(-1)
    n_assign = t_tok * TOP_K
    token_id = jnp.repeat(jnp.arange(t_tok), TOP_K)
    order = jnp.argsort(expert_id)
    e_sorted, tok_sorted, w_sorted = expert_id[order], token_id[order], w_top.reshape(-1)[order]
    counts = jnp.zeros((N_EXPERTS,), jnp.int32).at[expert_id].add(1)
    starts = jnp.cumsum(counts) - counts
    padded = (counts + MOE_BLOCK - 1) // MOE_BLOCK * MOE_BLOCK
    padded_end = jnp.cumsum(padded)
    dest = (padded_end - padded)[e_sorted] + jnp.arange(n_assign) - starts[e_sorted]
    n_blocks = -(-n_assign // MOE_BLOCK) + N_EXPERTS
    xs = jnp.zeros((n_blocks * MOE_BLOCK, d), h.dtype).at[dest].set(h[tok_sorted])
    block_expert = jnp.minimum(jnp.searchsorted(padded_end, jnp.arange(n_blocks) * MOE_BLOCK, side='right'), N_EXPERTS - 1)

    def expert_block(args):
        xb, e = args
        gu = xb @ w_gu[e]
        return (jax.nn.silu(gu[:, :D_EXPERT]) * gu[:, D_EXPERT:]) @ w_dn[e]

    ys = lax.map(expert_block, (xs.reshape(n_blocks, MOE_BLOCK, d), block_expert)).reshape(-1, d)
    y = jnp.zeros((t_tok, d), jnp.float32).at[tok_sorted].add(ys[dest].astype(jnp.float32) * w_sorted[:, None])
    return y.astype(h.dtype)


def setup_inputs(seed: int = 0) -> dict:
    key = jax.random.key(seed)
    ks = iter(jax.random.split(key, 32))

    def normal(shape, scale):
        return scale * jax.random.normal(next(ks), shape, jnp.float32)

    def gain(shape):
        return 1.0 + normal(shape, 0.05)

    D = D_MODEL
    return {
        'x': normal((BATCH, SEQ, D), 1.0),
        'c': normal((BATCH, D), 1.0),
        'ctx': normal((BATCH, CTX_LEN, D), 1.0),
        'c_ctx': normal((D,), 1.0),
        'w_mod': normal((DEPTH, D, 6 * D), 0.5 * D ** -0.5),
        'b_mod': normal((DEPTH, 6 * D), 0.02),
        'norm1_g': gain((DEPTH, D)),
        'norm2_g': gain((DEPTH, D)),
        'w_in': normal((DEPTH, D, D_IN), D ** -0.5),
        'w_out': normal((DEPTH, D_MIX, D), D_MIX ** -0.5),
        'hgrn_lb_logits': normal((DEPTH, 2, D_GROUP), 0.5),
        'hgrn_norm_g': gain((DEPTH, D_GROUP)),
        'na_rpb': normal((DEPTH, B_HEADS, 2 * WIN_H - 1, 2 * WIN_W - 1), 0.5),
        'gla_wg_f': normal((DEPTH, C_GATE_RANK, C_HEADS * C_DK), C_GATE_RANK ** -0.5),
        'gla_bg_f': normal((DEPTH, C_HEADS * C_DK), 0.1),
        'gla_wg_b': normal((DEPTH, C_GATE_RANK, C_HEADS * C_DK), C_GATE_RANK ** -0.5),
        'gla_bg_b': normal((DEPTH, C_HEADS * C_DK), 0.1),
        'gla_norm_g': gain((DEPTH, D_GROUP)),
        'mla_q_norm_g': gain((DEPTH, D_Q_RANK)),
        'mla_w_uq': normal((DEPTH, D_Q_RANK, D_HEADS * (D_NOPE + D_ROPE)), D_Q_RANK ** -0.5),
        'mla_kv_norm_g': gain((DEPTH, D_KV_RANK)),
        'mla_w_ukv': normal((DEPTH, D_KV_RANK, D_HEADS * (D_NOPE + D_V)), D_KV_RANK ** -0.5),
        'moe_w_rg': normal((DEPTH, D, N_GROUPS), D ** -0.5),
        'moe_b_rg': normal((DEPTH, N_GROUPS), 0.01),
        'moe_w_re': normal((DEPTH, D, N_EXPERTS), D ** -0.5),
        'moe_b_re': normal((DEPTH, N_EXPERTS), 0.01),
        'moe_w_gu': normal((DEPTH, N_EXPERTS, D, 2 * D_EXPERT), D ** -0.5),
        'moe_w_dn': normal((DEPTH, N_EXPERTS, D_EXPERT, D), D_EXPERT ** -0.5),
        'final_norm_g': gain((D,)),
    }


def reference(x, c, ctx, c_ctx, w_mod, b_mod, norm1_g, norm2_g, w_in, w_out, hgrn_lb_logits, hgrn_norm_g,
              na_rpb, gla_wg_f, gla_bg_f, gla_wg_b, gla_bg_b, gla_norm_g, mla_q_norm_g, mla_w_uq,
              mla_kv_norm_g, mla_w_ukv, moe_w_rg, moe_b_rg, moe_w_re, moe_b_re, moe_w_gu, moe_w_dn, final_norm_g):
    b, n, _ = x.shape
    l_ctx = ctx.shape[1]
    t = jnp.arange(n)
    inv_freq = ROPE_BASE ** (-jnp.arange(ROPE_FREQS, dtype=jnp.float32) / ROPE_FREQS)
    ang = jnp.stack([(t // GRID_W).astype(jnp.float32)[:, None] * inv_freq,
                     (t % GRID_W).astype(jnp.float32)[:, None] * inv_freq], axis=1)
    cos, sin = jnp.cos(ang), jnp.sin(ang)
    lower_bounds = hgrn_lower_bounds(hgrn_lb_logits)
    c_act = jax.nn.silu(c)
    c_ctx_act = jax.nn.silu(c_ctx)
    xc = ctx
    for layer in range(DEPTH):
        keep_ctx = layer < DEPTH - 1
        m_lat = jnp.split((c_act @ w_mod[layer] + b_mod[layer])[:, None, :], 6, axis=-1)
        m_ctx = jnp.split(c_ctx_act @ w_mod[layer] + b_mod[layer], 6, axis=-1)
        h = rmsnorm(x, norm1_g[layer]) * (1.0 + m_lat[1]) + m_lat[0]
        hc = rmsnorm(xc, norm1_g[layer]) * (1.0 + m_ctx[1]) + m_ctx[0]
        p = split_projection(h @ w_in[layer])
        pc = split_projection(hc @ w_in[layer])
        oa, oac = hgrn2_mixer(p[0:5], pc[0:5], lower_bounds[layer], hgrn_norm_g[layer], keep_ctx)
        ob, obc = neighbourhood_attention_mixer(p[5:8], pc[5:8], na_rpb[layer], keep_ctx)
        og, ogc = gla_mixer(p[8:14], pc[8:14], gla_wg_f[layer], gla_bg_f[layer], gla_wg_b[layer], gla_bg_b[layer],
                            gla_norm_g[layer], keep_ctx)
        od, odc = mla_mixer(p[14:17], pc[14:17], mla_q_norm_g[layer], mla_w_uq[layer], mla_kv_norm_g[layer],
                            mla_w_ukv[layer], cos, sin, keep_ctx)
        x = x + m_lat[2] * (jnp.concatenate([oa, ob, og, od], axis=-1) @ w_out[layer])
        h2 = rmsnorm(x, norm2_g[layer]) * (1.0 + m_lat[4]) + m_lat[3]
        moe_args = (moe_w_rg[layer], moe_b_rg[layer], moe_w_re[layer], moe_b_re[layer], moe_w_gu[layer], moe_w_dn[layer])
        if keep_ctx:
            xc = xc + m_ctx[2] * (jnp.concatenate([oac, obc, ogc, odc], axis=-1) @ w_out[layer])
            h2c = rmsnorm(xc, norm2_g[layer]) * (1.0 + m_ctx[4]) + m_ctx[3]
            y = hierarchical_moe(jnp.concatenate([h2c.reshape(-1, D_MODEL), h2.reshape(-1, D_MODEL)], axis=0), *moe_args)
            xc = xc + m_ctx[5] * y[: b * l_ctx].reshape(b, l_ctx, D_MODEL)
            x = x + m_lat[5] * y[b * l_ctx:].reshape(b, n, D_MODEL)
        else:
            y = hierarchical_moe(h2.reshape(-1, D_MODEL), *moe_args)
            x = x + m_lat[5] * y.reshape(b, n, D_MODEL)
    return rmsnorm(x, final_norm_g)
```

```python
import functools

import numpy as np
import jax
import jax.numpy as jnp
from jax import lax
from jax.experimental import pallas as pl
from jax.experimental.pallas import tpu as pltpu

F32 = jnp.float32
BF16 = jnp.bfloat16
I32 = jnp.int32

D_MODEL = 1024
DEPTH = 2
GRID_W = 64
EPS = 1e-6
D_GROUP = 256
N_HEADS = 4
A_DH = 64
B_DH = 64
WIN_H = 8
WIN_W = 16
C_DK = 32
C_DV = 64
C_GATE_RANK = 16
C_GATE_NORM = 16.0
D_NOPE = 64
D_V = 64
D_ROPE = 32
ROPE_FREQS = 8
ROPE_BASE = 10000.0
D_Q_RANK = 192
D_KV_RANK = 128
MLA_SCALE = (D_NOPE + D_ROPE) ** -0.5
N_GROUPS = 4
EXPERTS_PER_GROUP = 8
D_EXPERT = 256

ROW_BLOCK = 256
SUB = 16
LANE = 128
HEAD_TILE = 128
W_A, W_B, W_C, W_D = 1280, 768, 896, 512
XG_W = D_MODEL + LANE
VMEM_LIMIT = 52 * 1024 * 1024
NEG_INF = float("-inf")


def _bdot(a, b):
    return jnp.dot(a.astype(BF16), b.astype(BF16), preferred_element_type=F32)


def _bdot_nt(a, b):
    return lax.dot_general(a.astype(BF16), b.astype(BF16), (((1,), (1,)), ((), ())), preferred_element_type=F32)


def _bdot_tn(a, b):
    return lax.dot_general(a.astype(BF16), b.astype(BF16), (((0,), (0,)), ((), ())), preferred_element_type=F32)


def _split3(a):
    hi = a.astype(BF16)
    r1 = a - hi.astype(F32)
    mid = r1.astype(BF16)
    lo = (r1 - mid.astype(F32)).astype(BF16)
    return hi, mid, lo


def _dot_f32(a, b):
    ah, am, al = _split3(a)
    bh, bm, bl = _split3(b)
    d = lambda u, v: jnp.dot(u, v, preferred_element_type=F32)
    return d(ah, bh) + (d(ah, bm) + d(am, bh)) + (d(am, bm) + d(ah, bl) + d(al, bh))


def _dot_sel_l(sel, a):
    ah, am, al = _split3(a)
    d = lambda v: jnp.dot(sel, v, preferred_element_type=F32)
    return d(ah) + d(am) + d(al)


def _dot_sel_r(a, sel):
    ah, am, al = _split3(a)
    d = lambda u: jnp.dot(u, sel, preferred_element_type=F32)
    return d(ah) + d(am) + d(al)


def _sigmoid(x):
    return 1.0 / (1.0 + jnp.exp(-x))


def _log_sigmoid(x):
    return jnp.minimum(x, 0.0) - jnp.log1p(jnp.exp(-jnp.abs(x)))


def _logaddexp(a, b):
    amax = jnp.maximum(a, b)
    delta = a - b
    return jnp.where(jnp.isnan(delta), a + b, amax + jnp.log1p(jnp.exp(-jnp.abs(delta))))


def _rms(x, width):
    return x * lax.rsqrt(jnp.sum(x * x, axis=-1, keepdims=True) / width + EPS)


def _cparams(sem, vmem=VMEM_LIMIT):
    return pltpu.CompilerParams(dimension_semantics=sem, vmem_limit_bytes=vmem)


def _mod_kernel(c_ref, w_ref, b_ref, o_ref):
    c = c_ref[...]
    act = c * _sigmoid(c)
    o_ref[0] = _dot_f32(act, w_ref[0]) + b_ref[0]


def _mod_vectors(c8, w_mod, b_mod):
    depth, d, six_d = w_mod.shape
    nj = six_d // d
    return pl.pallas_call(
        _mod_kernel,
        grid=(depth, nj),
        in_specs=[
            pl.BlockSpec((8, d), lambda l, j: (0, 0)),
            pl.BlockSpec((1, d, d), lambda l, j: (l, 0, j)),
            pl.BlockSpec((1, 1, d), lambda l, j: (l, 0, j)),
        ],
        out_specs=pl.BlockSpec((1, 8, d), lambda l, j: (l, 0, j)),
        out_shape=jax.ShapeDtypeStruct((depth, 8, six_d), F32),
        compiler_params=_cparams(("arbitrary", "arbitrary")),
        name="mod_vectors",
    )(c8, w_mod, b_mod.reshape(depth, 1, six_d))


def _mod_row(m_ref, row):
    return m_ref[pl.ds(row, 1), :]


def _inproj_kernel(*refs, fuse_res, nblk):
    if fuse_res:
        x_ref, y_ref, mprev_ref, m_ref, g_ref, w_ref, xo_ref, pa_ref, pb_ref, pc_ref, pd_ref = refs
    else:
        x_ref, m_ref, g_ref, w_ref, pa_ref, pb_ref, pc_ref, pd_ref = refs
    i = pl.program_id(0)
    b = i // nblk
    row = jnp.where(i % nblk == 0, 4, b)
    d = D_MODEL
    x = x_ref[...]
    if fuse_res:
        mp = _mod_row(mprev_ref, row)
        x = x + mp[:, 5 * d:6 * d] * y_ref[...]
        xo_ref[...] = x
    m = _mod_row(m_ref, row)
    h = _rms(x, d) * g_ref[...] * (1.0 + m[:, d:2 * d]) + m[:, 0:d]
    p = _bdot(h, w_ref[...])
    pa_ref[...] = p[:, 0:W_A]
    pb_ref[...] = p[:, W_A:W_A + W_B]
    pc_ref[...] = p[:, W_A + W_B:W_A + W_B + W_C]
    pd_ref[...] = p[:, W_A + W_B + W_C:]


def _inproj(x, y, m_prev, m, g, w, nblk):
    t, d = x.shape
    fuse = y is not None
    rb = ROW_BLOCK
    row_spec = lambda w_: pl.BlockSpec((rb, w_), lambda i: (i, 0))
    full = lambda a: pl.BlockSpec(a.shape, lambda i: (0,) * a.ndim)
    ins, specs = [x], [row_spec(d)]
    if fuse:
        ins += [y, m_prev]
        specs += [row_spec(d), full(m_prev)]
    ins += [m, g, w]
    specs += [full(m), full(g), full(w)]
    outs, ospecs = [], []
    if fuse:
        outs.append(jax.ShapeDtypeStruct((t, d), F32))
        ospecs.append(row_spec(d))
    for w_ in (W_A, W_B, W_C, W_D):
        outs.append(jax.ShapeDtypeStruct((t, w_), F32))
        ospecs.append(row_spec(w_))
    res = pl.pallas_call(
        functools.partial(_inproj_kernel, fuse_res=fuse, nblk=nblk),
        grid=(t // rb,),
        in_specs=specs,
        out_specs=ospecs,
        out_shape=outs,
        compiler_params=_cparams(("arbitrary",)),
        name="in_projection",
    )(*ins)
    if fuse:
        return res[0], res[1:]
    return x, res


def _scan_block(q_ref, k_ref, la_ref, v_ref, o_ref, st_ref, r_ref, tri, emat, emask_t, reverse):
    n_sub = q_ref.shape[0] // SUB
    hv = v_ref.shape[1]
    row_id = lax.broadcasted_iota(I32, (SUB, 1), 0)

    def body(step, carry):
        i = (n_sub - 1 - step) if reverse else step
        off = pl.multiple_of(i * SUB, SUB)
        qs = q_ref[pl.ds(off, SUB), :]
        ks = k_ref[pl.ds(off, SUB), :]
        las = la_ref[pl.ds(off, SUB), :]
        vs = v_ref[pl.ds(off, SUB), :]
        cum = _dot_sel_l(tri, las)
        last = cum[0:1, :] if reverse else cum[SUB - 1:SUB, :]
        for j in range(SUB):
            valid = (row_id <= j) if reverse else (row_id >= j)
            dlt = jnp.where(valid, cum - cum[j:j + 1, :], NEG_INF)
            r_ref[j * SUB:(j + 1) * SUB, :] = (qs * ks[j:j + 1, :] * jnp.exp(dlt)).astype(BF16)
        att = jnp.dot(r_ref[...], emat, preferred_element_type=F32)
        o = jnp.zeros((SUB, hv), F32)
        for j in range(SUB):
            o = o + att[j * SUB:(j + 1) * SUB, :] * vs[j:j + 1, :]
        st = st_ref[...]
        o = o + _bdot_nt(qs * jnp.exp(cum), st)
        o_ref[pl.ds(off, SUB), :] = o
        kd = ks * jnp.exp(last - cum)
        st_ref[...] = st * jnp.exp(last) + _bdot_tn(vs, kd) * emask_t
        return carry

    lax.fori_loop(0, n_sub, body, 0)


def _hgrn_prep(q_ref, v_ref, f_ref, lb, qo, ko, lo, vo):
    qr = q_ref[...]
    qo[...] = qr * _sigmoid(qr) * (A_DH ** -0.5)
    vo[...] = v_ref[...]
    z = f_ref[...]
    lo[...] = _logaddexp(jnp.log(lb), jnp.log1p(-lb) + _log_sigmoid(z))
    ko[...] = (1.0 - lb) * _sigmoid(-z)


def _hgrn_kernel(qf_ref, vf_ref, ff_ref, qb_ref, vb_ref, fb_ref, lb_ref, trif_ref, trib_ref, emat_ref, emask_ref,
                 of_ref, ob_ref, stf, stb, rf, rb_, q1, k1, l1, v1, q2, k2, l2, v2):
    @pl.when(pl.program_id(1) == 0)
    def _():
        stf[...] = jnp.zeros_like(stf)
        stb[...] = jnp.zeros_like(stb)

    _hgrn_prep(qf_ref, vf_ref, ff_ref, lb_ref[0:1, :], q1, k1, l1, v1)
    _hgrn_prep(qb_ref, vb_ref, fb_ref, lb_ref[1:2, :], q2, k2, l2, v2)
    emat = emat_ref[...]
    emask = emask_ref[...]
    _scan_block(q1, k1, l1, v1, of_ref, stf, rf, trif_ref[...], emat, emask, False)
    _scan_block(q2, k2, l2, v2, ob_ref, stb, rb_, trib_ref[...], emat, emask, True)


def _tri_consts():
    i = np.arange(SUB)
    trif = (i[None, :] <= i[:, None]).astype(np.float32)
    trib = (i[None, :] >= i[:, None]).astype(np.float32)
    return jnp.asarray(trif, BF16), jnp.asarray(trib, BF16)


def _head_match(hk, hv, dk, dv):
    m = (np.arange(hk)[:, None] // dk == np.arange(hv)[None, :] // dv).astype(np.float32)
    return m


def _bwd_block(t, nblk):
    return jnp.where(t == 0, 0, nblk - t)


def _hgrn_scan(pa, lb, batch, nblk):
    t_all = pa.shape[0]
    rb, w = ROW_BLOCK, D_GROUP
    trif, trib = _tri_consts()
    em = _head_match(w, w, A_DH, A_DH)
    emat, emask = jnp.asarray(em, BF16), jnp.asarray(em.T, F32)
    fwd = lambda col: pl.BlockSpec((rb, w), lambda b, t: (b * nblk + t, col))
    bwd = lambda col: pl.BlockSpec((rb, w), lambda b, t: (b * nblk + _bwd_block(t, nblk), col))
    full = lambda a: pl.BlockSpec(a.shape, lambda b, t: (0,) * a.ndim)
    vm = lambda shape, dt=F32: pltpu.VMEM(shape, dt)
    return pl.pallas_call(
        _hgrn_kernel,
        grid=(batch, nblk),
        in_specs=[fwd(0), fwd(1), fwd(2), bwd(0), bwd(1), bwd(3), full(lb), full(trif), full(trib), full(emat), full(emask)],
        out_specs=[fwd(0), bwd(0)],
        out_shape=[jax.ShapeDtypeStruct((t_all, w), F32)] * 2,
        scratch_shapes=[vm((w, w)), vm((w, w)), vm((SUB * SUB, w), BF16), vm((SUB * SUB, w), BF16)]
        + [vm((rb, w))] * 8,
        compiler_params=_cparams(("arbitrary", "arbitrary")),
        name="hgrn_scan",
    )(pa, pa, pa, pa, pa, pa, lb, trif, trib, emat, emask)


def _gla_prep(q_ref, k_ref, v_ref, z_ref, wg_ref, bg_ref, qo, ko, lo, vo):
    qo[...] = q_ref[...] * (C_DK ** -0.5)
    ko[...] = k_ref[...]
    vo[...] = v_ref[...]
    zl = _dot_f32(z_ref[...], wg_ref[...]) + bg_ref[...]
    lo[...] = _log_sigmoid(zl) / C_GATE_NORM


def _gla_kernel(qf_ref, kf_ref, vf_ref, zf_ref, qb_ref, kb_ref, vb_ref, zb_ref, wgf_ref, bgf_ref, wgb_ref, bgb_ref,
                trif_ref, trib_ref, emat_ref, emask_ref, of_ref, ob_ref, stf, stb, rf, rb_, q1, k1, l1, v1, q2, k2, l2, v2):
    @pl.when(pl.program_id(1) == 0)
    def _():
        stf[...] = jnp.zeros_like(stf)
        stb[...] = jnp.zeros_like(stb)

    _gla_prep(qf_ref, kf_ref, vf_ref, zf_ref, wgf_ref, bgf_ref, q1, k1, l1, v1)
    _gla_prep(qb_ref, kb_ref, vb_ref, zb_ref, wgb_ref, bgb_ref, q2, k2, l2, v2)
    emat = emat_ref[...]
    emask = emask_ref[...]
    _scan_block(q1, k1, l1, v1, of_ref, stf, rf, trif_ref[...], emat, emask, False)
    _scan_block(q2, k2, l2, v2, ob_ref, stb, rb_, trib_ref[...], emat, emask, True)


def _gla_scan(pc, wg_f, bg_f, wg_b, bg_b, batch, nblk):
    t_all = pc.shape[0]
    rb = ROW_BLOCK
    hk, hv = N_HEADS * C_DK, N_HEADS * C_DV
    trif, trib = _tri_consts()
    em = _head_match(hk, hv, C_DK, C_DV)
    emat, emask = jnp.asarray(em, BF16), jnp.asarray(em.T, F32)
    wgf = jnp.zeros((LANE, hk), F32).at[0:C_GATE_RANK].set(wg_f)
    wgb = jnp.zeros((LANE, hk), F32).at[C_GATE_RANK:2 * C_GATE_RANK].set(wg_b)
    bgf, bgb = bg_f.reshape(1, hk), bg_b.reshape(1, hk)
    fwd = lambda w, col: pl.BlockSpec((rb, w), lambda b, t: (b * nblk + t, col))
    bwd = lambda w, col: pl.BlockSpec((rb, w), lambda b, t: (b * nblk + _bwd_block(t, nblk), col))
    full = lambda a: pl.BlockSpec(a.shape, lambda b, t: (0,) * a.ndim)
    vm = lambda shape, dt=F32: pltpu.VMEM(shape, dt)
    return pl.pallas_call(
        _gla_kernel,
        grid=(batch, nblk),
        in_specs=[fwd(hk, 0), fwd(hk, 1), fwd(hv, 1), fwd(LANE, 6), bwd(hk, 0), bwd(hk, 1), bwd(hv, 1), bwd(LANE, 6),
                  full(wgf), full(bgf), full(wgb), full(bgb), full(trif), full(trib), full(emat), full(emask)],
        out_specs=[fwd(hv, 0), bwd(hv, 0)],
        out_shape=[jax.ShapeDtypeStruct((t_all, hv), F32)] * 2,
        scratch_shapes=[vm((hv, hk)), vm((hv, hk)), vm((SUB * SUB, hk), BF16), vm((SUB * SUB, hk), BF16)]
        + [vm((rb, hk)), vm((rb, hk)), vm((rb, hk)), vm((rb, hv))] * 2,
        compiler_params=_cparams(("arbitrary", "arbitrary")),
        name="gla_scan",
    )(pc, pc, pc, pc, pc, pc, pc, pc, wgf, bgf, wgb, bgb, trif, trib, emat, emask)


def _na_kernel(q_ref, k_ref, v_ref, bias_ref, hm_ref, o_ref, *, j0, rows, ctx):
    j = pl.program_id(1) + j0
    n_ctx_blk = ctx // GRID_W
    q = q_ref[...] * (B_DH ** -0.5)
    hm = hm_ref[...]
    kc = k_ref[0:ctx, :].astype(BF16)
    vc = v_ref[0:ctx, :].astype(BF16)

    @pl.when(j < n_ctx_blk)
    def _():
        acc = jnp.zeros(q.shape, F32)
        for h in range(N_HEADS):
            mh = hm[h:h + 1, :]
            s = _bdot_nt(q * mh, kc)
            p = jnp.exp(s - jnp.max(s, axis=-1, keepdims=True))
            inv = 1.0 / jnp.sum(p, axis=-1, keepdims=True)
            acc = acc + _bdot(p, vc) * (mh * inv)
        o_ref[...] = acc.astype(o_ref.dtype)

    @pl.when(j >= n_ctx_blk)
    def _():
        r = j - n_ctx_blk
        start = jnp.clip(r - WIN_H // 2, 0, rows - WIN_H)
        off = pl.multiple_of(ctx + start * GRID_W, GRID_W)
        kw = k_ref[pl.ds(off, WIN_H * GRID_W), :].astype(BF16)
        vw = v_ref[pl.ds(off, WIN_H * GRID_W), :].astype(BF16)
        acc = jnp.zeros(q.shape, F32)
        for h in range(N_HEADS):
            mh = hm[h:h + 1, :]
            qh = q * mh
            sw = _bdot_nt(qh, kw) + bias_ref[0, h]
            sc = _bdot_nt(qh, kc)
            m = jnp.maximum(jnp.max(sw, axis=-1, keepdims=True), jnp.max(sc, axis=-1, keepdims=True))
            pw = jnp.exp(sw - m)
            pc_ = jnp.exp(sc - m)
            inv = 1.0 / (jnp.sum(pw, axis=-1, keepdims=True) + jnp.sum(pc_, axis=-1, keepdims=True))
            acc = acc + (_bdot(pw, vw) + _bdot(pc_, vc)) * (mh * inv)
        o_ref[...] = acc.astype(o_ref.dtype)


def _na_bias_table(rpb, rows):
    kh = WIN_H
    cidx = np.arange(GRID_W)
    c_start = np.clip(cidx - WIN_W // 2, 0, GRID_W - WIN_W)
    col_in = (cidx[None] >= c_start[:, None]) & (cidx[None] < c_start[:, None] + WIN_W)
    dc = np.clip(cidx[None] - cidx[:, None], -(WIN_W - 1), WIN_W - 1) + (WIN_W - 1)
    roff = np.arange(kh)
    dr = np.arange(kh)[None, :] - roff[:, None] + (WIN_H - 1)
    bias = rpb[:, dr[:, None, :, None], dc[None, :, None, :]].astype(F32)
    bias = jnp.where(jnp.asarray(col_in)[None, None, :, None, :], bias, NEG_INF)
    return jnp.transpose(bias, (1, 0, 2, 3, 4)).reshape(kh, N_HEADS, GRID_W, kh * GRID_W)


def _head_masks(width, dh):
    return jnp.asarray((np.arange(width)[None, :] // dh == np.arange(N_HEADS)[:, None]).astype(np.float32))


def _neighbourhood_attention(pb, rpb, batch, s_len, ctx, keep_ctx):
    t_all = pb.shape[0]
    rows = (s_len - ctx) // GRID_W
    assert rows >= WIN_H
    n_ctx_blk = ctx // GRID_W
    j0 = 0 if keep_ctx else n_ctx_blk
    per_b = s_len // GRID_W
    bias = _na_bias_table(rpb, rows)
    hm = _head_masks(D_GROUP, B_DH)

    def bias_idx(b, jj):
        r = jnp.maximum(jj + j0 - n_ctx_blk, 0)
        start = jnp.clip(r - WIN_H // 2, 0, rows - WIN_H)
        return (r - start, 0, 0, 0)

    return pl.pallas_call(
        functools.partial(_na_kernel, j0=j0, rows=rows, ctx=ctx),
        grid=(batch, per_b - j0),
        in_specs=[
            pl.BlockSpec((GRID_W, D_GROUP), lambda b, jj: (b * per_b + jj + j0, 0)),
            pl.BlockSpec((s_len, D_GROUP), lambda b, jj: (b, 1)),
            pl.BlockSpec((s_len, D_GROUP), lambda b, jj: (b, 2)),
            pl.BlockSpec((1, N_HEADS, GRID_W, WIN_H * GRID_W), bias_idx),
            pl.BlockSpec(hm.shape, lambda b, jj: (0, 0)),
        ],
        out_specs=pl.BlockSpec((GRID_W, D_GROUP), lambda b, jj: (b * (per_b - j0) + jj, 0)),
        out_shape=jax.ShapeDtypeStruct((batch * (per_b - j0) * GRID_W, D_GROUP), BF16),
        compiler_params=_cparams(("arbitrary", "arbitrary")),
        name="neighbourhood_attention",
    )(pb, pb, pb, bias, hm)


def _mla_prep_kernel(pd_ref, gq_ref, gkv_ref, wq1_ref, wq2_ref, wkv_ref, cq_ref, sq_ref, tk_ref, place_ref,
                     q_ref, k_ref, v_ref):
    pd = pd_ref[...]
    cq = pd[:, 0:256]
    ckv = pd[:, 256:384]
    kr = pd[:, 384:512]
    qn = _rms(cq, D_Q_RANK) * gq_ref[...]
    q = _bdot(qn, wq1_ref[...]) * cq_ref[...] + _bdot(qn, wq2_ref[...]) * sq_ref[...]
    q_ref[...] = q.astype(BF16)
    kvn = _rms(ckv, D_KV_RANK) * gkv_ref[...]
    kv = _bdot(kvn, wkv_ref[...])
    hw = N_HEADS * HEAD_TILE
    k = kv[:, 0:hw] + _dot_sel_r(kr * tk_ref[...], place_ref[...])
    k_ref[...] = k.astype(BF16)
    v_ref[...] = kv[:, hw:].astype(BF16)


def _rope_swap_perm():
    f = ROPE_FREQS
    return np.concatenate([np.arange(f, 2 * f), np.arange(0, f), np.arange(3 * f, 4 * f), np.arange(2 * f, 3 * f)])


def _mla_tables(n, ctx):
    t = np.arange(n)
    inv_freq = ROPE_BASE ** (-np.arange(ROPE_FREQS, dtype=np.float32) / ROPE_FREQS)
    ang_r = (t // GRID_W).astype(np.float32)[:, None] * inv_freq
    ang_c = (t % GRID_W).astype(np.float32)[:, None] * inv_freq
    cos32 = np.concatenate([np.cos(ang_r), np.cos(ang_r), np.cos(ang_c), np.cos(ang_c)], axis=1)
    sin32 = np.concatenate([-np.sin(ang_r), np.sin(ang_r), -np.sin(ang_c), np.sin(ang_c)], axis=1)
    cos32 = np.concatenate([np.ones((ctx, D_ROPE), np.float32), cos32.astype(np.float32)], axis=0)
    sin32 = np.concatenate([np.zeros((ctx, D_ROPE), np.float32), sin32.astype(np.float32)], axis=0)
    s_len = n + ctx
    cq = np.zeros((s_len, N_HEADS, HEAD_TILE), np.float32)
    sq = np.zeros((s_len, N_HEADS, HEAD_TILE), np.float32)
    cq[:, :, 0:D_NOPE] = MLA_SCALE
    cq[:, :, D_NOPE:D_NOPE + D_ROPE] = cos32[:, None, :] * MLA_SCALE
    sq[:, :, D_NOPE:D_NOPE + D_ROPE] = sin32[:, None, :] * MLA_SCALE
    tk = np.zeros((s_len, LANE), np.float32)
    tk[:, 0:D_ROPE] = cos32
    tk[:, D_ROPE:2 * D_ROPE] = sin32
    place = np.zeros((LANE, N_HEADS * HEAD_TILE), np.float32)
    for h in range(N_HEADS):
        for l in range(D_ROPE):
            place[l, h * HEAD_TILE + D_NOPE + l] = 1.0
            place[D_ROPE + l, h * HEAD_TILE + D_NOPE + l] = 1.0
    return (jnp.asarray(cq.reshape(s_len, -1)), jnp.asarray(sq.reshape(s_len, -1)), jnp.asarray(tk),
            jnp.asarray(place, BF16))


def _mla_weights(q_norm_g, w_uq, kv_norm_g, w_ukv):
    perm = _rope_swap_perm()
    wq = w_uq.reshape(D_Q_RANK, N_HEADS, D_NOPE + D_ROPE)
    wq1 = jnp.zeros((256, N_HEADS, HEAD_TILE), F32).at[0:D_Q_RANK, :, 0:D_NOPE + D_ROPE].set(wq)
    wq2 = jnp.zeros((256, N_HEADS, HEAD_TILE), F32).at[0:D_Q_RANK, :, D_NOPE:D_NOPE + D_ROPE].set(
        wq[:, :, D_NOPE:][:, :, perm])
    wkv = w_ukv.reshape(D_KV_RANK, N_HEADS, D_NOPE + D_V)
    wk = jnp.zeros((D_KV_RANK, N_HEADS, HEAD_TILE), F32).at[:, :, 0:D_NOPE].set(wkv[:, :, 0:D_NOPE])
    wv = jnp.zeros((D_KV_RANK, N_HEADS, HEAD_TILE), F32).at[:, :, 0:D_V].set(wkv[:, :, D_NOPE:])
    hw = N_HEADS * HEAD_TILE
    wkv_p = jnp.concatenate([wk.reshape(D_KV_RANK, hw), wv.reshape(D_KV_RANK, hw)], axis=1)
    gq = jnp.zeros((1, 256), F32).at[0, 0:D_Q_RANK].set(q_norm_g)
    return (gq, kv_norm_g.reshape(1, D_KV_RANK), wq1.reshape(256, hw).astype(BF16), wq2.reshape(256, hw).astype(BF16),
            wkv_p.astype(BF16))


def _mla_prep(pd, weights, tables, nblk):
    t_all = pd.shape[0]
    rb = ROW_BLOCK
    gq, gkv, wq1, wq2, wkv = weights
    cq, sq, tk, place = tables
    hw = N_HEADS * HEAD_TILE
    row = lambda w: pl.BlockSpec((rb, w), lambda i: (i, 0))
    pos = lambda w: pl.BlockSpec((rb, w), lambda i: (i % nblk, 0))
    full = lambda a: pl.BlockSpec(a.shape, lambda i: (0,) * a.ndim)
    return pl.pallas_call(
        _mla_prep_kernel,
        grid=(t_all // rb,),
        in_specs=[row(W_D), full(gq), full(gkv), full(wq1), full(wq2), full(wkv), pos(hw), pos(hw), pos(LANE), full(place)],
        out_specs=[row(hw)] * 3,
        out_shape=[jax.ShapeDtypeStruct((t_all, hw), BF16)] * 3,
        compiler_params=_cparams(("arbitrary",)),
        name="mla_prep",
    )(pd, gq, gkv, wq1, wq2, wkv, cq, sq, tk, place)


def _mla_attn_kernel(q_ref, k_ref, v_ref, o_ref, *, j0, ctx):
    j = pl.program_id(2) + j0
    q = q_ref[...]

    def attend(k, v):
        s = lax.dot_general(q, k, (((1,), (1,)), ((), ())), preferred_element_type=F32)
        p = jnp.exp(s - jnp.max(s, axis=-1, keepdims=True))
        inv = 1.0 / jnp.sum(p, axis=-1, keepdims=True)
        return (jnp.dot(p.astype(BF16), v, preferred_element_type=F32) * inv).astype(o_ref.dtype)

    @pl.when(j == 0)
    def _():
        o_ref[...] = attend(k_ref[0:ctx, :], v_ref[0:ctx, :])

    @pl.when(j > 0)
    def _():
        o_ref[...] = attend(k_ref[...], v_ref[...])


def _mla_attention(q, k, v, batch, s_len, ctx, keep_ctx):
    t_all = q.shape[0]
    rb = ROW_BLOCK
    nblk = s_len // rb
    j0 = 0 if keep_ctx else 1
    return pl.pallas_call(
        functools.partial(_mla_attn_kernel, j0=j0, ctx=ctx),
        grid=(batch, N_HEADS, nblk - j0),
        in_specs=[
            pl.BlockSpec((rb, HEAD_TILE), lambda b, h, jj: (b * nblk + jj + j0, h)),
            pl.BlockSpec((s_len, HEAD_TILE), lambda b, h, jj: (b, h)),
            pl.BlockSpec((s_len, HEAD_TILE), lambda b, h, jj: (b, h)),
        ],
        out_specs=pl.BlockSpec((rb, HEAD_TILE), lambda b, h, jj: (b * (nblk - j0) + jj, h)),
        out_shape=jax.ShapeDtypeStruct((batch * (nblk - j0) * rb, N_HEADS * HEAD_TILE), BF16),
        compiler_params=_cparams(("arbitrary", "arbitrary", "arbitrary")),
        name="mla_attention",
    )(q, k, v)


def _outproj_kernel(oaf_ref, oab_ref, ga_ref, ob_ref, ogf_ref, ogb_ref, gc_ref, od_ref, x_ref, m_ref, gha_ref, ghc_ref,
                    g2_ref, wa_ref, wb_ref, wc_ref, wd_ref, em_ref, x1_ref, h2_ref, *, j0, nblk_out):
    i = pl.program_id(0)
    b = i // nblk_out
    row = jnp.where(i % nblk_out + j0 == 0, 4, b)
    d = D_MODEL
    em = em_ref[...]

    def readout(o, g_norm, gate, dh):
        ms = _dot_sel_r(o * o, em) / dh
        return o * lax.rsqrt(ms + EPS) * g_norm * (gate * _sigmoid(gate))

    a = readout(oaf_ref[...] + oab_ref[...], gha_ref[...], ga_ref[...], A_DH)
    c = readout(ogf_ref[...] + ogb_ref[...], ghc_ref[...], gc_ref[...], C_DV)
    mix = (_bdot(a, wa_ref[...]) + jnp.dot(ob_ref[...], wb_ref[...], preferred_element_type=F32)
           + _bdot(c, wc_ref[...]) + jnp.dot(od_ref[...], wd_ref[...], preferred_element_type=F32))
    m = _mod_row(m_ref, row)
    x1 = x_ref[...] + m[:, 2 * d:3 * d] * mix
    x1_ref[...] = x1
    h2_ref[...] = _rms(x1, d) * g2_ref[...] * (1.0 + m[:, 4 * d:5 * d]) + m[:, 3 * d:4 * d]


def _outproj(oaf, oab, pa, ob, ogf, ogb, pc, od, x, m, gha, ghc, g2, w_out, batch, nblk, keep_ctx):
    rb, d = ROW_BLOCK, D_MODEL
    j0 = 0 if keep_ctx else 1
    nblk_out = nblk - j0
    t_out = batch * nblk_out * rb
    wa = w_out[0:256].astype(BF16)
    wb = w_out[256:512].astype(BF16)
    wc = w_out[512:768].astype(BF16)
    wd = jnp.zeros((N_HEADS, HEAD_TILE, d), F32).at[:, 0:D_V, :].set(w_out[768:1024].reshape(N_HEADS, D_V, d))
    wd = wd.reshape(N_HEADS * HEAD_TILE, d).astype(BF16)
    em = jnp.asarray(_head_match(D_GROUP, D_GROUP, 64, 64), BF16)
    src = lambda i: (i // nblk_out) * nblk + i % nblk_out + j0
    row_in = lambda w, col=0: pl.BlockSpec((rb, w), lambda i: (src(i), col))
    row_out = lambda w: pl.BlockSpec((rb, w), lambda i: (i, 0))
    full = lambda a_: pl.BlockSpec(a_.shape, lambda i: (0,) * a_.ndim)
    return pl.pallas_call(
        functools.partial(_outproj_kernel, j0=j0, nblk_out=nblk_out),
        grid=(t_out // rb,),
        in_specs=[row_in(256), row_in(256), row_in(256, 4), row_out(256), row_in(256), row_in(256), row_in(256, 2),
                  row_out(N_HEADS * HEAD_TILE), row_in(d), full(m), full(gha), full(ghc), full(g2),
                  full(wa), full(wb), full(wc), full(wd), full(em)],
        out_specs=[row_out(d), row_out(d)],
        out_shape=[jax.ShapeDtypeStruct((t_out, d), F32)] * 2,
        compiler_params=_cparams(("arbitrary",)),
        name="out_projection",
    )(oaf, oab, pa, ob, ogf, ogb, pc, od, x, m, gha, ghc, g2, wa, wb, wc, wd, em)


def _router_kernel(h_ref, wr_ref, br_ref, tril_ref, xg_ref, meta_ref, cnt_ref, carry):
    @pl.when(pl.program_id(0) == 0)
    def _():
        carry[...] = jnp.zeros_like(carry)

    h = h_ref[...]
    lg = _dot_f32(h, wr_ref[...]) + br_ref[...]
    lane = lax.broadcasted_iota(I32, lg.shape, 1).astype(F32)
    big = float(LANE)
    gl = jnp.where(lane < N_GROUPS, lg, NEG_INF)
    gmax = jnp.max(gl, axis=-1, keepdims=True)
    gsel = jnp.min(jnp.where(gl == gmax, lane, big), axis=-1, keepdims=True)
    p_group = 1.0 / jnp.sum(jnp.exp(gl - gmax), axis=-1, keepdims=True)
    lo = N_GROUPS + gsel * EXPERTS_PER_GROUP
    el = jnp.where((lane >= lo) & (lane < lo + EXPERTS_PER_GROUP), lg, NEG_INF)
    m1 = jnp.max(el, axis=-1, keepdims=True)
    i1 = jnp.min(jnp.where(el == m1, lane, big), axis=-1, keepdims=True)
    el2 = jnp.where(lane == i1, NEG_INF, el)
    m2 = jnp.max(el2, axis=-1, keepdims=True)
    i2 = jnp.min(jnp.where(el2 == m2, lane, big), axis=-1, keepdims=True)
    t = jnp.exp(m2 - m1)
    w1 = p_group / (1.0 + t)
    w2 = p_group * t / (1.0 + t)
    gates = jnp.where(lane == i1 - lo, w1, jnp.where(lane == i2 - lo, w2, 0.0))
    d = h.shape[1]
    xg_ref[:, 0:d] = h
    xg_ref[:, d:] = gates

    onehot = jnp.where(lane == gsel, 1.0, 0.0)
    incl = jnp.dot(tril_ref[...], onehot.astype(BF16), preferred_element_type=F32)
    rank = jnp.sum(onehot * (incl - 1.0 + carry[...]), axis=-1, keepdims=True)
    carry[...] = carry[...] + jnp.sum(onehot, axis=0, keepdims=True)
    meta_ref[...] = jnp.where(lane == 0, gsel, jnp.where(lane == 1, rank, 0.0)).astype(I32)
    cnt_ref[...] = carry[...].astype(I32)


def _router(h2, w_rg, b_rg, w_re, b_re):
    t, d = h2.shape
    rb = ROW_BLOCK
    ne = N_GROUPS * EXPERTS_PER_GROUP
    wr = jnp.zeros((d, LANE), F32).at[:, 0:N_GROUPS].set(w_rg).at[:, N_GROUPS:N_GROUPS + ne].set(w_re)
    br = jnp.zeros((1, LANE), F32).at[0, 0:N_GROUPS].set(b_rg).at[0, N_GROUPS:N_GROUPS + ne].set(b_re)
    tril = jnp.asarray(np.tril(np.ones((rb, rb), np.float32)), BF16)
    full = lambda a: pl.BlockSpec(a.shape, lambda i: (0,) * a.ndim)
    return pl.pallas_call(
        _router_kernel,
        grid=(t // rb,),
        in_specs=[pl.BlockSpec((rb, d), lambda i: (i, 0)), full(wr), full(br), full(tril)],
        out_specs=[pl.BlockSpec((rb, XG_W), lambda i: (i, 0)), pl.BlockSpec((rb, LANE), lambda i: (i, 0)),
                   pl.BlockSpec((1, LANE), lambda i: (0, 0))],
        out_shape=[jax.ShapeDtypeStruct((t, XG_W), F32), jax.ShapeDtypeStruct((t, LANE), I32),
                   jax.ShapeDtypeStruct((1, LANE), I32)],
        scratch_shapes=[pltpu.VMEM((1, LANE), F32)],
        compiler_params=_cparams(("arbitrary",)),
        name="moe_router",
    )(h2, wr, br, tril)


def _perm_chunk(t):
    return next(c for c in (1024, 512, 256) if t % c == 0)


def _row_copy(src_ref, dst_ref, s, d, sem):
    return pltpu.make_async_copy(src_ref.at[pl.ds(s, 1), :], dst_ref.at[pl.ds(d, 1), :], sem)


def _scatter_rows_kernel(dest_ref, src_ref, init_ref, dst_ref, sem, *, chunk):
    del init_ref
    base = pl.program_id(0) * chunk

    def issue(t, c):
        _row_copy(src_ref, dst_ref, base + t, dest_ref[base + t], sem).start()
        return c

    lax.fori_loop(0, chunk, issue, 0)

    def drain(t, c):
        _row_copy(src_ref, dst_ref, base + t, dest_ref[base + t], sem).wait()
        return c

    lax.fori_loop(0, chunk, drain, 0)


def _gather_rows_kernel(dest_ref, src_ref, dst_ref, sem, *, chunk):
    base = pl.program_id(0) * chunk

    def issue(t, c):
        _row_copy(src_ref, dst_ref, dest_ref[base + t], base + t, sem).start()
        return c

    lax.fori_loop(0, chunk, issue, 0)

    def drain(t, c):
        _row_copy(src_ref, dst_ref, dest_ref[base + t], base + t, sem).wait()
        return c

    lax.fori_loop(0, chunk, drain, 0)


def _scatter_rows(dest, src, n_dst):
    t, w = src.shape
    init = jnp.zeros((n_dst, w), src.dtype)
    any_spec = pl.BlockSpec(memory_space=pl.ANY)
    chunk = _perm_chunk(t)
    return pl.pallas_call(
        functools.partial(_scatter_rows_kernel, chunk=chunk),
        grid_spec=pltpu.PrefetchScalarGridSpec(
            num_scalar_prefetch=1, grid=(t // chunk,), in_specs=[any_spec, any_spec], out_specs=any_spec,
            scratch_shapes=[pltpu.SemaphoreType.DMA(())]),
        out_shape=jax.ShapeDtypeStruct((n_dst, w), src.dtype),
        input_output_aliases={2: 0},
        compiler_params=pltpu.CompilerParams(dimension_semantics=("arbitrary",), has_side_effects=True),
        name="moe_dispatch",
    )(dest, src, init)


def _gather_rows(dest, src, t):
    w = src.shape[1]
    any_spec = pl.BlockSpec(memory_space=pl.ANY)
    chunk = _perm_chunk(t)
    return pl.pallas_call(
        functools.partial(_gather_rows_kernel, chunk=chunk),
        grid_spec=pltpu.PrefetchScalarGridSpec(
            num_scalar_prefetch=1, grid=(t // chunk,), in_specs=[any_spec], out_specs=any_spec,
            scratch_shapes=[pltpu.SemaphoreType.DMA(())]),
        out_shape=jax.ShapeDtypeStruct((t, w), src.dtype),
        compiler_params=pltpu.CompilerParams(dimension_semantics=("arbitrary",), has_side_effects=True),
        name="moe_combine",
    )(dest, src)


def _experts_kernel(bg_ref, nv_ref, xs_ref, wgu_ref, wdn_ref, ys_ref):
    i = pl.program_id(0)
    d = D_MODEL

    @pl.when(i < nv_ref[0])
    def _():
        x = xs_ref[:, 0:d].astype(BF16)
        gates = xs_ref[:, d:]
        acc = jnp.zeros((x.shape[0], d), F32)
        for e in range(EXPERTS_PER_GROUP):
            gu = jnp.dot(x, wgu_ref[0, e], preferred_element_type=F32)
            g = gu[:, 0:D_EXPERT]
            act = g * _sigmoid(g) * gu[:, D_EXPERT:]
            y = jnp.dot(act.astype(BF16), wdn_ref[0, e], preferred_element_type=F32)
            acc = acc + gates[:, e:e + 1] * y
        ys_ref[...] = acc

    @pl.when(i >= nv_ref[0])
    def _():
        ys_ref[...] = jnp.zeros_like(ys_ref)


def _experts(block_group, n_valid, xs, w_gu, w_dn):
    n_rows = xs.shape[0]
    rb, d = ROW_BLOCK, D_MODEL
    wgu = w_gu.reshape(N_GROUPS, EXPERTS_PER_GROUP, d, 2 * D_EXPERT)
    wdn = w_dn.reshape(N_GROUPS, EXPERTS_PER_GROUP, D_EXPERT, d)
    return pl.pallas_call(
        _experts_kernel,
        grid_spec=pltpu.PrefetchScalarGridSpec(
            num_scalar_prefetch=2, grid=(n_rows // rb,),
            in_specs=[
                pl.BlockSpec((rb, XG_W), lambda i, bg, nv: (i, 0)),
                pl.BlockSpec((1, EXPERTS_PER_GROUP, d, 2 * D_EXPERT), lambda i, bg, nv: (bg[i], 0, 0, 0)),
                pl.BlockSpec((1, EXPERTS_PER_GROUP, D_EXPERT, d), lambda i, bg, nv: (bg[i], 0, 0, 0)),
            ],
            out_specs=pl.BlockSpec((rb, d), lambda i, bg, nv: (i, 0))),
        out_shape=jax.ShapeDtypeStruct((n_rows, d), F32),
        compiler_params=_cparams(("arbitrary",)),
        name="moe_experts",
    )(block_group, n_valid, xs, wgu, wdn)


def _moe(h2, w_rg, b_rg, w_re, b_re, w_gu_bf16, w_dn_bf16):
    t = h2.shape[0]
    rb = ROW_BLOCK
    xg, meta, counts = _router(h2, w_rg, b_rg, w_re, b_re)
    group, rank = meta[:, 0], meta[:, 1]
    cnt = counts[0, 0:N_GROUPS]
    padded = (cnt + rb - 1) // rb * rb
    seg_end = jnp.cumsum(padded)
    dest = (seg_end - padded)[group] + rank
    n_blocks = t // rb + N_GROUPS
    block_start = jnp.arange(n_blocks, dtype=I32) * rb
    block_group = jnp.minimum(jnp.sum((block_start[:, None] >= seg_end[None, :]).astype(I32), axis=1), N_GROUPS - 1)
    n_valid = (seg_end[-1:] // rb).astype(I32)
    xs = _scatter_rows(dest, xg, n_blocks * rb)
    ys = _experts(block_group, n_valid, xs, w_gu_bf16, w_dn_bf16)
    return _gather_rows(dest, ys, t)


def _final_kernel(x_ref, y_ref, m_ref, g_ref, o_ref, *, nblk):
    b = pl.program_id(0) // nblk
    d = D_MODEL
    m = _mod_row(m_ref, b)
    x = x_ref[...] + m[:, 5 * d:6 * d] * y_ref[...]
    o_ref[...] = _rms(x, d) * g_ref[...]


def _final(x1, y, m, g, nblk):
    t, d = x1.shape
    rb = ROW_BLOCK
    row = pl.BlockSpec((rb, d), lambda i: (i, 0))
    full = lambda a: pl.BlockSpec(a.shape, lambda i: (0,) * a.ndim)
    return pl.pallas_call(
        functools.partial(_final_kernel, nblk=nblk),
        grid=(t // rb,),
        in_specs=[row, row, full(m), full(g)],
        out_specs=row,
        out_shape=jax.ShapeDtypeStruct((t, d), F32),
        compiler_params=_cparams(("arbitrary",)),
        name="final_norm",
    )(x1, y, m, g)


def _inproj_weight(w_in):
    d = w_in.shape[0]
    z = lambda n: jnp.zeros((d, n), w_in.dtype)
    perm = _rope_swap_perm()
    kr = w_in[:, 3168:3200]
    cols = [w_in[:, 0:2048], w_in[:, 2048:2848], z(W_C - 800),
            w_in[:, 2848:3040], z(256 - D_Q_RANK), w_in[:, 3040:3168], kr, kr[:, perm], z(LANE - 2 * D_ROPE)]
    return jnp.concatenate(cols, axis=1).astype(BF16)


def _hgrn_lower_bounds(logits):
    cum = jnp.cumsum(jax.nn.softmax(logits.astype(F32), axis=0), axis=0)
    return cum - cum[0]


def kernel(x, c, ctx, c_ctx, w_mod, b_mod, norm1_g, norm2_g, w_in, w_out, hgrn_lb_logits, hgrn_norm_g, na_rpb, gla_wg_f, gla_bg_f, gla_wg_b, gla_bg_b, gla_norm_g, mla_q_norm_g, mla_w_uq, mla_kv_norm_g, mla_w_ukv, moe_w_rg, moe_b_rg, moe_w_re, moe_b_re, moe_w_gu, moe_w_dn, final_norm_g):
    batch, n, d = x.shape
    l_ctx = ctx.shape[1]
    assert d == D_MODEL and l_ctx == ROW_BLOCK and n % ROW_BLOCK == 0 and batch <= 4
    s_len = l_ctx + n
    nblk = s_len // ROW_BLOCK
    depth = w_mod.shape[0]

    c8 = jnp.zeros((8, d), F32).at[0:batch].set(c).at[4].set(c_ctx)
    mods = _mod_vectors(c8, w_mod, b_mod)
    lower_bounds = _hgrn_lower_bounds(hgrn_lb_logits)
    tables = _mla_tables(n, l_ctx)

    xa = jnp.concatenate([ctx, x], axis=1).reshape(batch * s_len, d)
    y_prev = None
    for layer in range(depth):
        keep_ctx = layer < depth - 1
        m = mods[layer]
        xa, (pa, pb, pc, pd) = _inproj(xa, y_prev, mods[layer - 1] if layer else None, m,
                                       norm1_g[layer].reshape(1, d), _inproj_weight(w_in[layer]), nblk)
        oaf, oab = _hgrn_scan(pa, lower_bounds[layer], batch, nblk)
        ob = _neighbourhood_attention(pb, na_rpb[layer], batch, s_len, l_ctx, keep_ctx)
        ogf, ogb = _gla_scan(pc, gla_wg_f[layer], gla_bg_f[layer], gla_wg_b[layer], gla_bg_b[layer], batch, nblk)
        mla_w = _mla_weights(mla_q_norm_g[layer], mla_w_uq[layer], mla_kv_norm_g[layer], mla_w_ukv[layer])
        q, k, v = _mla_prep(pd, mla_w, tables, nblk)
        od = _mla_attention(q, k, v, batch, s_len, l_ctx, keep_ctx)
        xa, h2 = _outproj(oaf, oab, pa, ob, ogf, ogb, pc, od, xa, m, hgrn_norm_g[layer].reshape(1, -1),
                          gla_norm_g[layer].reshape(1, -1), norm2_g[layer].reshape(1, d), w_out[layer],
                          batch, nblk, keep_ctx)
        y_prev = _moe(h2, moe_w_rg[layer], moe_b_rg[layer], moe_w_re[layer], moe_b_re[layer],
                      moe_w_gu[layer].astype(BF16), moe_w_dn[layer].astype(BF16))
    out = _final(xa, y_prev, mods[depth - 1], final_norm_g.reshape(1, d), n // ROW_BLOCK)
    return out.reshape(batch, n, d)
```

```python
import functools

import numpy as np
import jax
import jax.numpy as jnp
from jax import lax
from jax.experimental import pallas as pl
from jax.experimental.pallas import tpu as pltpu

F32 = jnp.float32
BF16 = jnp.bfloat16
I32 = jnp.int32

D_MODEL = 1024
DEPTH = 2
GRID_W = 64
EPS = 1e-6
D_GROUP = 256
N_HEADS = 4
A_DH = 64
B_DH = 64
WIN_H = 8
WIN_W = 16
C_DK = 32
C_DV = 64
C_GATE_RANK = 16
C_GATE_NORM = 16.0
D_NOPE = 64
D_V = 64
D_ROPE = 32
ROPE_FREQS = 8
ROPE_BASE = 10000.0
D_Q_RANK = 192
D_KV_RANK = 128
MLA_SCALE = (D_NOPE + D_ROPE) ** -0.5
N_GROUPS = 4
EXPERTS_PER_GROUP = 8
D_EXPERT = 256

ROW_BLOCK = 256
SUB = 16
LANE = 128
HEAD_TILE = 128
W_A, W_B, W_C, W_D = 1280, 768, 896, 512
TOK_TILES = D_MODEL // LANE
VMEM_LIMIT = 52 * 1024 * 1024
NEG_INF = float("-inf")


def _bdot(a, b):
    return jnp.dot(a.astype(BF16), b.astype(BF16), preferred_element_type=F32)


def _bdot_nt(a, b):
    return lax.dot_general(a.astype(BF16), b.astype(BF16), (((1,), (1,)), ((), ())), preferred_element_type=F32)


def _bdot_tn(a, b):
    return lax.dot_general(a.astype(BF16), b.astype(BF16), (((0,), (0,)), ((), ())), preferred_element_type=F32)


def _split3(a):
    hi = a.astype(BF16)
    r1 = a - hi.astype(F32)
    mid = r1.astype(BF16)
    lo = (r1 - mid.astype(F32)).astype(BF16)
    return hi, mid, lo


def _dot_f32(a, b):
    ah, am, al = _split3(a)
    bh, bm, bl = _split3(b)
    d = lambda u, v: jnp.dot(u, v, preferred_element_type=F32)
    return d(ah, bh) + (d(ah, bm) + d(am, bh)) + (d(am, bm) + d(ah, bl) + d(al, bh))


def _dot_sel_l(sel, a):
    ah, am, al = _split3(a)
    d = lambda v: jnp.dot(sel, v, preferred_element_type=F32)
    return d(ah) + d(am) + d(al)


def _dot_sel_r(a, sel):
    ah, am, al = _split3(a)
    d = lambda u: jnp.dot(u, sel, preferred_element_type=F32)
    return d(ah) + d(am) + d(al)


def _sigmoid(x):
    return 1.0 / (1.0 + jnp.exp(-x))


def _log_sigmoid(x):
    return jnp.minimum(x, 0.0) - jnp.log1p(jnp.exp(-jnp.abs(x)))


def _logaddexp(a, b):
    amax = jnp.maximum(a, b)
    delta = a - b
    return jnp.where(jnp.isnan(delta), a + b, amax + jnp.log1p(jnp.exp(-jnp.abs(delta))))


def _rms(x, width):
    return x * lax.rsqrt(jnp.sum(x * x, axis=-1, keepdims=True) / width + EPS)


def _cparams(sem, vmem=VMEM_LIMIT):
    return pltpu.CompilerParams(dimension_semantics=sem, vmem_limit_bytes=vmem)


def _mod_kernel(c_ref, w_ref, b_ref, o_ref):
    c = c_ref[...]
    act = c * _sigmoid(c)
    o_ref[0] = _dot_f32(act, w_ref[0]) + b_ref[0]


def _mod_vectors(c8, w_mod, b_mod):
    depth, d, six_d = w_mod.shape
    nj = six_d // d
    return pl.pallas_call(
        _mod_kernel,
        grid=(depth, nj),
        in_specs=[
            pl.BlockSpec((8, d), lambda l, j: (0, 0)),
            pl.BlockSpec((1, d, d), lambda l, j: (l, 0, j)),
            pl.BlockSpec((1, 1, d), lambda l, j: (l, 0, j)),
        ],
        out_specs=pl.BlockSpec((1, 8, d), lambda l, j: (l, 0, j)),
        out_shape=jax.ShapeDtypeStruct((depth, 8, six_d), F32),
        compiler_params=_cparams(("arbitrary", "arbitrary")),
        name="mod_vectors",
    )(c8, w_mod, b_mod.reshape(depth, 1, six_d))


def _mod_row(m_ref, row):
    return m_ref[pl.ds(row, 1), :]


def _inproj_kernel(*refs, fuse_res, nblk):
    if fuse_res:
        x_ref, y_ref, mprev_ref, m_ref, g_ref, w_ref, xo_ref, pa_ref, pb_ref, pc_ref, pd_ref = refs
    else:
        x_ref, m_ref, g_ref, w_ref, pa_ref, pb_ref, pc_ref, pd_ref = refs
    i = pl.program_id(0)
    b = i // nblk
    row = jnp.where(i % nblk == 0, 4, b)
    d = D_MODEL
    x = x_ref[...]
    if fuse_res:
        mp = _mod_row(mprev_ref, row)
        x = x + mp[:, 5 * d:6 * d] * _load_token_rows(y_ref)
        xo_ref[...] = x
    m = _mod_row(m_ref, row)
    h = _rms(x, d) * g_ref[...] * (1.0 + m[:, d:2 * d]) + m[:, 0:d]
    p = _bdot(h, w_ref[...])
    pa_ref[...] = p[:, 0:W_A]
    pb_ref[...] = p[:, W_A:W_A + W_B]
    pc_ref[...] = p[:, W_A + W_B:W_A + W_B + W_C]
    pd_ref[...] = p[:, W_A + W_B + W_C:]


def _inproj(x, y, m_prev, m, g, w, nblk):
    t, d = x.shape
    fuse = y is not None
    rb = ROW_BLOCK
    row_spec = lambda w_: pl.BlockSpec((rb, w_), lambda i: (i, 0))
    full = lambda a: pl.BlockSpec(a.shape, lambda i: (0,) * a.ndim)
    ins, specs = [x], [row_spec(d)]
    if fuse:
        ins += [y, m_prev]
        specs += [pl.BlockSpec((rb * TOK_TILES, LANE), lambda i: (i, 0)), full(m_prev)]
    ins += [m, g, w]
    specs += [full(m), full(g), full(w)]
    outs, ospecs = [], []
    if fuse:
        outs.append(jax.ShapeDtypeStruct((t, d), F32))
        ospecs.append(row_spec(d))
    for w_ in (W_A, W_B, W_C, W_D):
        outs.append(jax.ShapeDtypeStruct((t, w_), F32))
        ospecs.append(row_spec(w_))
    res = pl.pallas_call(
        functools.partial(_inproj_kernel, fuse_res=fuse, nblk=nblk),
        grid=(t // rb,),
        in_specs=specs,
        out_specs=ospecs,
        out_shape=outs,
        compiler_params=_cparams(("arbitrary",)),
        name="in_projection",
    )(*ins)
    if fuse:
        return res[0], res[1:]
    return x, res


def _sub_chunk(refs, i, tri, emat, emask_t, reverse):
    q_ref, k_ref, la_ref, v_ref, o_ref, st_ref, r_ref = refs
    hv = v_ref.shape[1]
    row_id = lax.broadcasted_iota(I32, (SUB, 1), 0)
    off = pl.multiple_of(i * SUB, SUB)
    qs = q_ref[pl.ds(off, SUB), :]
    ks = k_ref[pl.ds(off, SUB), :]
    las = la_ref[pl.ds(off, SUB), :]
    vs = v_ref[pl.ds(off, SUB), :]
    cum = _dot_sel_l(tri, las)
    last = cum[0:1, :] if reverse else cum[SUB - 1:SUB, :]
    for j in range(SUB):
        valid = (row_id <= j) if reverse else (row_id >= j)
        dlt = jnp.where(valid, cum - cum[j:j + 1, :], NEG_INF)
        r_ref[j * SUB:(j + 1) * SUB, :] = (qs * ks[j:j + 1, :] * jnp.exp(dlt)).astype(BF16)
    att = jnp.dot(r_ref[...], emat, preferred_element_type=F32)
    o = jnp.zeros((SUB, hv), F32)
    for j in range(SUB):
        o = o + att[j * SUB:(j + 1) * SUB, :] * vs[j:j + 1, :]
    st = st_ref[...]
    o = o + _bdot_nt(qs * jnp.exp(cum), st)
    o_ref[pl.ds(off, SUB), :] = o
    kd = ks * jnp.exp(last - cum)
    st_ref[...] = st * jnp.exp(last) + _bdot_tn(vs, kd) * emask_t


def _scan_pair(fwd_refs, bwd_refs, tri_f, tri_b, emat, emask_t):
    n_sub = fwd_refs[0].shape[0] // SUB

    def body(step, carry):
        _sub_chunk(fwd_refs, step, tri_f, emat, emask_t, False)
        _sub_chunk(bwd_refs, n_sub - 1 - step, tri_b, emat, emask_t, True)
        return carry

    lax.fori_loop(0, n_sub, body, 0, unroll=2)


def _hgrn_prep(q_ref, v_ref, f_ref, lb, qo, ko, lo, vo):
    qr = q_ref[...]
    qo[...] = qr * _sigmoid(qr) * (A_DH ** -0.5)
    vo[...] = v_ref[...]
    z = f_ref[...]
    lo[...] = _logaddexp(jnp.log(lb), jnp.log1p(-lb) + _log_sigmoid(z))
    ko[...] = (1.0 - lb) * _sigmoid(-z)


def _hgrn_kernel(qf_ref, vf_ref, ff_ref, qb_ref, vb_ref, fb_ref, lb_ref, trif_ref, trib_ref, emat_ref, emask_ref,
                 of_ref, ob_ref, stf, stb, rf, rb_, q1, k1, l1, v1, q2, k2, l2, v2):
    @pl.when(pl.program_id(1) == 0)
    def _():
        stf[...] = jnp.zeros_like(stf)
        stb[...] = jnp.zeros_like(stb)

    _hgrn_prep(qf_ref, vf_ref, ff_ref, lb_ref[0:1, :], q1, k1, l1, v1)
    _hgrn_prep(qb_ref, vb_ref, fb_ref, lb_ref[1:2, :], q2, k2, l2, v2)
    emat = emat_ref[...]
    emask = emask_ref[...]
    _scan_pair((q1, k1, l1, v1, of_ref, stf, rf), (q2, k2, l2, v2, ob_ref, stb, rb_),
               trif_ref[...], trib_ref[...], emat, emask)


def _tri_consts():
    i = np.arange(SUB)
    trif = (i[None, :] <= i[:, None]).astype(np.float32)
    trib = (i[None, :] >= i[:, None]).astype(np.float32)
    return jnp.asarray(trif, BF16), jnp.asarray(trib, BF16)


def _head_match(hk, hv, dk, dv):
    m = (np.arange(hk)[:, None] // dk == np.arange(hv)[None, :] // dv).astype(np.float32)
    return m


def _bwd_block(t, nblk):
    return jnp.where(t == 0, 0, nblk - t)


def _hgrn_scan(pa, lb, batch, nblk):
    t_all = pa.shape[0]
    rb, w = ROW_BLOCK, D_GROUP
    trif, trib = _tri_consts()
    em = _head_match(w, w, A_DH, A_DH)
    emat, emask = jnp.asarray(em, BF16), jnp.asarray(em.T, F32)
    fwd = lambda col: pl.BlockSpec((rb, w), lambda b, t: (b * nblk + t, col))
    bwd = lambda col: pl.BlockSpec((rb, w), lambda b, t: (b * nblk + _bwd_block(t, nblk), col))
    full = lambda a: pl.BlockSpec(a.shape, lambda b, t: (0,) * a.ndim)
    vm = lambda shape, dt=F32: pltpu.VMEM(shape, dt)
    return pl.pallas_call(
        _hgrn_kernel,
        grid=(batch, nblk),
        in_specs=[fwd(0), fwd(1), fwd(2), bwd(0), bwd(1), bwd(3), full(lb), full(trif), full(trib), full(emat), full(emask)],
        out_specs=[fwd(0), bwd(0)],
        out_shape=[jax.ShapeDtypeStruct((t_all, w), F32)] * 2,
        scratch_shapes=[vm((w, w)), vm((w, w)), vm((SUB * SUB, w), BF16), vm((SUB * SUB, w), BF16)]
        + [vm((rb, w))] * 8,
        compiler_params=_cparams(("arbitrary", "arbitrary")),
        name="hgrn_scan",
    )(pa, pa, pa, pa, pa, pa, lb, trif, trib, emat, emask)


def _gla_prep(q_ref, k_ref, v_ref, z_ref, wg_ref, bg_ref, qo, ko, lo, vo):
    qo[...] = q_ref[...] * (C_DK ** -0.5)
    ko[...] = k_ref[...]
    vo[...] = v_ref[...]
    zl = _dot_f32(z_ref[...], wg_ref[...]) + bg_ref[...]
    lo[...] = _log_sigmoid(zl) / C_GATE_NORM


def _gla_kernel(qf_ref, kf_ref, vf_ref, zf_ref, qb_ref, kb_ref, vb_ref, zb_ref, wgf_ref, bgf_ref, wgb_ref, bgb_ref,
                trif_ref, trib_ref, emat_ref, emask_ref, of_ref, ob_ref, stf, stb, rf, rb_, q1, k1, l1, v1, q2, k2, l2, v2):
    @pl.when(pl.program_id(1) == 0)
    def _():
        stf[...] = jnp.zeros_like(stf)
        stb[...] = jnp.zeros_like(stb)

    _gla_prep(qf_ref, kf_ref, vf_ref, zf_ref, wgf_ref, bgf_ref, q1, k1, l1, v1)
    _gla_prep(qb_ref, kb_ref, vb_ref, zb_ref, wgb_ref, bgb_ref, q2, k2, l2, v2)
    emat = emat_ref[...]
    emask = emask_ref[...]
    _scan_pair((q1, k1, l1, v1, of_ref, stf, rf), (q2, k2, l2, v2, ob_ref, stb, rb_),
               trif_ref[...], trib_ref[...], emat, emask)


def _gla_scan(pc, wg_f, bg_f, wg_b, bg_b, batch, nblk):
    t_all = pc.shape[0]
    rb = ROW_BLOCK
    hk, hv = N_HEADS * C_DK, N_HEADS * C_DV
    trif, trib = _tri_consts()
    em = _head_match(hk, hv, C_DK, C_DV)
    emat, emask = jnp.asarray(em, BF16), jnp.asarray(em.T, F32)
    wgf = jnp.zeros((LANE, hk), F32).at[0:C_GATE_RANK].set(wg_f)
    wgb = jnp.zeros((LANE, hk), F32).at[C_GATE_RANK:2 * C_GATE_RANK].set(wg_b)
    bgf, bgb = bg_f.reshape(1, hk), bg_b.reshape(1, hk)
    fwd = lambda w, col: pl.BlockSpec((rb, w), lambda b, t: (b * nblk + t, col))
    bwd = lambda w, col: pl.BlockSpec((rb, w), lambda b, t: (b * nblk + _bwd_block(t, nblk), col))
    full = lambda a: pl.BlockSpec(a.shape, lambda b, t: (0,) * a.ndim)
    vm = lambda shape, dt=F32: pltpu.VMEM(shape, dt)
    return pl.pallas_call(
        _gla_kernel,
        grid=(batch, nblk),
        in_specs=[fwd(hk, 0), fwd(hk, 1), fwd(hv, 1), fwd(LANE, 6), bwd(hk, 0), bwd(hk, 1), bwd(hv, 1), bwd(LANE, 6),
                  full(wgf), full(bgf), full(wgb), full(bgb), full(trif), full(trib), full(emat), full(emask)],
        out_specs=[fwd(hv, 0), bwd(hv, 0)],
        out_shape=[jax.ShapeDtypeStruct((t_all, hv), F32)] * 2,
        scratch_shapes=[vm((hv, hk)), vm((hv, hk)), vm((SUB * SUB, hk), BF16), vm((SUB * SUB, hk), BF16)]
        + [vm((rb, hk)), vm((rb, hk)), vm((rb, hk)), vm((rb, hv))] * 2,
        compiler_params=_cparams(("arbitrary", "arbitrary")),
        name="gla_scan",
    )(pc, pc, pc, pc, pc, pc, pc, pc, wgf, bgf, wgb, bgb, trif, trib, emat, emask)


def _na_kernel(q_ref, k_ref, v_ref, bias_ref, hm_ref, o_ref, *, j0, rows, ctx):
    j = pl.program_id(1) + j0
    n_ctx_blk = ctx // GRID_W
    q = q_ref[...] * (B_DH ** -0.5)
    hm = hm_ref[...]
    kc = k_ref[0:ctx, :].astype(BF16)
    vc = v_ref[0:ctx, :].astype(BF16)

    @pl.when(j < n_ctx_blk)
    def _():
        acc = jnp.zeros(q.shape, F32)
        for h in range(N_HEADS):
            mh = hm[h:h + 1, :]
            s = _bdot_nt(q * mh, kc)
            p = jnp.exp(s - jnp.max(s, axis=-1, keepdims=True))
            inv = 1.0 / jnp.sum(p, axis=-1, keepdims=True)
            acc = acc + _bdot(p, vc) * (mh * inv)
        o_ref[...] = acc.astype(o_ref.dtype)

    @pl.when(j >= n_ctx_blk)
    def _():
        r = j - n_ctx_blk
        start = jnp.clip(r - WIN_H // 2, 0, rows - WIN_H)
        off = pl.multiple_of(ctx + start * GRID_W, GRID_W)
        kw = k_ref[pl.ds(off, WIN_H * GRID_W), :].astype(BF16)
        vw = v_ref[pl.ds(off, WIN_H * GRID_W), :].astype(BF16)
        acc = jnp.zeros(q.shape, F32)
        for h in range(N_HEADS):
            mh = hm[h:h + 1, :]
            qh = q * mh
            sw = _bdot_nt(qh, kw) + bias_ref[0, h]
            sc = _bdot_nt(qh, kc)
            m = jnp.maximum(jnp.max(sw, axis=-1, keepdims=True), jnp.max(sc, axis=-1, keepdims=True))
            pw = jnp.exp(sw - m)
            pc_ = jnp.exp(sc - m)
            inv = 1.0 / (jnp.sum(pw, axis=-1, keepdims=True) + jnp.sum(pc_, axis=-1, keepdims=True))
            acc = acc + (_bdot(pw, vw) + _bdot(pc_, vc)) * (mh * inv)
        o_ref[...] = acc.astype(o_ref.dtype)


def _na_bias_table(rpb, rows):
    kh = WIN_H
    cidx = np.arange(GRID_W)
    c_start = np.clip(cidx - WIN_W // 2, 0, GRID_W - WIN_W)
    col_in = (cidx[None] >= c_start[:, None]) & (cidx[None] < c_start[:, None] + WIN_W)
    dc = np.clip(cidx[None] - cidx[:, None], -(WIN_W - 1), WIN_W - 1) + (WIN_W - 1)
    sel = (dc[None] == np.arange(2 * WIN_W - 1)[:, None, None]).astype(np.float32)
    by_col = jnp.einsum("hrc,cqw->hrqw", rpb.astype(F32), jnp.asarray(sel), precision=lax.Precision.HIGHEST)
    by_col = jnp.where(jnp.asarray(col_in)[None, None], by_col, NEG_INF)
    slabs = [by_col[:, WIN_H - 1 - roff:WIN_H - 1 - roff + kh] for roff in range(kh)]
    bias = jnp.stack(slabs, axis=0)
    return jnp.transpose(bias, (0, 1, 3, 2, 4)).reshape(kh, N_HEADS, GRID_W, kh * GRID_W)


def _head_masks(width, dh):
    return jnp.asarray((np.arange(width)[None, :] // dh == np.arange(N_HEADS)[:, None]).astype(np.float32))


def _neighbourhood_attention(pb, rpb, batch, s_len, ctx, keep_ctx):
    t_all = pb.shape[0]
    rows = (s_len - ctx) // GRID_W
    assert rows >= WIN_H
    n_ctx_blk = ctx // GRID_W
    j0 = 0 if keep_ctx else n_ctx_blk
    per_b = s_len // GRID_W
    bias = _na_bias_table(rpb, rows)
    hm = _head_masks(D_GROUP, B_DH)

    def bias_idx(b, jj):
        r = jnp.maximum(jj + j0 - n_ctx_blk, 0)
        start = jnp.clip(r - WIN_H // 2, 0, rows - WIN_H)
        return (r - start, 0, 0, 0)

    return pl.pallas_call(
        functools.partial(_na_kernel, j0=j0, rows=rows, ctx=ctx),
        grid=(batch, per_b - j0),
        in_specs=[
            pl.BlockSpec((GRID_W, D_GROUP), lambda b, jj: (b * per_b + jj + j0, 0)),
            pl.BlockSpec((s_len, D_GROUP), lambda b, jj: (b, 1)),
            pl.BlockSpec((s_len, D_GROUP), lambda b, jj: (b, 2)),
            pl.BlockSpec((1, N_HEADS, GRID_W, WIN_H * GRID_W), bias_idx),
            pl.BlockSpec(hm.shape, lambda b, jj: (0, 0)),
        ],
        out_specs=pl.BlockSpec((GRID_W, D_GROUP), lambda b, jj: (b * (per_b - j0) + jj, 0)),
        out_shape=jax.ShapeDtypeStruct((batch * (per_b - j0) * GRID_W, D_GROUP), BF16),
        compiler_params=_cparams(("arbitrary", "arbitrary")),
        name="neighbourhood_attention",
    )(pb, pb, pb, bias, hm)


def _mla_prep_kernel(pd_ref, gq_ref, gkv_ref, wq1_ref, wq2_ref, wkv_ref, cq_ref, sq_ref, tk_ref, place_ref,
                     q_ref, k_ref, v_ref):
    pd = pd_ref[...]
    cq = pd[:, 0:256]
    ckv = pd[:, 256:384]
    kr = pd[:, 384:512]
    qn = _rms(cq, D_Q_RANK) * gq_ref[...]
    q = _bdot(qn, wq1_ref[...]) * cq_ref[...] + _bdot(qn, wq2_ref[...]) * sq_ref[...]
    q_ref[...] = q.astype(BF16)
    kvn = _rms(ckv, D_KV_RANK) * gkv_ref[...]
    kv = _bdot(kvn, wkv_ref[...])
    hw = N_HEADS * HEAD_TILE
    k = kv[:, 0:hw] + _dot_sel_r(kr * tk_ref[...], place_ref[...])
    k_ref[...] = k.astype(BF16)
    v_ref[...] = kv[:, hw:].astype(BF16)


def _rope_swap_perm():
    f = ROPE_FREQS
    return np.concatenate([np.arange(f, 2 * f), np.arange(0, f), np.arange(3 * f, 4 * f), np.arange(2 * f, 3 * f)])


def _mla_tables(n, ctx):
    t = np.arange(n)
    inv_freq = ROPE_BASE ** (-np.arange(ROPE_FREQS, dtype=np.float32) / ROPE_FREQS)
    ang_r = (t // GRID_W).astype(np.float32)[:, None] * inv_freq
    ang_c = (t % GRID_W).astype(np.float32)[:, None] * inv_freq
    cos32 = np.concatenate([np.cos(ang_r), np.cos(ang_r), np.cos(ang_c), np.cos(ang_c)], axis=1)
    sin32 = np.concatenate([-np.sin(ang_r), np.sin(ang_r), -np.sin(ang_c), np.sin(ang_c)], axis=1)
    cos32 = np.concatenate([np.ones((ctx, D_ROPE), np.float32), cos32.astype(np.float32)], axis=0)
    sin32 = np.concatenate([np.zeros((ctx, D_ROPE), np.float32), sin32.astype(np.float32)], axis=0)
    s_len = n + ctx
    cq = np.zeros((s_len, N_HEADS, HEAD_TILE), np.float32)
    sq = np.zeros((s_len, N_HEADS, HEAD_TILE), np.float32)
    cq[:, :, 0:D_NOPE] = MLA_SCALE
    cq[:, :, D_NOPE:D_NOPE + D_ROPE] = cos32[:, None, :] * MLA_SCALE
    sq[:, :, D_NOPE:D_NOPE + D_ROPE] = sin32[:, None, :] * MLA_SCALE
    tk = np.zeros((s_len, LANE), np.float32)
    tk[:, 0:D_ROPE] = cos32
    tk[:, D_ROPE:2 * D_ROPE] = sin32
    place = np.zeros((LANE, N_HEADS * HEAD_TILE), np.float32)
    for h in range(N_HEADS):
        for l in range(D_ROPE):
            place[l, h * HEAD_TILE + D_NOPE + l] = 1.0
            place[D_ROPE + l, h * HEAD_TILE + D_NOPE + l] = 1.0
    return (jnp.asarray(cq.reshape(s_len, -1)), jnp.asarray(sq.reshape(s_len, -1)), jnp.asarray(tk),
            jnp.asarray(place, BF16))


def _mla_weights(q_norm_g, w_uq, kv_norm_g, w_ukv):
    perm = _rope_swap_perm()
    wq = w_uq.reshape(D_Q_RANK, N_HEADS, D_NOPE + D_ROPE)
    wq1 = jnp.zeros((256, N_HEADS, HEAD_TILE), F32).at[0:D_Q_RANK, :, 0:D_NOPE + D_ROPE].set(wq)
    wq2 = jnp.zeros((256, N_HEADS, HEAD_TILE), F32).at[0:D_Q_RANK, :, D_NOPE:D_NOPE + D_ROPE].set(
        wq[:, :, D_NOPE:][:, :, perm])
    wkv = w_ukv.reshape(D_KV_RANK, N_HEADS, D_NOPE + D_V)
    wk = jnp.zeros((D_KV_RANK, N_HEADS, HEAD_TILE), F32).at[:, :, 0:D_NOPE].set(wkv[:, :, 0:D_NOPE])
    wv = jnp.zeros((D_KV_RANK, N_HEADS, HEAD_TILE), F32).at[:, :, 0:D_V].set(wkv[:, :, D_NOPE:])
    hw = N_HEADS * HEAD_TILE
    wkv_p = jnp.concatenate([wk.reshape(D_KV_RANK, hw), wv.reshape(D_KV_RANK, hw)], axis=1)
    gq = jnp.zeros((1, 256), F32).at[0, 0:D_Q_RANK].set(q_norm_g)
    return (gq, kv_norm_g.reshape(1, D_KV_RANK), wq1.reshape(256, hw).astype(BF16), wq2.reshape(256, hw).astype(BF16),
            wkv_p.astype(BF16))


def _mla_prep(pd, weights, tables, nblk):
    t_all = pd.shape[0]
    rb = ROW_BLOCK
    gq, gkv, wq1, wq2, wkv = weights
    cq, sq, tk, place = tables
    hw = N_HEADS * HEAD_TILE
    row = lambda w: pl.BlockSpec((rb, w), lambda i: (i, 0))
    pos = lambda w: pl.BlockSpec((rb, w), lambda i: (i % nblk, 0))
    full = lambda a: pl.BlockSpec(a.shape, lambda i: (0,) * a.ndim)
    return pl.pallas_call(
        _mla_prep_kernel,
        grid=(t_all // rb,),
        in_specs=[row(W_D), full(gq), full(gkv), full(wq1), full(wq2), full(wkv), pos(hw), pos(hw), pos(LANE), full(place)],
        out_specs=[row(hw)] * 3,
        out_shape=[jax.ShapeDtypeStruct((t_all, hw), BF16)] * 3,
        compiler_params=_cparams(("arbitrary",)),
        name="mla_prep",
    )(pd, gq, gkv, wq1, wq2, wkv, cq, sq, tk, place)


def _mla_attn_kernel(q_ref, k_ref, v_ref, o_ref, *, j0, ctx):
    j = pl.program_id(2) + j0
    q = q_ref[...]

    def attend(k, v):
        s = lax.dot_general(q, k, (((1,), (1,)), ((), ())), preferred_element_type=F32)
        p = jnp.exp(s - jnp.max(s, axis=-1, keepdims=True))
        inv = 1.0 / jnp.sum(p, axis=-1, keepdims=True)
        return (jnp.dot(p.astype(BF16), v, preferred_element_type=F32) * inv).astype(o_ref.dtype)

    @pl.when(j == 0)
    def _():
        o_ref[...] = attend(k_ref[0:ctx, :], v_ref[0:ctx, :])

    @pl.when(j > 0)
    def _():
        o_ref[...] = attend(k_ref[...], v_ref[...])


def _mla_attention(q, k, v, batch, s_len, ctx, keep_ctx):
    t_all = q.shape[0]
    rb = ROW_BLOCK
    nblk = s_len // rb
    j0 = 0 if keep_ctx else 1
    return pl.pallas_call(
        functools.partial(_mla_attn_kernel, j0=j0, ctx=ctx),
        grid=(batch, N_HEADS, nblk - j0),
        in_specs=[
            pl.BlockSpec((rb, HEAD_TILE), lambda b, h, jj: (b * nblk + jj + j0, h)),
            pl.BlockSpec((s_len, HEAD_TILE), lambda b, h, jj: (b, h)),
            pl.BlockSpec((s_len, HEAD_TILE), lambda b, h, jj: (b, h)),
        ],
        out_specs=pl.BlockSpec((rb, HEAD_TILE), lambda b, h, jj: (b * (nblk - j0) + jj, h)),
        out_shape=jax.ShapeDtypeStruct((batch * (nblk - j0) * rb, N_HEADS * HEAD_TILE), BF16),
        compiler_params=_cparams(("arbitrary", "arbitrary", "arbitrary")),
        name="mla_attention",
    )(q, k, v)


def _outproj_kernel(oaf_ref, oab_ref, ga_ref, ob_ref, ogf_ref, ogb_ref, gc_ref, od_ref, x_ref, m_ref, gha_ref, ghc_ref,
                    g2_ref, wa_ref, wb_ref, wc_ref, wd_ref, em_ref, x1_ref, h2_ref, *, j0, nblk_out):
    i = pl.program_id(0)
    b = i // nblk_out
    row = jnp.where(i % nblk_out + j0 == 0, 4, b)
    d = D_MODEL
    em = em_ref[...]

    def readout(o, g_norm, gate, dh):
        ms = _dot_sel_r(o * o, em) / dh
        return o * lax.rsqrt(ms + EPS) * g_norm * (gate * _sigmoid(gate))

    a = readout(oaf_ref[...] + oab_ref[...], gha_ref[...], ga_ref[...], A_DH)
    c = readout(ogf_ref[...] + ogb_ref[...], ghc_ref[...], gc_ref[...], C_DV)
    mix = (_bdot(a, wa_ref[...]) + jnp.dot(ob_ref[...], wb_ref[...], preferred_element_type=F32)
           + _bdot(c, wc_ref[...]) + jnp.dot(od_ref[...], wd_ref[...], preferred_element_type=F32))
    m = _mod_row(m_ref, row)
    x1 = x_ref[...] + m[:, 2 * d:3 * d] * mix
    x1_ref[...] = x1
    _store_token_rows(h2_ref, _rms(x1, d) * g2_ref[...] * (1.0 + m[:, 4 * d:5 * d]) + m[:, 3 * d:4 * d])


def _outproj(oaf, oab, pa, ob, ogf, ogb, pc, od, x, m, gha, ghc, g2, w_out, batch, nblk, keep_ctx):
    rb, d = ROW_BLOCK, D_MODEL
    j0 = 0 if keep_ctx else 1
    nblk_out = nblk - j0
    t_out = batch * nblk_out * rb
    wa = w_out[0:256].astype(BF16)
    wb = w_out[256:512].astype(BF16)
    wc = w_out[512:768].astype(BF16)
    wd = jnp.zeros((N_HEADS, HEAD_TILE, d), F32).at[:, 0:D_V, :].set(w_out[768:1024].reshape(N_HEADS, D_V, d))
    wd = wd.reshape(N_HEADS * HEAD_TILE, d).astype(BF16)
    em = jnp.asarray(_head_match(D_GROUP, D_GROUP, 64, 64), BF16)
    src = lambda i: (i // nblk_out) * nblk + i % nblk_out + j0
    row_in = lambda w, col=0: pl.BlockSpec((rb, w), lambda i: (src(i), col))
    row_out = lambda w: pl.BlockSpec((rb, w), lambda i: (i, 0))
    full = lambda a_: pl.BlockSpec(a_.shape, lambda i: (0,) * a_.ndim)
    return pl.pallas_call(
        functools.partial(_outproj_kernel, j0=j0, nblk_out=nblk_out),
        grid=(t_out // rb,),
        in_specs=[row_in(256), row_in(256), row_in(256, 4), row_out(256), row_in(256), row_in(256), row_in(256, 2),
                  row_out(N_HEADS * HEAD_TILE), row_in(d), full(m), full(gha), full(ghc), full(g2),
                  full(wa), full(wb), full(wc), full(wd), full(em)],
        out_specs=[row_out(d), pl.BlockSpec((rb * TOK_TILES, LANE), lambda i: (i, 0))],
        out_shape=[jax.ShapeDtypeStruct((t_out, d), F32), jax.ShapeDtypeStruct((t_out * TOK_TILES, LANE), F32)],
        compiler_params=_cparams(("arbitrary",)),
        name="out_projection",
    )(oaf, oab, pa, ob, ogf, ogb, pc, od, x, m, gha, ghc, g2, wa, wb, wc, wd, em)


def _load_token_rows(ref):
    rows = ref.shape[0] // TOK_TILES
    return jnp.concatenate([ref[pl.ds(k, rows, stride=TOK_TILES), :] for k in range(TOK_TILES)], axis=1)


def _store_token_rows(ref, val):
    rows = val.shape[0]
    for k in range(TOK_TILES):
        ref[pl.ds(k, rows, stride=TOK_TILES), :] = val[:, k * LANE:(k + 1) * LANE]


def _router_logits(h, wr_ref, br_ref):
    lg = _dot_f32(h, wr_ref[...]) + br_ref[...]
    lane = lax.broadcasted_iota(I32, lg.shape, 1).astype(F32)
    return lg, lane


def _top_group(lg, lane):
    gl = jnp.where(lane < N_GROUPS, lg, NEG_INF)
    gmax = jnp.max(gl, axis=-1, keepdims=True)
    gsel = jnp.min(jnp.where(gl == gmax, lane, float(LANE)), axis=-1, keepdims=True)
    p_group = 1.0 / jnp.sum(jnp.exp(gl - gmax), axis=-1, keepdims=True)
    return gsel, p_group


def _expert_gates(lg, lane, lo, p_group):
    big = float(LANE)
    el = jnp.where((lane >= lo) & (lane < lo + EXPERTS_PER_GROUP), lg, NEG_INF)
    m1 = jnp.max(el, axis=-1, keepdims=True)
    i1 = jnp.min(jnp.where(el == m1, lane, big), axis=-1, keepdims=True)
    el2 = jnp.where(lane == i1, NEG_INF, el)
    m2 = jnp.max(el2, axis=-1, keepdims=True)
    i2 = jnp.min(jnp.where(el2 == m2, lane, big), axis=-1, keepdims=True)
    t = jnp.exp(m2 - m1)
    w1 = p_group / (1.0 + t)
    w2 = p_group * t / (1.0 + t)
    return jnp.where(lane == i1 - lo, w1, jnp.where(lane == i2 - lo, w2, 0.0))


def _router_kernel(h_ref, wr_ref, br_ref, tril_ref, meta_ref, cnt_ref, carry):
    @pl.when(pl.program_id(0) == 0)
    def _():
        carry[...] = jnp.zeros_like(carry)

    lg, lane = _router_logits(_load_token_rows(h_ref), wr_ref, br_ref)
    gsel, _ = _top_group(lg, lane)
    onehot = jnp.where(lane == gsel, 1.0, 0.0)
    incl = jnp.dot(tril_ref[...], onehot.astype(BF16), preferred_element_type=F32)
    rank = jnp.sum(onehot * (incl - 1.0 + carry[...]), axis=-1, keepdims=True)
    carry[...] = carry[...] + jnp.sum(onehot, axis=0, keepdims=True)
    meta_ref[...] = jnp.where(lane == 0, gsel, jnp.where(lane == 1, rank, 0.0)).astype(I32)
    cnt_ref[...] = carry[...].astype(I32)


def _router_weights(w_rg, b_rg, w_re, b_re):
    d = w_rg.shape[0]
    ne = N_GROUPS * EXPERTS_PER_GROUP
    wr = jnp.zeros((d, LANE), F32).at[:, 0:N_GROUPS].set(w_rg).at[:, N_GROUPS:N_GROUPS + ne].set(w_re)
    br = jnp.zeros((1, LANE), F32).at[0, 0:N_GROUPS].set(b_rg).at[0, N_GROUPS:N_GROUPS + ne].set(b_re)
    return wr, br


def _router(h2t, wr, br):
    t = h2t.shape[0] // TOK_TILES
    rb = ROW_BLOCK
    tril = jnp.asarray(np.tril(np.ones((rb, rb), np.float32)), BF16)
    full = lambda a: pl.BlockSpec(a.shape, lambda i: (0,) * a.ndim)
    return pl.pallas_call(
        _router_kernel,
        grid=(t // rb,),
        in_specs=[pl.BlockSpec((rb * TOK_TILES, LANE), lambda i: (i, 0)), full(wr), full(br), full(tril)],
        out_specs=[pl.BlockSpec((rb, LANE), lambda i: (i, 0)), pl.BlockSpec((1, LANE), lambda i: (0, 0))],
        out_shape=[jax.ShapeDtypeStruct((t, LANE), I32), jax.ShapeDtypeStruct((1, LANE), I32)],
        scratch_shapes=[pltpu.VMEM((1, LANE), F32)],
        compiler_params=_cparams(("arbitrary",)),
        name="moe_router",
    )(h2t, wr, br, tril)


def _perm_chunk(t):
    return next(c for c in (1024, 512, 256) if t % c == 0)


def _row_copy(src_ref, dst_ref, s, d, sem):
    s8 = pl.multiple_of(s * TOK_TILES, TOK_TILES)
    d8 = pl.multiple_of(d * TOK_TILES, TOK_TILES)
    return pltpu.make_async_copy(src_ref.at[pl.ds(s8, TOK_TILES), :], dst_ref.at[pl.ds(d8, TOK_TILES), :], sem)


def _scatter_rows_kernel(dest_ref, src_ref, init_ref, dst_ref, sem, *, chunk):
    del init_ref
    base = pl.program_id(0) * chunk

    def issue(t, c):
        _row_copy(src_ref, dst_ref, base + t, dest_ref[base + t], sem).start()
        return c

    lax.fori_loop(0, chunk, issue, 0)

    def drain(t, c):
        _row_copy(src_ref, dst_ref, base + t, dest_ref[base + t], sem).wait()
        return c

    lax.fori_loop(0, chunk, drain, 0)


def _gather_rows_kernel(dest_ref, src_ref, dst_ref, sem, *, chunk):
    base = pl.program_id(0) * chunk

    def issue(t, c):
        _row_copy(src_ref, dst_ref, dest_ref[base + t], base + t, sem).start()
        return c

    lax.fori_loop(0, chunk, issue, 0)

    def drain(t, c):
        _row_copy(src_ref, dst_ref, dest_ref[base + t], base + t, sem).wait()
        return c

    lax.fori_loop(0, chunk, drain, 0)


def _scatter_rows(dest, src, n_dst):
    t, w = src.shape[0] // TOK_TILES, src.shape[1]
    init = jnp.zeros((n_dst * TOK_TILES, w), src.dtype)
    any_spec = pl.BlockSpec(memory_space=pl.ANY)
    chunk = _perm_chunk(t)
    return pl.pallas_call(
        functools.partial(_scatter_rows_kernel, chunk=chunk),
        grid_spec=pltpu.PrefetchScalarGridSpec(
            num_scalar_prefetch=1, grid=(t // chunk,), in_specs=[any_spec, any_spec], out_specs=any_spec,
            scratch_shapes=[pltpu.SemaphoreType.DMA(())]),
        out_shape=jax.ShapeDtypeStruct((n_dst * TOK_TILES, w), src.dtype),
        input_output_aliases={2: 0},
        compiler_params=pltpu.CompilerParams(dimension_semantics=("arbitrary",), has_side_effects=True),
        name="moe_dispatch",
    )(dest, src, init)


def _gather_rows(dest, src, t):
    w = src.shape[1]
    any_spec = pl.BlockSpec(memory_space=pl.ANY)
    chunk = _perm_chunk(t)
    return pl.pallas_call(
        functools.partial(_gather_rows_kernel, chunk=chunk),
        grid_spec=pltpu.PrefetchScalarGridSpec(
            num_scalar_prefetch=1, grid=(t // chunk,), in_specs=[any_spec], out_specs=any_spec,
            scratch_shapes=[pltpu.SemaphoreType.DMA(())]),
        out_shape=jax.ShapeDtypeStruct((t * TOK_TILES, w), src.dtype),
        compiler_params=pltpu.CompilerParams(dimension_semantics=("arbitrary",), has_side_effects=True),
        name="moe_combine",
    )(dest, src)


def _experts_kernel(bg_ref, nv_ref, xs_ref, wr_ref, br_ref, wgu_ref, wdn_ref, ys_ref):
    i = pl.program_id(0)
    d = D_MODEL

    @pl.when(i < nv_ref[0])
    def _():
        xf = _load_token_rows(xs_ref)
        lg, lane = _router_logits(xf, wr_ref, br_ref)
        _, p_group = _top_group(lg, lane)
        lo = (N_GROUPS + bg_ref[i] * EXPERTS_PER_GROUP).astype(F32)
        gates = _expert_gates(lg, lane, lo, p_group)
        x = xf.astype(BF16)
        acc = jnp.zeros((x.shape[0], d), F32)
        for e in range(EXPERTS_PER_GROUP):
            gu = jnp.dot(x, wgu_ref[0, e], preferred_element_type=F32)
            g = gu[:, 0:D_EXPERT]
            act = g * _sigmoid(g) * gu[:, D_EXPERT:]
            y = jnp.dot(act.astype(BF16), wdn_ref[0, e], preferred_element_type=F32)
            acc = acc + gates[:, e:e + 1] * y
        _store_token_rows(ys_ref, acc)

    @pl.when(i >= nv_ref[0])
    def _():
        ys_ref[...] = jnp.zeros_like(ys_ref)


def _experts(block_group, n_valid, xs, wr, br, w_gu, w_dn):
    n_rows = xs.shape[0] // TOK_TILES
    rb, d = ROW_BLOCK, D_MODEL
    wgu = w_gu.reshape(N_GROUPS, EXPERTS_PER_GROUP, d, 2 * D_EXPERT)
    wdn = w_dn.reshape(N_GROUPS, EXPERTS_PER_GROUP, D_EXPERT, d)
    tok = pl.BlockSpec((rb * TOK_TILES, LANE), lambda i, bg, nv: (i, 0))
    return pl.pallas_call(
        _experts_kernel,
        grid_spec=pltpu.PrefetchScalarGridSpec(
            num_scalar_prefetch=2, grid=(n_rows // rb,),
            in_specs=[
                tok,
                pl.BlockSpec(wr.shape, lambda i, bg, nv: (0, 0)),
                pl.BlockSpec(br.shape, lambda i, bg, nv: (0, 0)),
                pl.BlockSpec((1, EXPERTS_PER_GROUP, d, 2 * D_EXPERT), lambda i, bg, nv: (bg[i], 0, 0, 0)),
                pl.BlockSpec((1, EXPERTS_PER_GROUP, D_EXPERT, d), lambda i, bg, nv: (bg[i], 0, 0, 0)),
            ],
            out_specs=tok),
        out_shape=jax.ShapeDtypeStruct((n_rows * TOK_TILES, LANE), F32),
        compiler_params=_cparams(("arbitrary",)),
        name="moe_experts",
    )(block_group, n_valid, xs, wr, br, wgu, wdn)


def _moe(h2t, w_rg, b_rg, w_re, b_re, w_gu_bf16, w_dn_bf16):
    t = h2t.shape[0] // TOK_TILES
    rb = ROW_BLOCK
    wr, br = _router_weights(w_rg, b_rg, w_re, b_re)
    meta, counts = _router(h2t, wr, br)
    group, rank = meta[:, 0], meta[:, 1]
    cnt = counts[0, 0:N_GROUPS]
    padded = (cnt + rb - 1) // rb * rb
    seg_end = jnp.cumsum(padded)
    dest = (seg_end - padded)[group] + rank
    n_blocks = t // rb + N_GROUPS
    block_start = jnp.arange(n_blocks, dtype=I32) * rb
    block_group = jnp.minimum(jnp.sum((block_start[:, None] >= seg_end[None, :]).astype(I32), axis=1), N_GROUPS - 1)
    n_valid = (seg_end[-1:] // rb).astype(I32)
    xs = _scatter_rows(dest, h2t, n_blocks * rb)
    ys = _experts(block_group, n_valid, xs, wr, br, w_gu_bf16, w_dn_bf16)
    return _gather_rows(dest, ys, t)


def _final_kernel(x_ref, y_ref, m_ref, g_ref, o_ref, *, nblk):
    b = pl.program_id(0) // nblk
    d = D_MODEL
    m = _mod_row(m_ref, b)
    x = x_ref[...] + m[:, 5 * d:6 * d] * _load_token_rows(y_ref)
    o_ref[...] = _rms(x, d) * g_ref[...]


def _final(x1, y, m, g, nblk):
    t, d = x1.shape
    rb = ROW_BLOCK
    row = pl.BlockSpec((rb, d), lambda i: (i, 0))
    tok = pl.BlockSpec((rb * TOK_TILES, LANE), lambda i: (i, 0))
    full = lambda a: pl.BlockSpec(a.shape, lambda i: (0,) * a.ndim)
    return pl.pallas_call(
        functools.partial(_final_kernel, nblk=nblk),
        grid=(t // rb,),
        in_specs=[row, tok, full(m), full(g)],
        out_specs=row,
        out_shape=jax.ShapeDtypeStruct((t, d), F32),
        compiler_params=_cparams(("arbitrary",)),
        name="final_norm",
    )(x1, y, m, g)


def _inproj_weight(w_in):
    d = w_in.shape[0]
    z = lambda n: jnp.zeros((d, n), w_in.dtype)
    perm = _rope_swap_perm()
    kr = w_in[:, 3168:3200]
    cols = [w_in[:, 0:2048], w_in[:, 2048:2848], z(W_C - 800),
            w_in[:, 2848:3040], z(256 - D_Q_RANK), w_in[:, 3040:3168], kr, kr[:, perm], z(LANE - 2 * D_ROPE)]
    return jnp.concatenate(cols, axis=1).astype(BF16)


def _hgrn_lower_bounds(logits):
    cum = jnp.cumsum(jax.nn.softmax(logits.astype(F32), axis=0), axis=0)
    return cum - cum[0]


def kernel(x, c, ctx, c_ctx, w_mod, b_mod, norm1_g, norm2_g, w_in, w_out, hgrn_lb_logits, hgrn_norm_g, na_rpb, gla_wg_f, gla_bg_f, gla_wg_b, gla_bg_b, gla_norm_g, mla_q_norm_g, mla_w_uq, mla_kv_norm_g, mla_w_ukv, moe_w_rg, moe_b_rg, moe_w_re, moe_b_re, moe_w_gu, moe_w_dn, final_norm_g):
    batch, n, d = x.shape
    l_ctx = ctx.shape[1]
    assert d == D_MODEL and l_ctx == ROW_BLOCK and n % ROW_BLOCK == 0 and batch <= 4
    s_len = l_ctx + n
    nblk = s_len // ROW_BLOCK
    depth = w_mod.shape[0]

    c8 = jnp.zeros((8, d), F32).at[0:batch].set(c).at[4].set(c_ctx)
    mods = _mod_vectors(c8, w_mod, b_mod)
    lower_bounds = _hgrn_lower_bounds(hgrn_lb_logits)
    tables = _mla_tables(n, l_ctx)

    xa = jnp.concatenate([ctx, x], axis=1).reshape(batch * s_len, d)
    y_prev = None
    for layer in range(depth):
        keep_ctx = layer < depth - 1
        m = mods[layer]
        xa, (pa, pb, pc, pd) = _inproj(xa, y_prev, mods[layer - 1] if layer else None, m,
                                       norm1_g[layer].reshape(1, d), _inproj_weight(w_in[layer]), nblk)
        oaf, oab = _hgrn_scan(pa, lower_bounds[layer], batch, nblk)
        ob = _neighbourhood_attention(pb, na_rpb[layer], batch, s_len, l_ctx, keep_ctx)
        ogf, ogb = _gla_scan(pc, gla_wg_f[layer], gla_bg_f[layer], gla_wg_b[layer], gla_bg_b[layer], batch, nblk)
        mla_w = _mla_weights(mla_q_norm_g[layer], mla_w_uq[layer], mla_kv_norm_g[layer], mla_w_ukv[layer])
        q, k, v = _mla_prep(pd, mla_w, tables, nblk)
        od = _mla_attention(q, k, v, batch, s_len, l_ctx, keep_ctx)
        xa, h2 = _outproj(oaf, oab, pa, ob, ogf, ogb, pc, od, xa, m, hgrn_norm_g[layer].reshape(1, -1),
                          gla_norm_g[layer].reshape(1, -1), norm2_g[layer].reshape(1, d), w_out[layer],
                          batch, nblk, keep_ctx)
        y_prev = _moe(h2, moe_w_rg[layer], moe_b_rg[layer], moe_w_re[layer], moe_b_re[layer],
                      moe_w_gu[layer].astype(BF16), moe_w_dn[layer].astype(BF16))
    out = _final(xa, y_prev, mods[depth - 1], final_norm_g.reshape(1, d), n // ROW_BLOCK)
    return out.reshape(batch, n, d)
```

```python
import functools

import numpy as np
import jax
import jax.numpy as jnp
from jax import lax
from jax.experimental import pallas as pl
from jax.experimental.pallas import tpu as pltpu

F32 = jnp.float32
BF16 = jnp.bfloat16
I32 = jnp.int32

D_MODEL = 1024
DEPTH = 2
GRID_W = 64
EPS = 1e-6
D_GROUP = 256
N_HEADS = 4
A_DH = 64
B_DH = 64
WIN_H = 8
WIN_W = 16
C_DK = 32
C_DV = 64
C_GATE_RANK = 16
C_GATE_NORM = 16.0
D_NOPE = 64
D_V = 64
D_ROPE = 32
ROPE_FREQS = 8
ROPE_BASE = 10000.0
D_Q_RANK = 192
D_KV_RANK = 128
MLA_SCALE = (D_NOPE + D_ROPE) ** -0.5
N_GROUPS = 4
EXPERTS_PER_GROUP = 8
D_EXPERT = 256

ROW_BLOCK = 256
SUB = 16
LANE = 128
HEAD_TILE = 128
W_A, W_B, W_C, W_D = 1280, 768, 896, 512
TOK_TILES = D_MODEL // LANE
VMEM_LIMIT = 52 * 1024 * 1024
NEG_INF = float("-inf")


def _bdot(a, b):
    return jnp.dot(a.astype(BF16), b.astype(BF16), preferred_element_type=F32)


def _bdot_nt(a, b):
    return lax.dot_general(a.astype(BF16), b.astype(BF16), (((1,), (1,)), ((), ())), preferred_element_type=F32)


def _bdot_tn(a, b):
    return lax.dot_general(a.astype(BF16), b.astype(BF16), (((0,), (0,)), ((), ())), preferred_element_type=F32)


def _split3(a):
    hi = a.astype(BF16)
    r1 = a - hi.astype(F32)
    mid = r1.astype(BF16)
    lo = (r1 - mid.astype(F32)).astype(BF16)
    return hi, mid, lo


def _dot_f32(a, b):
    ah, am, al = _split3(a)
    bh, bm, bl = _split3(b)
    d = lambda u, v: jnp.dot(u, v, preferred_element_type=F32)
    return d(ah, bh) + (d(ah, bm) + d(am, bh)) + (d(am, bm) + d(ah, bl) + d(al, bh))


def _dot_sel_l(sel, a):
    ah, am, al = _split3(a)
    d = lambda v: jnp.dot(sel, v, preferred_element_type=F32)
    return d(ah) + d(am) + d(al)


def _dot_sel_r(a, sel):
    ah, am, al = _split3(a)
    d = lambda u: jnp.dot(u, sel, preferred_element_type=F32)
    return d(ah) + d(am) + d(al)


def _sigmoid(x):
    return 1.0 / (1.0 + jnp.exp(-x))


def _log_sigmoid(x):
    return jnp.minimum(x, 0.0) - jnp.log1p(jnp.exp(-jnp.abs(x)))


def _logaddexp(a, b):
    amax = jnp.maximum(a, b)
    delta = a - b
    return jnp.where(jnp.isnan(delta), a + b, amax + jnp.log1p(jnp.exp(-jnp.abs(delta))))


def _rms(x, width):
    return x * lax.rsqrt(jnp.sum(x * x, axis=-1, keepdims=True) / width + EPS)


def _cparams(sem, vmem=VMEM_LIMIT):
    return pltpu.CompilerParams(dimension_semantics=sem, vmem_limit_bytes=vmem)


def _mod_kernel(c_ref, w_ref, b_ref, o_ref):
    c = c_ref[...]
    act = c * _sigmoid(c)
    o_ref[0] = _dot_f32(act, w_ref[0]) + b_ref[0]


def _mod_vectors(c8, w_mod, b_mod):
    depth, d, six_d = w_mod.shape
    nj = six_d // d
    return pl.pallas_call(
        _mod_kernel,
        grid=(depth, nj),
        in_specs=[
            pl.BlockSpec((8, d), lambda l, j: (0, 0)),
            pl.BlockSpec((1, d, d), lambda l, j: (l, 0, j)),
            pl.BlockSpec((1, 1, d), lambda l, j: (l, 0, j)),
        ],
        out_specs=pl.BlockSpec((1, 8, d), lambda l, j: (l, 0, j)),
        out_shape=jax.ShapeDtypeStruct((depth, 8, six_d), F32),
        compiler_params=_cparams(("arbitrary", "arbitrary")),
        name="mod_vectors",
    )(c8, w_mod, b_mod.reshape(depth, 1, six_d))


def _mod_row(m_ref, row):
    return m_ref[pl.ds(row, 1), :]


def _inproj_kernel(*refs, fuse_res, nblk):
    if fuse_res:
        x_ref, y_ref, mprev_ref, m_ref, g_ref, w_ref, xo_ref, pa_ref, pb_ref, pc_ref, pd_ref = refs
    else:
        x_ref, m_ref, g_ref, w_ref, pa_ref, pb_ref, pc_ref, pd_ref = refs
    i = pl.program_id(0)
    b = i // nblk
    row = jnp.where(i % nblk == 0, 4, b)
    d = D_MODEL
    x = x_ref[...]
    if fuse_res:
        mp = _mod_row(mprev_ref, row)
        x = x + mp[:, 5 * d:6 * d] * _load_token_rows(y_ref)
        xo_ref[...] = x
    m = _mod_row(m_ref, row)
    h = _rms(x, d) * g_ref[...] * (1.0 + m[:, d:2 * d]) + m[:, 0:d]
    p = _bdot(h, w_ref[...])
    pa_ref[...] = p[:, 0:W_A]
    pb_ref[...] = p[:, W_A:W_A + W_B].astype(pb_ref.dtype)
    pc_ref[...] = p[:, W_A + W_B:W_A + W_B + W_C]
    pd_ref[...] = p[:, W_A + W_B + W_C:]


def _inproj(x, y, m_prev, m, g, w, nblk):
    t, d = x.shape
    fuse = y is not None
    rb = ROW_BLOCK
    row_spec = lambda w_: pl.BlockSpec((rb, w_), lambda i: (i, 0))
    full = lambda a: pl.BlockSpec(a.shape, lambda i: (0,) * a.ndim)
    ins, specs = [x], [row_spec(d)]
    if fuse:
        ins += [y, m_prev]
        specs += [pl.BlockSpec((rb * TOK_TILES, LANE), lambda i: (i, 0)), full(m_prev)]
    ins += [m, g, w]
    specs += [full(m), full(g), full(w)]
    outs, ospecs = [], []
    if fuse:
        outs.append(jax.ShapeDtypeStruct((t, d), F32))
        ospecs.append(row_spec(d))
    for w_ in (W_A, W_B, W_C, W_D):
        outs.append(jax.ShapeDtypeStruct((t, w_), BF16 if w_ == W_B else F32))
        ospecs.append(row_spec(w_))
    res = pl.pallas_call(
        functools.partial(_inproj_kernel, fuse_res=fuse, nblk=nblk),
        grid=(t // rb,),
        in_specs=specs,
        out_specs=ospecs,
        out_shape=outs,
        compiler_params=_cparams(("arbitrary",)),
        name="in_projection",
    )(*ins)
    if fuse:
        return res[0], res[1:]
    return x, res


def _sub_chunk(refs, i, tri, emat, emask_t, reverse):
    q_ref, k_ref, la_ref, v_ref, o_ref, st_ref, r_ref = refs
    hv = v_ref.shape[1]
    row_id = lax.broadcasted_iota(I32, (SUB, 1), 0)
    off = pl.multiple_of(i * SUB, SUB)
    qs = q_ref[pl.ds(off, SUB), :]
    ks = k_ref[pl.ds(off, SUB), :]
    las = la_ref[pl.ds(off, SUB), :]
    vs = v_ref[pl.ds(off, SUB), :]
    cum = _dot_sel_l(tri, las)
    last = cum[0:1, :] if reverse else cum[SUB - 1:SUB, :]
    for j in range(SUB):
        valid = (row_id <= j) if reverse else (row_id >= j)
        dlt = jnp.where(valid, cum - cum[j:j + 1, :], NEG_INF)
        r_ref[j * SUB:(j + 1) * SUB, :] = (qs * ks[j:j + 1, :] * jnp.exp(dlt)).astype(BF16)
    att = jnp.dot(r_ref[...], emat, preferred_element_type=F32)
    o = jnp.zeros((SUB, hv), F32)
    for j in range(SUB):
        o = o + att[j * SUB:(j + 1) * SUB, :] * vs[j:j + 1, :]
    st = st_ref[...]
    o = o + _bdot_nt(qs * jnp.exp(cum), st)
    o_ref[pl.ds(off, SUB), :] = o
    kd = ks * jnp.exp(last - cum)
    st_ref[...] = st * jnp.exp(last) + _bdot_tn(vs, kd) * emask_t


def _scan_pair(fwd_refs, bwd_refs, tri_f, tri_b, emat, emask_t):
    n_sub = fwd_refs[0].shape[0] // SUB

    def body(step, carry):
        _sub_chunk(fwd_refs, step, tri_f, emat, emask_t, False)
        _sub_chunk(bwd_refs, n_sub - 1 - step, tri_b, emat, emask_t, True)
        return carry

    lax.fori_loop(0, n_sub, body, 0, unroll=2)


def _hgrn_prep(q_ref, v_ref, f_ref, lb, qo, ko, lo, vo):
    qr = q_ref[...]
    qo[...] = qr * _sigmoid(qr) * (A_DH ** -0.5)
    vo[...] = v_ref[...]
    z = f_ref[...]
    lo[...] = _logaddexp(jnp.log(lb), jnp.log1p(-lb) + _log_sigmoid(z))
    ko[...] = (1.0 - lb) * _sigmoid(-z)


def _hgrn_kernel(qf_ref, vf_ref, ff_ref, qb_ref, vb_ref, fb_ref, lb_ref, trif_ref, trib_ref, emat_ref, emask_ref,
                 of_ref, ob_ref, stf, stb, rf, rb_, q1, k1, l1, v1, q2, k2, l2, v2):
    @pl.when(pl.program_id(1) == 0)
    def _():
        stf[...] = jnp.zeros_like(stf)
        stb[...] = jnp.zeros_like(stb)

    _hgrn_prep(qf_ref, vf_ref, ff_ref, lb_ref[0:1, :], q1, k1, l1, v1)
    _hgrn_prep(qb_ref, vb_ref, fb_ref, lb_ref[1:2, :], q2, k2, l2, v2)
    emat = emat_ref[...]
    emask = emask_ref[...]
    _scan_pair((q1, k1, l1, v1, of_ref, stf, rf), (q2, k2, l2, v2, ob_ref, stb, rb_),
               trif_ref[...], trib_ref[...], emat, emask)


def _tri_consts():
    i = np.arange(SUB)
    trif = (i[None, :] <= i[:, None]).astype(np.float32)
    trib = (i[None, :] >= i[:, None]).astype(np.float32)
    return jnp.asarray(trif, BF16), jnp.asarray(trib, BF16)


def _head_match(hk, hv, dk, dv):
    m = (np.arange(hk)[:, None] // dk == np.arange(hv)[None, :] // dv).astype(np.float32)
    return m


def _bwd_block(t, nblk):
    return jnp.where(t == 0, 0, nblk - t)


def _hgrn_scan(pa, lb, batch, nblk):
    t_all = pa.shape[0]
    rb, w = ROW_BLOCK, D_GROUP
    trif, trib = _tri_consts()
    em = _head_match(w, w, A_DH, A_DH)
    emat, emask = jnp.asarray(em, BF16), jnp.asarray(em.T, F32)
    fwd = lambda col: pl.BlockSpec((rb, w), lambda b, t: (b * nblk + t, col))
    bwd = lambda col: pl.BlockSpec((rb, w), lambda b, t: (b * nblk + _bwd_block(t, nblk), col))
    full = lambda a: pl.BlockSpec(a.shape, lambda b, t: (0,) * a.ndim)
    vm = lambda shape, dt=F32: pltpu.VMEM(shape, dt)
    return pl.pallas_call(
        _hgrn_kernel,
        grid=(batch, nblk),
        in_specs=[fwd(0), fwd(1), fwd(2), bwd(0), bwd(1), bwd(3), full(lb), full(trif), full(trib), full(emat), full(emask)],
        out_specs=[fwd(0), bwd(0)],
        out_shape=[jax.ShapeDtypeStruct((t_all, w), F32)] * 2,
        scratch_shapes=[vm((w, w)), vm((w, w)), vm((SUB * SUB, w), BF16), vm((SUB * SUB, w), BF16)]
        + [vm((rb, w))] * 8,
        compiler_params=_cparams(("arbitrary", "arbitrary")),
        name="hgrn_scan",
    )(pa, pa, pa, pa, pa, pa, lb, trif, trib, emat, emask)


def _gla_prep(q_ref, k_ref, v_ref, z_ref, wg_ref, bg_ref, qo, ko, lo, vo):
    qo[...] = q_ref[...] * (C_DK ** -0.5)
    ko[...] = k_ref[...]
    vo[...] = v_ref[...]
    zl = _dot_f32(z_ref[...], wg_ref[...]) + bg_ref[...]
    lo[...] = _log_sigmoid(zl) / C_GATE_NORM


def _gla_kernel(qf_ref, kf_ref, vf_ref, zf_ref, qb_ref, kb_ref, vb_ref, zb_ref, wgf_ref, bgf_ref, wgb_ref, bgb_ref,
                trif_ref, trib_ref, emat_ref, emask_ref, of_ref, ob_ref, stf, stb, rf, rb_, q1, k1, l1, v1, q2, k2, l2, v2):
    @pl.when(pl.program_id(1) == 0)
    def _():
        stf[...] = jnp.zeros_like(stf)
        stb[...] = jnp.zeros_like(stb)

    _gla_prep(qf_ref, kf_ref, vf_ref, zf_ref, wgf_ref, bgf_ref, q1, k1, l1, v1)
    _gla_prep(qb_ref, kb_ref, vb_ref, zb_ref, wgb_ref, bgb_ref, q2, k2, l2, v2)
    emat = emat_ref[...]
    emask = emask_ref[...]
    _scan_pair((q1, k1, l1, v1, of_ref, stf, rf), (q2, k2, l2, v2, ob_ref, stb, rb_),
               trif_ref[...], trib_ref[...], emat, emask)


def _gla_scan(pc, wg_f, bg_f, wg_b, bg_b, batch, nblk):
    t_all = pc.shape[0]
    rb = ROW_BLOCK
    hk, hv = N_HEADS * C_DK, N_HEADS * C_DV
    trif, trib = _tri_consts()
    em = _head_match(hk, hv, C_DK, C_DV)
    emat, emask = jnp.asarray(em, BF16), jnp.asarray(em.T, F32)
    wgf = jnp.zeros((LANE, hk), F32).at[0:C_GATE_RANK].set(wg_f)
    wgb = jnp.zeros((LANE, hk), F32).at[C_GATE_RANK:2 * C_GATE_RANK].set(wg_b)
    bgf, bgb = bg_f.reshape(1, hk), bg_b.reshape(1, hk)
    fwd = lambda w, col: pl.BlockSpec((rb, w), lambda b, t: (b * nblk + t, col))
    bwd = lambda w, col: pl.BlockSpec((rb, w), lambda b, t: (b * nblk + _bwd_block(t, nblk), col))
    full = lambda a: pl.BlockSpec(a.shape, lambda b, t: (0,) * a.ndim)
    vm = lambda shape, dt=F32: pltpu.VMEM(shape, dt)
    return pl.pallas_call(
        _gla_kernel,
        grid=(batch, nblk),
        in_specs=[fwd(hk, 0), fwd(hk, 1), fwd(hv, 1), fwd(LANE, 6), bwd(hk, 0), bwd(hk, 1), bwd(hv, 1), bwd(LANE, 6),
                  full(wgf), full(bgf), full(wgb), full(bgb), full(trif), full(trib), full(emat), full(emask)],
        out_specs=[fwd(hv, 0), bwd(hv, 0)],
        out_shape=[jax.ShapeDtypeStruct((t_all, hv), F32)] * 2,
        scratch_shapes=[vm((hv, hk)), vm((hv, hk)), vm((SUB * SUB, hk), BF16), vm((SUB * SUB, hk), BF16)]
        + [vm((rb, hk)), vm((rb, hk)), vm((rb, hk)), vm((rb, hv))] * 2,
        compiler_params=_cparams(("arbitrary", "arbitrary")),
        name="gla_scan",
    )(pc, pc, pc, pc, pc, pc, pc, pc, wgf, bgf, wgb, bgb, trif, trib, emat, emask)


NA_QROWS = ROW_BLOCK // GRID_W
NA_WROWS = WIN_H + NA_QROWS


def _na_kernel(q_ref, k_ref, v_ref, bias_ref, hm_ref, o_ref, *, j0, rows, ctx):
    j = pl.program_id(1) + j0
    q = q_ref[...] * (B_DH ** -0.5)
    hm = hm_ref[...]
    kc = k_ref[0:ctx, :]
    vc = v_ref[0:ctx, :]
    nt = lambda a, b: lax.dot_general(a, b, (((1,), (1,)), ((), ())), preferred_element_type=F32)

    @pl.when(j == 0)
    def _():
        acc = jnp.zeros(q.shape, F32)
        for h in range(N_HEADS):
            mh = hm[h:h + 1, :]
            s = nt(q * mh.astype(BF16), kc)
            p = jnp.exp(s - jnp.max(s, axis=-1, keepdims=True))
            inv = 1.0 / jnp.sum(p, axis=-1, keepdims=True)
            acc = acc + jnp.dot(p.astype(BF16), vc, preferred_element_type=F32) * (mh * inv)
        o_ref[...] = acc.astype(o_ref.dtype)

    @pl.when(j > 0)
    def _():
        r0 = (j - 1) * NA_QROWS
        start = jnp.clip(r0 - WIN_H // 2, 0, rows - NA_WROWS)
        off = pl.multiple_of(ctx + start * GRID_W, GRID_W)
        kw = k_ref[pl.ds(off, NA_WROWS * GRID_W), :]
        vw = v_ref[pl.ds(off, NA_WROWS * GRID_W), :]
        acc = jnp.zeros(q.shape, F32)
        for h in range(N_HEADS):
            mh = hm[h:h + 1, :]
            qh = q * mh.astype(BF16)
            sw = nt(qh, kw) + bias_ref[0, h]
            sc = nt(qh, kc)
            m = jnp.maximum(jnp.max(sw, axis=-1, keepdims=True), jnp.max(sc, axis=-1, keepdims=True))
            pw = jnp.exp(sw - m)
            pc_ = jnp.exp(sc - m)
            inv = 1.0 / (jnp.sum(pw, axis=-1, keepdims=True) + jnp.sum(pc_, axis=-1, keepdims=True))
            o = (jnp.dot(pw.astype(BF16), vw, preferred_element_type=F32)
                 + jnp.dot(pc_.astype(BF16), vc, preferred_element_type=F32))
            acc = acc + o * (mh * inv)
        o_ref[...] = acc.astype(o_ref.dtype)


def _na_bias_table(rpb, rows):
    kh = WIN_H
    cidx = np.arange(GRID_W)
    c_start = np.clip(cidx - WIN_W // 2, 0, GRID_W - WIN_W)
    col_in = (cidx[None] >= c_start[:, None]) & (cidx[None] < c_start[:, None] + WIN_W)
    dc = np.clip(cidx[None] - cidx[:, None], -(WIN_W - 1), WIN_W - 1) + (WIN_W - 1)
    sel = (dc[None] == np.arange(2 * WIN_W - 1)[:, None, None]).astype(np.float32)
    by_col = jnp.einsum("hrc,cqw->hrqw", rpb.astype(F32), jnp.asarray(sel), precision=lax.Precision.HIGHEST)
    by_col = jnp.where(jnp.asarray(col_in)[None, None], by_col, NEG_INF)
    blocked = jnp.full((N_HEADS, GRID_W, GRID_W), NEG_INF, F32)
    tables = []
    for r0 in (0, NA_QROWS, rows - NA_QROWS):
        ws = int(np.clip(r0 - kh // 2, 0, rows - NA_WROWS))
        per_q = []
        for a in range(NA_QROWS):
            r = r0 + a
            s = int(np.clip(r - kh // 2, 0, rows - kh))
            per_w = [by_col[:, ws + jj - r + kh - 1] if s <= ws + jj < s + kh else blocked for jj in range(NA_WROWS)]
            per_q.append(jnp.stack(per_w, axis=2))
        tables.append(jnp.stack(per_q, axis=1))
    return jnp.stack(tables, axis=0).reshape(3, N_HEADS, NA_QROWS * GRID_W, NA_WROWS * GRID_W)


def _head_masks(width, dh):
    return jnp.asarray((np.arange(width)[None, :] // dh == np.arange(N_HEADS)[:, None]).astype(np.float32))


def _neighbourhood_attention(pb, rpb, batch, s_len, ctx, keep_ctx):
    rows = (s_len - ctx) // GRID_W
    assert rows >= NA_WROWS and rows % NA_QROWS == 0 and ctx == ROW_BLOCK
    rb = ROW_BLOCK
    j0 = 0 if keep_ctx else 1
    per_b = s_len // rb
    bias = _na_bias_table(rpb, rows)
    hm = _head_masks(D_GROUP, B_DH)

    def bias_idx(b, jj):
        r0 = (jj + j0 - 1) * NA_QROWS
        return (jnp.where(r0 <= 0, 0, jnp.where(r0 == rows - NA_QROWS, 2, 1)), 0, 0, 0)

    return pl.pallas_call(
        functools.partial(_na_kernel, j0=j0, rows=rows, ctx=ctx),
        grid=(batch, per_b - j0),
        in_specs=[
            pl.BlockSpec((rb, D_GROUP), lambda b, jj: (b * per_b + jj + j0, 0)),
            pl.BlockSpec((s_len, D_GROUP), lambda b, jj: (b, 1)),
            pl.BlockSpec((s_len, D_GROUP), lambda b, jj: (b, 2)),
            pl.BlockSpec((1, N_HEADS, rb, NA_WROWS * GRID_W), bias_idx),
            pl.BlockSpec(hm.shape, lambda b, jj: (0, 0)),
        ],
        out_specs=pl.BlockSpec((rb, D_GROUP), lambda b, jj: (b * (per_b - j0) + jj, 0)),
        out_shape=jax.ShapeDtypeStruct((batch * (per_b - j0) * rb, D_GROUP), BF16),
        compiler_params=_cparams(("arbitrary", "arbitrary")),
        name="neighbourhood_attention",
    )(pb, pb, pb, bias, hm)


def _mla_prep_kernel(pd_ref, gq_ref, gkv_ref, wq1_ref, wq2_ref, wkv_ref, cq_ref, sq_ref, tk_ref, place_ref,
                     q_ref, k_ref, v_ref):
    pd = pd_ref[...]
    cq = pd[:, 0:256]
    ckv = pd[:, 256:384]
    kr = pd[:, 384:512]
    qn = _rms(cq, D_Q_RANK) * gq_ref[...]
    q = _bdot(qn, wq1_ref[...]) * cq_ref[...] + _bdot(qn, wq2_ref[...]) * sq_ref[...]
    q_ref[...] = q.astype(BF16)
    kvn = _rms(ckv, D_KV_RANK) * gkv_ref[...]
    kv = _bdot(kvn, wkv_ref[...])
    hw = N_HEADS * HEAD_TILE
    k = kv[:, 0:hw] + _dot_sel_r(kr * tk_ref[...], place_ref[...])
    k_ref[...] = k.astype(BF16)
    v_ref[...] = kv[:, hw:].astype(BF16)


def _rope_swap_perm():
    f = ROPE_FREQS
    return np.concatenate([np.arange(f, 2 * f), np.arange(0, f), np.arange(3 * f, 4 * f), np.arange(2 * f, 3 * f)])


def _mla_tables(n, ctx):
    t = np.arange(n)
    inv_freq = ROPE_BASE ** (-np.arange(ROPE_FREQS, dtype=np.float32) / ROPE_FREQS)
    ang_r = (t // GRID_W).astype(np.float32)[:, None] * inv_freq
    ang_c = (t % GRID_W).astype(np.float32)[:, None] * inv_freq
    cos32 = np.concatenate([np.cos(ang_r), np.cos(ang_r), np.cos(ang_c), np.cos(ang_c)], axis=1)
    sin32 = np.concatenate([-np.sin(ang_r), np.sin(ang_r), -np.sin(ang_c), np.sin(ang_c)], axis=1)
    cos32 = np.concatenate([np.ones((ctx, D_ROPE), np.float32), cos32.astype(np.float32)], axis=0)
    sin32 = np.concatenate([np.zeros((ctx, D_ROPE), np.float32), sin32.astype(np.float32)], axis=0)
    s_len = n + ctx
    cq = np.zeros((s_len, N_HEADS, HEAD_TILE), np.float32)
    sq = np.zeros((s_len, N_HEADS, HEAD_TILE), np.float32)
    cq[:, :, 0:D_NOPE] = MLA_SCALE
    cq[:, :, D_NOPE:D_NOPE + D_ROPE] = cos32[:, None, :] * MLA_SCALE
    sq[:, :, D_NOPE:D_NOPE + D_ROPE] = sin32[:, None, :] * MLA_SCALE
    tk = np.zeros((s_len, LANE), np.float32)
    tk[:, 0:D_ROPE] = cos32
    tk[:, D_ROPE:2 * D_ROPE] = sin32
    place = np.zeros((LANE, N_HEADS * HEAD_TILE), np.float32)
    for h in range(N_HEADS):
        for l in range(D_ROPE):
            place[l, h * HEAD_TILE + D_NOPE + l] = 1.0
            place[D_ROPE + l, h * HEAD_TILE + D_NOPE + l] = 1.0
    return (jnp.asarray(cq.reshape(s_len, -1)), jnp.asarray(sq.reshape(s_len, -1)), jnp.asarray(tk),
            jnp.asarray(place, BF16))


def _mla_weights(q_norm_g, w_uq, kv_norm_g, w_ukv):
    perm = _rope_swap_perm()
    wq = w_uq.reshape(D_Q_RANK, N_HEADS, D_NOPE + D_ROPE)
    wq1 = jnp.zeros((256, N_HEADS, HEAD_TILE), F32).at[0:D_Q_RANK, :, 0:D_NOPE + D_ROPE].set(wq)
    wq2 = jnp.zeros((256, N_HEADS, HEAD_TILE), F32).at[0:D_Q_RANK, :, D_NOPE:D_NOPE + D_ROPE].set(
        wq[:, :, D_NOPE:][:, :, perm])
    wkv = w_ukv.reshape(D_KV_RANK, N_HEADS, D_NOPE + D_V)
    wk = jnp.zeros((D_KV_RANK, N_HEADS, HEAD_TILE), F32).at[:, :, 0:D_NOPE].set(wkv[:, :, 0:D_NOPE])
    wv = jnp.zeros((D_KV_RANK, N_HEADS, HEAD_TILE), F32).at[:, :, 0:D_V].set(wkv[:, :, D_NOPE:])
    hw = N_HEADS * HEAD_TILE
    wkv_p = jnp.concatenate([wk.reshape(D_KV_RANK, hw), wv.reshape(D_KV_RANK, hw)], axis=1)
    gq = jnp.zeros((1, 256), F32).at[0, 0:D_Q_RANK].set(q_norm_g)
    return (gq, kv_norm_g.reshape(1, D_KV_RANK), wq1.reshape(256, hw).astype(BF16), wq2.reshape(256, hw).astype(BF16),
            wkv_p.astype(BF16))


def _mla_prep(pd, weights, tables, nblk):
    t_all = pd.shape[0]
    rb = ROW_BLOCK
    gq, gkv, wq1, wq2, wkv = weights
    cq, sq, tk, place = tables
    hw = N_HEADS * HEAD_TILE
    row = lambda w: pl.BlockSpec((rb, w), lambda i: (i, 0))
    pos = lambda w: pl.BlockSpec((rb, w), lambda i: (i % nblk, 0))
    full = lambda a: pl.BlockSpec(a.shape, lambda i: (0,) * a.ndim)
    return pl.pallas_call(
        _mla_prep_kernel,
        grid=(t_all // rb,),
        in_specs=[row(W_D), full(gq), full(gkv), full(wq1), full(wq2), full(wkv), pos(hw), pos(hw), pos(LANE), full(place)],
        out_specs=[row(hw)] * 3,
        out_shape=[jax.ShapeDtypeStruct((t_all, hw), BF16)] * 3,
        compiler_params=_cparams(("arbitrary",)),
        name="mla_prep",
    )(pd, gq, gkv, wq1, wq2, wkv, cq, sq, tk, place)


def _mla_attn_kernel(q_ref, k_ref, v_ref, o_ref, *, j0, ctx):
    j = pl.program_id(2) + j0
    q = q_ref[...]

    def attend(k, v):
        s = lax.dot_general(q, k, (((1,), (1,)), ((), ())), preferred_element_type=F32)
        p = jnp.exp(s - jnp.max(s, axis=-1, keepdims=True))
        inv = 1.0 / jnp.sum(p, axis=-1, keepdims=True)
        return (jnp.dot(p.astype(BF16), v, preferred_element_type=F32) * inv).astype(o_ref.dtype)

    @pl.when(j == 0)
    def _():
        o_ref[...] = attend(k_ref[0:ctx, :], v_ref[0:ctx, :])

    @pl.when(j > 0)
    def _():
        o_ref[...] = attend(k_ref[...], v_ref[...])


def _mla_attention(q, k, v, batch, s_len, ctx, keep_ctx):
    t_all = q.shape[0]
    rb = ROW_BLOCK
    nblk = s_len // rb
    j0 = 0 if keep_ctx else 1
    return pl.pallas_call(
        functools.partial(_mla_attn_kernel, j0=j0, ctx=ctx),
        grid=(batch, N_HEADS, nblk - j0),
        in_specs=[
            pl.BlockSpec((rb, HEAD_TILE), lambda b, h, jj: (b * nblk + jj + j0, h)),
            pl.BlockSpec((s_len, HEAD_TILE), lambda b, h, jj: (b, h)),
            pl.BlockSpec((s_len, HEAD_TILE), lambda b, h, jj: (b, h)),
        ],
        out_specs=pl.BlockSpec((rb, HEAD_TILE), lambda b, h, jj: (b * (nblk - j0) + jj, h)),
        out_shape=jax.ShapeDtypeStruct((batch * (nblk - j0) * rb, N_HEADS * HEAD_TILE), BF16),
        compiler_params=_cparams(("arbitrary", "arbitrary", "arbitrary")),
        name="mla_attention",
    )(q, k, v)


def _outproj_kernel(oaf_ref, oab_ref, ga_ref, ob_ref, ogf_ref, ogb_ref, gc_ref, od_ref, x_ref, m_ref, gha_ref, ghc_ref,
                    g2_ref, wa_ref, wb_ref, wc_ref, wd_ref, em_ref, x1_ref, h2_ref, *, j0, nblk_out):
    i = pl.program_id(0)
    b = i // nblk_out
    row = jnp.where(i % nblk_out + j0 == 0, 4, b)
    d = D_MODEL
    em = em_ref[...]

    def readout(o, g_norm, gate, dh):
        ms = _dot_sel_r(o * o, em) / dh
        return o * lax.rsqrt(ms + EPS) * g_norm * (gate * _sigmoid(gate))

    a = readout(oaf_ref[...] + oab_ref[...], gha_ref[...], ga_ref[...], A_DH)
    c = readout(ogf_ref[...] + ogb_ref[...], ghc_ref[...], gc_ref[...], C_DV)
    mix = (_bdot(a, wa_ref[...]) + jnp.dot(ob_ref[...], wb_ref[...], preferred_element_type=F32)
           + _bdot(c, wc_ref[...]) + jnp.dot(od_ref[...], wd_ref[...], preferred_element_type=F32))
    m = _mod_row(m_ref, row)
    x1 = x_ref[...] + m[:, 2 * d:3 * d] * mix
    x1_ref[...] = x1
    _store_token_rows(h2_ref, _rms(x1, d) * g2_ref[...] * (1.0 + m[:, 4 * d:5 * d]) + m[:, 3 * d:4 * d])


def _outproj(oaf, oab, pa, ob, ogf, ogb, pc, od, x, m, gha, ghc, g2, w_out, batch, nblk, keep_ctx):
    rb, d = ROW_BLOCK, D_MODEL
    j0 = 0 if keep_ctx else 1
    nblk_out = nblk - j0
    t_out = batch * nblk_out * rb
    wa = w_out[0:256].astype(BF16)
    wb = w_out[256:512].astype(BF16)
    wc = w_out[512:768].astype(BF16)
    wd = jnp.zeros((N_HEADS, HEAD_TILE, d), F32).at[:, 0:D_V, :].set(w_out[768:1024].reshape(N_HEADS, D_V, d))
    wd = wd.reshape(N_HEADS * HEAD_TILE, d).astype(BF16)
    em = jnp.asarray(_head_match(D_GROUP, D_GROUP, 64, 64), BF16)
    src = lambda i: (i // nblk_out) * nblk + i % nblk_out + j0
    row_in = lambda w, col=0: pl.BlockSpec((rb, w), lambda i: (src(i), col))
    row_out = lambda w: pl.BlockSpec((rb, w), lambda i: (i, 0))
    full = lambda a_: pl.BlockSpec(a_.shape, lambda i: (0,) * a_.ndim)
    return pl.pallas_call(
        functools.partial(_outproj_kernel, j0=j0, nblk_out=nblk_out),
        grid=(t_out // rb,),
        in_specs=[row_in(256), row_in(256), row_in(256, 4), row_out(256), row_in(256), row_in(256), row_in(256, 2),
                  row_out(N_HEADS * HEAD_TILE), row_in(d), full(m), full(gha), full(ghc), full(g2),
                  full(wa), full(wb), full(wc), full(wd), full(em)],
        out_specs=[row_out(d), pl.BlockSpec((rb * TOK_TILES, LANE), lambda i: (i, 0))],
        out_shape=[jax.ShapeDtypeStruct((t_out, d), F32), jax.ShapeDtypeStruct((t_out * TOK_TILES, LANE), F32)],
        compiler_params=_cparams(("arbitrary",)),
        name="out_projection",
    )(oaf, oab, pa, ob, ogf, ogb, pc, od, x, m, gha, ghc, g2, wa, wb, wc, wd, em)


def _load_token_rows(ref):
    rows = ref.shape[0] // TOK_TILES
    return jnp.concatenate([ref[pl.ds(k, rows, stride=TOK_TILES), :] for k in range(TOK_TILES)], axis=1)


def _store_token_rows(ref, val):
    rows = val.shape[0]
    for k in range(TOK_TILES):
        ref[pl.ds(k, rows, stride=TOK_TILES), :] = val[:, k * LANE:(k + 1) * LANE]


def _router_logits(h, wr_ref, br_ref):
    lg = _dot_f32(h, wr_ref[...]) + br_ref[...]
    lane = lax.broadcasted_iota(I32, lg.shape, 1).astype(F32)
    return lg, lane


def _top_group(lg, lane):
    gl = jnp.where(lane < N_GROUPS, lg, NEG_INF)
    gmax = jnp.max(gl, axis=-1, keepdims=True)
    gsel = jnp.min(jnp.where(gl == gmax, lane, float(LANE)), axis=-1, keepdims=True)
    p_group = 1.0 / jnp.sum(jnp.exp(gl - gmax), axis=-1, keepdims=True)
    return gsel, p_group


def _expert_gates(lg, lane, lo, p_group):
    big = float(LANE)
    el = jnp.where((lane >= lo) & (lane < lo + EXPERTS_PER_GROUP), lg, NEG_INF)
    m1 = jnp.max(el, axis=-1, keepdims=True)
    i1 = jnp.min(jnp.where(el == m1, lane, big), axis=-1, keepdims=True)
    el2 = jnp.where(lane == i1, NEG_INF, el)
    m2 = jnp.max(el2, axis=-1, keepdims=True)
    i2 = jnp.min(jnp.where(el2 == m2, lane, big), axis=-1, keepdims=True)
    t = jnp.exp(m2 - m1)
    w1 = p_group / (1.0 + t)
    w2 = p_group * t / (1.0 + t)
    return jnp.where(lane == i1 - lo, w1, jnp.where(lane == i2 - lo, w2, 0.0))


def _router_kernel(h_ref, wr_ref, br_ref, tril_ref, meta_ref, cnt_ref, carry):
    @pl.when(pl.program_id(0) == 0)
    def _():
        carry[...] = jnp.zeros_like(carry)

    lg, lane = _router_logits(_load_token_rows(h_ref), wr_ref, br_ref)
    gsel, _ = _top_group(lg, lane)
    onehot = jnp.where(lane == gsel, 1.0, 0.0)
    incl = jnp.dot(tril_ref[...], onehot.astype(BF16), preferred_element_type=F32)
    rank = jnp.sum(onehot * (incl - 1.0 + carry[...]), axis=-1, keepdims=True)
    carry[...] = carry[...] + jnp.sum(onehot, axis=0, keepdims=True)
    meta_ref[...] = jnp.where(lane == 0, gsel, jnp.where(lane == 1, rank, 0.0)).astype(I32)
    cnt_ref[...] = carry[...].astype(I32)


def _router_weights(w_rg, b_rg, w_re, b_re):
    d = w_rg.shape[0]
    ne = N_GROUPS * EXPERTS_PER_GROUP
    wr = jnp.zeros((d, LANE), F32).at[:, 0:N_GROUPS].set(w_rg).at[:, N_GROUPS:N_GROUPS + ne].set(w_re)
    br = jnp.zeros((1, LANE), F32).at[0, 0:N_GROUPS].set(b_rg).at[0, N_GROUPS:N_GROUPS + ne].set(b_re)
    return wr, br


def _router(h2t, wr, br):
    t = h2t.shape[0] // TOK_TILES
    rb = ROW_BLOCK
    tril = jnp.asarray(np.tril(np.ones((rb, rb), np.float32)), BF16)
    full = lambda a: pl.BlockSpec(a.shape, lambda i: (0,) * a.ndim)
    return pl.pallas_call(
        _router_kernel,
        grid=(t // rb,),
        in_specs=[pl.BlockSpec((rb * TOK_TILES, LANE), lambda i: (i, 0)), full(wr), full(br), full(tril)],
        out_specs=[pl.BlockSpec((rb, LANE), lambda i: (i, 0)), pl.BlockSpec((1, LANE), lambda i: (0, 0))],
        out_shape=[jax.ShapeDtypeStruct((t, LANE), I32), jax.ShapeDtypeStruct((1, LANE), I32)],
        scratch_shapes=[pltpu.VMEM((1, LANE), F32)],
        compiler_params=_cparams(("arbitrary",)),
        name="moe_router",
    )(h2t, wr, br, tril)


def _invert_kernel(dest_ref, inv_ref):
    def clear(s, c):
        inv_ref[s] = -1
        return c

    lax.fori_loop(0, inv_ref.shape[0], clear, 0, unroll=8)

    def put(t, c):
        inv_ref[dest_ref[t]] = t
        return c

    lax.fori_loop(0, dest_ref.shape[0], put, 0, unroll=8)


def _invert(dest, n_slots):
    smem = pl.BlockSpec(memory_space=pltpu.SMEM)
    return pl.pallas_call(
        _invert_kernel, in_specs=[smem], out_specs=smem,
        out_shape=jax.ShapeDtypeStruct((n_slots,), I32), name="moe_invert",
    )(dest)


def _token_copy(src_ref, dst_ref, s, d, sem):
    s8 = pl.multiple_of(s * TOK_TILES, TOK_TILES)
    d8 = pl.multiple_of(d * TOK_TILES, TOK_TILES)
    return pltpu.make_async_copy(src_ref.at[pl.ds(s8, TOK_TILES), :], dst_ref.at[pl.ds(d8, TOK_TILES), :], sem)


def _experts_kernel(bg_ref, nr_ref, inv_ref, h_ref, wr_ref, br_ref, wgu_ref, wdn_ref, y_ref, xbuf, ybuf, gsem, ssem):
    i = pl.program_id(0)
    n_steps = pl.num_programs(0)
    rb, d = ROW_BLOCK, D_MODEL
    slot = i % 2

    def gather(blk, sl, start):
        def one(r, c):
            tok = jnp.maximum(inv_ref[blk * rb + r], 0)
            cp = _token_copy(h_ref, xbuf.at[sl], tok, r, gsem.at[sl])
            cp.start() if start else cp.wait()
            return c
        lax.fori_loop(0, rb, one, 0, unroll=8)

    def scatter(blk, sl, start):
        def one(r, c):
            cp = _token_copy(ybuf.at[sl], y_ref, r, inv_ref[blk * rb + r], ssem.at[sl])
            cp.start() if start else cp.wait()
            return c
        lax.fori_loop(0, nr_ref[blk], one, 0)

    @pl.when((i == 0) & (nr_ref[0] > 0))
    def _():
        gather(0, 0, True)

    @pl.when(i + 1 < n_steps)
    def _():
        @pl.when(nr_ref[i + 1] > 0)
        def _():
            gather(i + 1, 1 - slot, True)

    @pl.when(i >= 2)
    def _():
        scatter(i - 2, slot, False)

    @pl.when(nr_ref[i] > 0)
    def _():
        gather(i, slot, False)
        xf = _load_token_rows(xbuf.at[slot])
        lg, lane = _router_logits(xf, wr_ref, br_ref)
        _, p_group = _top_group(lg, lane)
        lo = (N_GROUPS + bg_ref[i] * EXPERTS_PER_GROUP).astype(F32)
        gates = _expert_gates(lg, lane, lo, p_group)
        x = xf.astype(BF16)
        acc = jnp.zeros((rb, d), F32)
        for e in range(EXPERTS_PER_GROUP):
            gu = jnp.dot(x, wgu_ref[0, e], preferred_element_type=F32)
            g = gu[:, 0:D_EXPERT]
            act = g * _sigmoid(g) * gu[:, D_EXPERT:]
            y = jnp.dot(act.astype(BF16), wdn_ref[0, e], preferred_element_type=F32)
            acc = acc + gates[:, e:e + 1] * y
        _store_token_rows(ybuf.at[slot], acc)
        scatter(i, slot, True)

    @pl.when(i == n_steps - 1)
    def _():
        @pl.when(i >= 1)
        def _():
            scatter(i - 1, 1 - slot, False)
        scatter(i, slot, False)


def _experts(block_group, block_rows, inv, h2t, wr, br, w_gu, w_dn):
    n_blocks = block_group.shape[0]
    rb, d = ROW_BLOCK, D_MODEL
    wgu = w_gu.reshape(N_GROUPS, EXPERTS_PER_GROUP, d, 2 * D_EXPERT)
    wdn = w_dn.reshape(N_GROUPS, EXPERTS_PER_GROUP, D_EXPERT, d)
    any_spec = pl.BlockSpec(memory_space=pl.ANY)
    return pl.pallas_call(
        _experts_kernel,
        grid_spec=pltpu.PrefetchScalarGridSpec(
            num_scalar_prefetch=3, grid=(n_blocks,),
            in_specs=[
                any_spec,
                pl.BlockSpec(wr.shape, lambda i, bg, nr, inv_: (0, 0)),
                pl.BlockSpec(br.shape, lambda i, bg, nr, inv_: (0, 0)),
                pl.BlockSpec((1, EXPERTS_PER_GROUP, d, 2 * D_EXPERT), lambda i, bg, nr, inv_: (bg[i], 0, 0, 0)),
                pl.BlockSpec((1, EXPERTS_PER_GROUP, D_EXPERT, d), lambda i, bg, nr, inv_: (bg[i], 0, 0, 0)),
            ],
            out_specs=any_spec,
            scratch_shapes=[pltpu.VMEM((2, rb * TOK_TILES, LANE), F32), pltpu.VMEM((2, rb * TOK_TILES, LANE), F32),
                            pltpu.SemaphoreType.DMA((2,)), pltpu.SemaphoreType.DMA((2,))]),
        out_shape=jax.ShapeDtypeStruct(h2t.shape, F32),
        compiler_params=pltpu.CompilerParams(dimension_semantics=("arbitrary",), vmem_limit_bytes=VMEM_LIMIT,
                                             has_side_effects=True),
        name="moe_experts",
    )(block_group, block_rows, inv, h2t, wr, br, wgu, wdn)


def _moe(h2t, w_rg, b_rg, w_re, b_re, w_gu_bf16, w_dn_bf16):
    t = h2t.shape[0] // TOK_TILES
    rb = ROW_BLOCK
    wr, br = _router_weights(w_rg, b_rg, w_re, b_re)
    meta, counts = _router(h2t, wr, br)
    group, rank = meta[:, 0], meta[:, 1]
    cnt = counts[0, 0:N_GROUPS]
    padded = (cnt + rb - 1) // rb * rb
    seg_end = jnp.cumsum(padded)
    seg_start = seg_end - padded
    dest = seg_start[group] + rank
    n_blocks = t // rb + N_GROUPS
    block_start = jnp.arange(n_blocks, dtype=I32) * rb
    block_group = jnp.minimum(jnp.sum((block_start[:, None] >= seg_end[None, :]).astype(I32), axis=1), N_GROUPS - 1)
    block_rows = jnp.clip((seg_start + cnt)[block_group] - block_start, 0, rb).astype(I32)
    inv = _invert(dest, n_blocks * rb)
    return _experts(block_group, block_rows, inv, h2t, wr, br, w_gu_bf16, w_dn_bf16)


def _final_kernel(x_ref, y_ref, m_ref, g_ref, o_ref, *, nblk):
    b = pl.program_id(0) // nblk
    d = D_MODEL
    m = _mod_row(m_ref, b)
    x = x_ref[...] + m[:, 5 * d:6 * d] * _load_token_rows(y_ref)
    o_ref[...] = _rms(x, d) * g_ref[...]


def _final(x1, y, m, g, nblk):
    t, d = x1.shape
    rb = ROW_BLOCK
    row = pl.BlockSpec((rb, d), lambda i: (i, 0))
    tok = pl.BlockSpec((rb * TOK_TILES, LANE), lambda i: (i, 0))
    full = lambda a: pl.BlockSpec(a.shape, lambda i: (0,) * a.ndim)
    return pl.pallas_call(
        functools.partial(_final_kernel, nblk=nblk),
        grid=(t // rb,),
        in_specs=[row, tok, full(m), full(g)],
        out_specs=row,
        out_shape=jax.ShapeDtypeStruct((t, d), F32),
        compiler_params=_cparams(("arbitrary",)),
        name="final_norm",
    )(x1, y, m, g)


def _inproj_weight(w_in):
    d = w_in.shape[0]
    z = lambda n: jnp.zeros((d, n), w_in.dtype)
    perm = _rope_swap_perm()
    kr = w_in[:, 3168:3200]
    cols = [w_in[:, 0:2048], w_in[:, 2048:2848], z(W_C - 800),
            w_in[:, 2848:3040], z(256 - D_Q_RANK), w_in[:, 3040:3168], kr, kr[:, perm], z(LANE - 2 * D_ROPE)]
    return jnp.concatenate(cols, axis=1).astype(BF16)


def _hgrn_lower_bounds(logits):
    cum = jnp.cumsum(jax.nn.softmax(logits.astype(F32), axis=0), axis=0)
    return cum - cum[0]


def kernel(x, c, ctx, c_ctx, w_mod, b_mod, norm1_g, norm2_g, w_in, w_out, hgrn_lb_logits, hgrn_norm_g, na_rpb, gla_wg_f, gla_bg_f, gla_wg_b, gla_bg_b, gla_norm_g, mla_q_norm_g, mla_w_uq, mla_kv_norm_g, mla_w_ukv, moe_w_rg, moe_b_rg, moe_w_re, moe_b_re, moe_w_gu, moe_w_dn, final_norm_g):
    batch, n, d = x.shape
    l_ctx = ctx.shape[1]
    assert d == D_MODEL and l_ctx == ROW_BLOCK and n % ROW_BLOCK == 0 and batch <= 4
    s_len = l_ctx + n
    nblk = s_len // ROW_BLOCK
    depth = w_mod.shape[0]

    c8 = jnp.zeros((8, d), F32).at[0:batch].set(c).at[4].set(c_ctx)
    mods = _mod_vectors(c8, w_mod, b_mod)
    lower_bounds = _hgrn_lower_bounds(hgrn_lb_logits)
    tables = _mla_tables(n, l_ctx)

    xa = jnp.concatenate([ctx, x], axis=1).reshape(batch * s_len, d)
    y_prev = None
    for layer in range(depth):
        keep_ctx = layer < depth - 1
        m = mods[layer]
        xa, (pa, pb, pc, pd) = _inproj(xa, y_prev, mods[layer - 1] if layer else None, m,
                                       norm1_g[layer].reshape(1, d), _inproj_weight(w_in[layer]), nblk)
        oaf, oab = _hgrn_scan(pa, lower_bounds[layer], batch, nblk)
        ob = _neighbourhood_attention(pb, na_rpb[layer], batch, s_len, l_ctx, keep_ctx)
        ogf, ogb = _gla_scan(pc, gla_wg_f[layer], gla_bg_f[layer], gla_wg_b[layer], gla_bg_b[layer], batch, nblk)
        mla_w = _mla_weights(mla_q_norm_g[layer], mla_w_uq[layer], mla_kv_norm_g[layer], mla_w_ukv[layer])
        q, k, v = _mla_prep(pd, mla_w, tables, nblk)
        od = _mla_attention(q, k, v, batch, s_len, l_ctx, keep_ctx)
        xa, h2 = _outproj(oaf, oab, pa, ob, ogf, ogb, pc, od, xa, m, hgrn_norm_g[layer].reshape(1, -1),
                          gla_norm_g[layer].reshape(1, -1), norm2_g[layer].reshape(1, d), w_out[layer],
                          batch, nblk, keep_ctx)
        y_prev = _moe(h2, moe_w_rg[layer], moe_b_rg[layer], moe_w_re[layer], moe_b_re[layer],
                      moe_w_gu[layer].astype(BF16), moe_w_dn[layer].astype(BF16))
    out = _final(xa, y_prev, mods[depth - 1], final_norm_g.reshape(1, d), n // ROW_BLOCK)
    return out.reshape(batch, n, d)
```

```python
import functools

import numpy as np
import jax
import jax.numpy as jnp
from jax import lax
from jax.experimental import pallas as pl
from jax.experimental.pallas import tpu as pltpu

F32 = jnp.float32
BF16 = jnp.bfloat16
I32 = jnp.int32

D_MODEL = 1024
DEPTH = 2
GRID_W = 64
EPS = 1e-6
D_GROUP = 256
N_HEADS = 4
A_DH = 64
B_DH = 64
WIN_H = 8
WIN_W = 16
C_DK = 32
C_DV = 64
C_GATE_RANK = 16
C_GATE_NORM = 16.0
D_NOPE = 64
D_V = 64
D_ROPE = 32
ROPE_FREQS = 8
ROPE_BASE = 10000.0
D_Q_RANK = 192
D_KV_RANK = 128
MLA_SCALE = (D_NOPE + D_ROPE) ** -0.5
N_GROUPS = 4
EXPERTS_PER_GROUP = 8
D_EXPERT = 256

ROW_BLOCK = 256
SUB = 16
MACRO = 64
DECAY_GUARD = 60.0
N_SCAN_CONSTS = 9
LANE = 128
HEAD_TILE = 128
W_A, W_B, W_C, W_D = 1280, 768, 896, 512
TOK_TILES = D_MODEL // LANE
VMEM_LIMIT = 52 * 1024 * 1024
NEG_INF = float("-inf")


def _bdot(a, b):
    return jnp.dot(a.astype(BF16), b.astype(BF16), preferred_element_type=F32)


def _bdot_nt(a, b):
    return lax.dot_general(a.astype(BF16), b.astype(BF16), (((1,), (1,)), ((), ())), preferred_element_type=F32)


def _bdot_tn(a, b):
    return lax.dot_general(a.astype(BF16), b.astype(BF16), (((0,), (0,)), ((), ())), preferred_element_type=F32)


def _split3(a):
    hi = a.astype(BF16)
    r1 = a - hi.astype(F32)
    mid = r1.astype(BF16)
    lo = (r1 - mid.astype(F32)).astype(BF16)
    return hi, mid, lo


def _dot_f32(a, b):
    ah, am, al = _split3(a)
    bh, bm, bl = _split3(b)
    d = lambda u, v: jnp.dot(u, v, preferred_element_type=F32)
    return d(ah, bh) + (d(ah, bm) + d(am, bh)) + (d(am, bm) + d(ah, bl) + d(al, bh))


def _dot_sel_l(sel, a):
    ah, am, al = _split3(a)
    d = lambda v: jnp.dot(sel, v, preferred_element_type=F32)
    return d(ah) + d(am) + d(al)


def _dot_sel_r(a, sel):
    ah, am, al = _split3(a)
    d = lambda u: jnp.dot(u, sel, preferred_element_type=F32)
    return d(ah) + d(am) + d(al)


def _sigmoid(x):
    return 1.0 / (1.0 + jnp.exp(-x))


def _log_sigmoid(x):
    return jnp.minimum(x, 0.0) - jnp.log1p(jnp.exp(-jnp.abs(x)))


def _logaddexp(a, b):
    amax = jnp.maximum(a, b)
    delta = a - b
    return jnp.where(jnp.isnan(delta), a + b, amax + jnp.log1p(jnp.exp(-jnp.abs(delta))))


def _rms(x, width):
    return x * lax.rsqrt(jnp.sum(x * x, axis=-1, keepdims=True) / width + EPS)


def _cparams(sem, vmem=VMEM_LIMIT):
    return pltpu.CompilerParams(dimension_semantics=sem, vmem_limit_bytes=vmem)


def _mod_kernel(c_ref, w_ref, b_ref, o_ref):
    c = c_ref[...]
    act = c * _sigmoid(c)
    o_ref[0] = _dot_f32(act, w_ref[0]) + b_ref[0]


def _mod_vectors(c8, w_mod, b_mod):
    depth, d, six_d = w_mod.shape
    nj = six_d // d
    return pl.pallas_call(
        _mod_kernel,
        grid=(depth, nj),
        in_specs=[
            pl.BlockSpec((8, d), lambda l, j: (0, 0)),
            pl.BlockSpec((1, d, d), lambda l, j: (l, 0, j)),
            pl.BlockSpec((1, 1, d), lambda l, j: (l, 0, j)),
        ],
        out_specs=pl.BlockSpec((1, 8, d), lambda l, j: (l, 0, j)),
        out_shape=jax.ShapeDtypeStruct((depth, 8, six_d), F32),
        compiler_params=_cparams(("arbitrary", "arbitrary")),
        name="mod_vectors",
    )(c8, w_mod, b_mod.reshape(depth, 1, six_d))


def _mod_row(m_ref, row):
    return m_ref[pl.ds(row, 1), :]


def _inproj_kernel(*refs, fuse_res, nblk):
    if fuse_res:
        x_ref, y_ref, mprev_ref, m_ref, g_ref, w_ref, xo_ref, pa_ref, pb_ref, pc_ref, pd_ref = refs
    else:
        x_ref, m_ref, g_ref, w_ref, pa_ref, pb_ref, pc_ref, pd_ref = refs
    i = pl.program_id(0)
    b = i // nblk
    row = jnp.where(i % nblk == 0, 4, b)
    d = D_MODEL
    x = x_ref[...]
    if fuse_res:
        mp = _mod_row(mprev_ref, row)
        x = x + mp[:, 5 * d:6 * d] * _load_token_rows(y_ref)
        xo_ref[...] = x
    m = _mod_row(m_ref, row)
    h = _rms(x, d) * g_ref[...] * (1.0 + m[:, d:2 * d]) + m[:, 0:d]
    p = _bdot(h, w_ref[...])
    pa_ref[...] = p[:, 0:W_A]
    pb_ref[...] = p[:, W_A:W_A + W_B].astype(pb_ref.dtype)
    pc_ref[...] = p[:, W_A + W_B:W_A + W_B + W_C]
    pd_ref[...] = p[:, W_A + W_B + W_C:]


def _inproj(x, y, m_prev, m, g, w, nblk):
    t, d = x.shape
    fuse = y is not None
    rb = ROW_BLOCK
    row_spec = lambda w_: pl.BlockSpec((rb, w_), lambda i: (i, 0))
    full = lambda a: pl.BlockSpec(a.shape, lambda i: (0,) * a.ndim)
    ins, specs = [x], [row_spec(d)]
    if fuse:
        ins += [y, m_prev]
        specs += [pl.BlockSpec((rb * TOK_TILES, LANE), lambda i: (i, 0)), full(m_prev)]
    ins += [m, g, w]
    specs += [full(m), full(g), full(w)]
    outs, ospecs = [], []
    if fuse:
        outs.append(jax.ShapeDtypeStruct((t, d), F32))
        ospecs.append(row_spec(d))
    for w_ in (W_A, W_B, W_C, W_D):
        outs.append(jax.ShapeDtypeStruct((t, w_), BF16 if w_ == W_B else F32))
        ospecs.append(row_spec(w_))
    res = pl.pallas_call(
        functools.partial(_inproj_kernel, fuse_res=fuse, nblk=nblk),
        grid=(t // rb,),
        in_specs=specs,
        out_specs=ospecs,
        out_shape=outs,
        compiler_params=_cparams(("arbitrary",)),
        name="in_projection",
    )(*ins)
    if fuse:
        return res[0], res[1:]
    return x, res


def _sub_chunk(refs, i, tri, emat, emask_t, reverse):
    q_ref, k_ref, la_ref, v_ref, o_ref, st_ref, r_ref = refs
    hv = v_ref.shape[1]
    row_id = lax.broadcasted_iota(I32, (SUB, 1), 0)
    off = pl.multiple_of(i * SUB, SUB)
    qs = q_ref[pl.ds(off, SUB), :]
    ks = k_ref[pl.ds(off, SUB), :]
    las = la_ref[pl.ds(off, SUB), :]
    vs = v_ref[pl.ds(off, SUB), :]
    cum = _dot_sel_l(tri, las)
    last = cum[0:1, :] if reverse else cum[SUB - 1:SUB, :]
    for j in range(SUB):
        valid = (row_id <= j) if reverse else (row_id >= j)
        dlt = jnp.where(valid, cum - cum[j:j + 1, :], NEG_INF)
        r_ref[j * SUB:(j + 1) * SUB, :] = (qs * ks[j:j + 1, :] * jnp.exp(dlt)).astype(BF16)
    att = jnp.dot(r_ref[...], emat, preferred_element_type=F32)
    o = jnp.zeros((SUB, hv), F32)
    for j in range(SUB):
        o = o + att[j * SUB:(j + 1) * SUB, :] * vs[j:j + 1, :]
    st = st_ref[...]
    o = o + _bdot_nt(qs * jnp.exp(cum), st)
    o_ref[pl.ds(off, SUB), :] = o
    kd = ks * jnp.exp(last - cum)
    st_ref[...] = st * jnp.exp(last) + _bdot_tn(vs, kd) * emask_t


def _macro_step(refs, m, tri, emask_t, hmk, hmv, reverse):
    q_ref, k_ref, la_ref, v_ref, o_ref, st_ref, _ = refs
    hv = v_ref.shape[1]
    n_sub = MACRO // SUB
    off = pl.multiple_of(m * MACRO, MACRO)
    q = q_ref[pl.ds(off, MACRO), :]
    k = k_ref[pl.ds(off, MACRO), :]
    v = v_ref[pl.ds(off, MACRO), :]
    cum = _dot_sel_l(tri, la_ref[pl.ds(off, MACRO), :])
    last = cum[0:1, :] if reverse else cum[MACRO - 1:MACRO, :]
    st = st_ref[...]
    o_inter = _bdot_nt(q * jnp.exp(cum), st)
    st_ref[...] = st * jnp.exp(last) + _bdot_tn(v, k * jnp.exp(last - cum)) * emask_t
    vb = v.astype(BF16)
    row_i = lax.broadcasted_iota(I32, (N_HEADS * SUB, 1), 0) % SUB
    for s in range(n_sub):
        lo = s * SUB
        if reverse:
            k_lo, k_hi = lo, MACRO
            ref = cum[lo + SUB:lo + SUB + 1, :] if s < n_sub - 1 else jnp.zeros_like(last)
        else:
            k_lo, k_hi = 0, lo + SUB
            ref = cum[lo - 1:lo, :] if s > 0 else jnp.zeros_like(last)
        qt = q[lo:lo + SUB, :] * jnp.exp(cum[lo:lo + SUB, :] - ref)
        qs = jnp.concatenate([qt * hmk[h:h + 1, :] for h in range(N_HEADS)], axis=0).astype(BF16)
        kt = (k[k_lo:k_hi, :] * jnp.exp(ref - cum[k_lo:k_hi, :])).astype(BF16)
        att = lax.dot_general(qs, kt, (((1,), (1,)), ((), ())), preferred_element_type=F32)
        col = lax.broadcasted_iota(I32, (1, k_hi - k_lo), 1) + k_lo
        valid = (col >= lo + row_i) if reverse else (col <= lo + row_i)
        att = jnp.where(valid, att, 0.0)
        o_heads = jnp.dot(att.astype(BF16), vb[k_lo:k_hi, :], preferred_element_type=F32)
        o = o_inter[lo:lo + SUB, :]
        for h in range(N_HEADS):
            o = o + o_heads[h * SUB:(h + 1) * SUB, :] * hmv[h:h + 1, :]
        o_ref[pl.ds(off + lo, SUB), :] = o


def _scan_pair(fwd_refs, bwd_refs, consts):
    tri_f, tri_b, trim_f, trim_b, emat, emask_t, hmk, hmv, sub_sum = consts
    rows = fwd_refs[0].shape[0]
    n_sub, n_macro = rows // SUB, rows // MACRO
    tot_f = jnp.dot(sub_sum, fwd_refs[2][...].astype(BF16), preferred_element_type=F32)
    tot_b = jnp.dot(sub_sum, bwd_refs[2][...].astype(BF16), preferred_element_type=F32)
    factorisable = jnp.min(jnp.minimum(tot_f, tot_b)) > -DECAY_GUARD

    @pl.when(factorisable)
    def _():
        def body(step, carry):
            _macro_step(fwd_refs, step, trim_f, emask_t, hmk, hmv, False)
            _macro_step(bwd_refs, n_macro - 1 - step, trim_b, emask_t, hmk, hmv, True)
            return carry

        lax.fori_loop(0, n_macro, body, 0, unroll=True)

    @pl.when(jnp.logical_not(factorisable))
    def _():
        def body(step, carry):
            _sub_chunk(fwd_refs, step, tri_f, emat, emask_t, False)
            _sub_chunk(bwd_refs, n_sub - 1 - step, tri_b, emat, emask_t, True)
            return carry

        lax.fori_loop(0, n_sub, body, 0)


def _scan_consts(hk, hv, dk, dv):
    tri = lambda n, low: jnp.asarray(np.tril(np.ones((n, n), np.float32)) if low else np.triu(np.ones((n, n), np.float32)), BF16)
    em = _head_match(hk, hv, dk, dv)
    heads = np.arange(N_HEADS)[:, None]
    hmk = (np.arange(hk)[None, :] // dk == heads).astype(np.float32)
    hmv = (np.arange(hv)[None, :] // dv == heads).astype(np.float32)
    sub_sum = (np.arange(ROW_BLOCK)[None, :] // SUB == np.arange(ROW_BLOCK // SUB)[:, None]).astype(np.float32)
    return [tri(SUB, True), tri(SUB, False), tri(MACRO, True), tri(MACRO, False), jnp.asarray(em, BF16),
            jnp.asarray(em.T, F32), jnp.asarray(hmk), jnp.asarray(hmv), jnp.asarray(sub_sum, BF16)]


def _hgrn_prep(q_ref, v_ref, f_ref, lb, qo, ko, lo, vo):
    qr = q_ref[...]
    qo[...] = qr * _sigmoid(qr) * (A_DH ** -0.5)
    vo[...] = v_ref[...]
    z = f_ref[...]
    lo[...] = _logaddexp(jnp.log(lb), jnp.log1p(-lb) + _log_sigmoid(z))
    ko[...] = (1.0 - lb) * _sigmoid(-z)


def _hgrn_kernel(qf_ref, vf_ref, ff_ref, qb_ref, vb_ref, fb_ref, lb_ref, *rest):
    consts, (of_ref, ob_ref, stf, stb, rf, rb_, q1, k1, l1, v1, q2, k2, l2, v2) = rest[:N_SCAN_CONSTS], rest[N_SCAN_CONSTS:]
    @pl.when(pl.program_id(1) == 0)
    def _():
        stf[...] = jnp.zeros_like(stf)
        stb[...] = jnp.zeros_like(stb)

    _hgrn_prep(qf_ref, vf_ref, ff_ref, lb_ref[0:1, :], q1, k1, l1, v1)
    _hgrn_prep(qb_ref, vb_ref, fb_ref, lb_ref[1:2, :], q2, k2, l2, v2)
    _scan_pair((q1, k1, l1, v1, of_ref, stf, rf), (q2, k2, l2, v2, ob_ref, stb, rb_), [c[...] for c in consts])


def _head_match(hk, hv, dk, dv):
    m = (np.arange(hk)[:, None] // dk == np.arange(hv)[None, :] // dv).astype(np.float32)
    return m


def _bwd_block(t, nblk):
    return jnp.where(t == 0, 0, nblk - t)


def _hgrn_scan(pa, lb, batch, nblk):
    t_all = pa.shape[0]
    rb, w = ROW_BLOCK, D_GROUP
    consts = _scan_consts(w, w, A_DH, A_DH)
    fwd = lambda col: pl.BlockSpec((rb, w), lambda b, t: (b * nblk + t, col))
    bwd = lambda col: pl.BlockSpec((rb, w), lambda b, t: (b * nblk + _bwd_block(t, nblk), col))
    full = lambda a: pl.BlockSpec(a.shape, lambda b, t: (0,) * a.ndim)
    vm = lambda shape, dt=F32: pltpu.VMEM(shape, dt)
    return pl.pallas_call(
        _hgrn_kernel,
        grid=(batch, nblk),
        in_specs=[fwd(0), fwd(1), fwd(2), bwd(0), bwd(1), bwd(3), full(lb)] + [full(c) for c in consts],
        out_specs=[fwd(0), bwd(0)],
        out_shape=[jax.ShapeDtypeStruct((t_all, w), F32)] * 2,
        scratch_shapes=[vm((w, w)), vm((w, w)), vm((SUB * SUB, w), BF16), vm((SUB * SUB, w), BF16)]
        + [vm((rb, w))] * 8,
        compiler_params=_cparams(("arbitrary", "arbitrary")),
        name="hgrn_scan",
    )(pa, pa, pa, pa, pa, pa, lb, *consts)


def _gla_prep(q_ref, k_ref, v_ref, z_ref, wg_ref, bg_ref, qo, ko, lo, vo):
    qo[...] = q_ref[...] * (C_DK ** -0.5)
    ko[...] = k_ref[...]
    vo[...] = v_ref[...]
    zl = _dot_f32(z_ref[...], wg_ref[...]) + bg_ref[...]
    lo[...] = _log_sigmoid(zl) / C_GATE_NORM


def _gla_kernel(qf_ref, kf_ref, vf_ref, zf_ref, qb_ref, kb_ref, vb_ref, zb_ref, wgf_ref, bgf_ref, wgb_ref, bgb_ref, *rest):
    consts, (of_ref, ob_ref, stf, stb, rf, rb_, q1, k1, l1, v1, q2, k2, l2, v2) = rest[:N_SCAN_CONSTS], rest[N_SCAN_CONSTS:]
    @pl.when(pl.program_id(1) == 0)
    def _():
        stf[...] = jnp.zeros_like(stf)
        stb[...] = jnp.zeros_like(stb)

    _gla_prep(qf_ref, kf_ref, vf_ref, zf_ref, wgf_ref, bgf_ref, q1, k1, l1, v1)
    _gla_prep(qb_ref, kb_ref, vb_ref, zb_ref, wgb_ref, bgb_ref, q2, k2, l2, v2)
    _scan_pair((q1, k1, l1, v1, of_ref, stf, rf), (q2, k2, l2, v2, ob_ref, stb, rb_), [c[...] for c in consts])


def _gla_scan(pc, wg_f, bg_f, wg_b, bg_b, batch, nblk):
    t_all = pc.shape[0]
    rb = ROW_BLOCK
    hk, hv = N_HEADS * C_DK, N_HEADS * C_DV
    consts = _scan_consts(hk, hv, C_DK, C_DV)
    wgf = jnp.zeros((LANE, hk), F32).at[0:C_GATE_RANK].set(wg_f)
    wgb = jnp.zeros((LANE, hk), F32).at[C_GATE_RANK:2 * C_GATE_RANK].set(wg_b)
    bgf, bgb = bg_f.reshape(1, hk), bg_b.reshape(1, hk)
    fwd = lambda w, col: pl.BlockSpec((rb, w), lambda b, t: (b * nblk + t, col))
    bwd = lambda w, col: pl.BlockSpec((rb, w), lambda b, t: (b * nblk + _bwd_block(t, nblk), col))
    full = lambda a: pl.BlockSpec(a.shape, lambda b, t: (0,) * a.ndim)
    vm = lambda shape, dt=F32: pltpu.VMEM(shape, dt)
    return pl.pallas_call(
        _gla_kernel,
        grid=(batch, nblk),
        in_specs=[fwd(hk, 0), fwd(hk, 1), fwd(hv, 1), fwd(LANE, 6), bwd(hk, 0), bwd(hk, 1), bwd(hv, 1), bwd(LANE, 6),
                  full(wgf), full(bgf), full(wgb), full(bgb)] + [full(c) for c in consts],
        out_specs=[fwd(hv, 0), bwd(hv, 0)],
        out_shape=[jax.ShapeDtypeStruct((t_all, hv), F32)] * 2,
        scratch_shapes=[vm((hv, hk)), vm((hv, hk)), vm((SUB * SUB, hk), BF16), vm((SUB * SUB, hk), BF16)]
        + [vm((rb, hk)), vm((rb, hk)), vm((rb, hk)), vm((rb, hv))] * 2,
        compiler_params=_cparams(("arbitrary", "arbitrary")),
        name="gla_scan",
    )(pc, pc, pc, pc, pc, pc, pc, pc, wgf, bgf, wgb, bgb, *consts)


NA_QROWS = ROW_BLOCK // GRID_W
NA_WROWS = WIN_H + NA_QROWS


def _na_kernel(q_ref, k_ref, v_ref, bias_ref, hm_ref, o_ref, *, j0, rows, ctx):
    j = pl.program_id(1) + j0
    q = q_ref[...] * (B_DH ** -0.5)
    hm = hm_ref[...]
    kc = k_ref[0:ctx, :]
    vc = v_ref[0:ctx, :]
    nt = lambda a, b: lax.dot_general(a, b, (((1,), (1,)), ((), ())), preferred_element_type=F32)

    @pl.when(j == 0)
    def _():
        acc = jnp.zeros(q.shape, F32)
        for h in range(N_HEADS):
            mh = hm[h:h + 1, :]
            s = nt(q * mh.astype(BF16), kc)
            p = jnp.exp(s - jnp.max(s, axis=-1, keepdims=True))
            inv = 1.0 / jnp.sum(p, axis=-1, keepdims=True)
            acc = acc + jnp.dot(p.astype(BF16), vc, preferred_element_type=F32) * (mh * inv)
        o_ref[...] = acc.astype(o_ref.dtype)

    @pl.when(j > 0)
    def _():
        r0 = (j - 1) * NA_QROWS
        start = jnp.clip(r0 - WIN_H // 2, 0, rows - NA_WROWS)
        off = pl.multiple_of(ctx + start * GRID_W, GRID_W)
        kw = k_ref[pl.ds(off, NA_WROWS * GRID_W), :]
        vw = v_ref[pl.ds(off, NA_WROWS * GRID_W), :]
        acc = jnp.zeros(q.shape, F32)
        for h in range(N_HEADS):
            mh = hm[h:h + 1, :]
            qh = q * mh.astype(BF16)
            sw = nt(qh, kw) + bias_ref[0, h]
            sc = nt(qh, kc)
            m = jnp.maximum(jnp.max(sw, axis=-1, keepdims=True), jnp.max(sc, axis=-1, keepdims=True))
            pw = jnp.exp(sw - m)
            pc_ = jnp.exp(sc - m)
            inv = 1.0 / (jnp.sum(pw, axis=-1, keepdims=True) + jnp.sum(pc_, axis=-1, keepdims=True))
            o = (jnp.dot(pw.astype(BF16), vw, preferred_element_type=F32)
                 + jnp.dot(pc_.astype(BF16), vc, preferred_element_type=F32))
            acc = acc + o * (mh * inv)
        o_ref[...] = acc.astype(o_ref.dtype)


def _na_bias_table(rpb, rows):
    kh = WIN_H
    cidx = np.arange(GRID_W)
    c_start = np.clip(cidx - WIN_W // 2, 0, GRID_W - WIN_W)
    col_in = (cidx[None] >= c_start[:, None]) & (cidx[None] < c_start[:, None] + WIN_W)
    dc = np.clip(cidx[None] - cidx[:, None], -(WIN_W - 1), WIN_W - 1) + (WIN_W - 1)
    sel = (dc[None] == np.arange(2 * WIN_W - 1)[:, None, None]).astype(np.float32)
    by_col = jnp.einsum("hrc,cqw->hrqw", rpb.astype(F32), jnp.asarray(sel), precision=lax.Precision.HIGHEST)
    by_col = jnp.where(jnp.asarray(col_in)[None, None], by_col, NEG_INF)
    blocked = jnp.full((N_HEADS, GRID_W, GRID_W), NEG_INF, F32)
    tables = []
    for r0 in (0, NA_QROWS, rows - NA_QROWS):
        ws = int(np.clip(r0 - kh // 2, 0, rows - NA_WROWS))
        per_q = []
        for a in range(NA_QROWS):
            r = r0 + a
            s = int(np.clip(r - kh // 2, 0, rows - kh))
            per_w = [by_col[:, ws + jj - r + kh - 1] if s <= ws + jj < s + kh else blocked for jj in range(NA_WROWS)]
            per_q.append(jnp.stack(per_w, axis=2))
        tables.append(jnp.stack(per_q, axis=1))
    return jnp.stack(tables, axis=0).reshape(3, N_HEADS, NA_QROWS * GRID_W, NA_WROWS * GRID_W)


def _head_masks(width, dh):
    return jnp.asarray((np.arange(width)[None, :] // dh == np.arange(N_HEADS)[:, None]).astype(np.float32))


def _neighbourhood_attention(pb, rpb, batch, s_len, ctx, keep_ctx):
    rows = (s_len - ctx) // GRID_W
    assert rows >= NA_WROWS and rows % NA_QROWS == 0 and ctx == ROW_BLOCK
    rb = ROW_BLOCK
    j0 = 0 if keep_ctx else 1
    per_b = s_len // rb
    bias = _na_bias_table(rpb, rows)
    hm = _head_masks(D_GROUP, B_DH)

    def bias_idx(b, jj):
        r0 = (jj + j0 - 1) * NA_QROWS
        return (jnp.where(r0 <= 0, 0, jnp.where(r0 == rows - NA_QROWS, 2, 1)), 0, 0, 0)

    return pl.pallas_call(
        functools.partial(_na_kernel, j0=j0, rows=rows, ctx=ctx),
        grid=(batch, per_b - j0),
        in_specs=[
            pl.BlockSpec((rb, D_GROUP), lambda b, jj: (b * per_b + jj + j0, 0)),
            pl.BlockSpec((s_len, D_GROUP), lambda b, jj: (b, 1)),
            pl.BlockSpec((s_len, D_GROUP), lambda b, jj: (b, 2)),
            pl.BlockSpec((1, N_HEADS, rb, NA_WROWS * GRID_W), bias_idx),
            pl.BlockSpec(hm.shape, lambda b, jj: (0, 0)),
        ],
        out_specs=pl.BlockSpec((rb, D_GROUP), lambda b, jj: (b * (per_b - j0) + jj, 0)),
        out_shape=jax.ShapeDtypeStruct((batch * (per_b - j0) * rb, D_GROUP), BF16),
        compiler_params=_cparams(("arbitrary", "arbitrary")),
        name="neighbourhood_attention",
    )(pb, pb, pb, bias, hm)


def _mla_prep_kernel(pd_ref, gq_ref, gkv_ref, wq1_ref, wq2_ref, wkv_ref, cq_ref, sq_ref, tk_ref, place_ref,
                     q_ref, k_ref, v_ref):
    pd = pd_ref[...]
    cq = pd[:, 0:256]
    ckv = pd[:, 256:384]
    kr = pd[:, 384:512]
    qn = _rms(cq, D_Q_RANK) * gq_ref[...]
    q = _bdot(qn, wq1_ref[...]) * cq_ref[...] + _bdot(qn, wq2_ref[...]) * sq_ref[...]
    q_ref[...] = q.astype(BF16)
    kvn = _rms(ckv, D_KV_RANK) * gkv_ref[...]
    kv = _bdot(kvn, wkv_ref[...])
    hw = N_HEADS * HEAD_TILE
    k = kv[:, 0:hw] + _dot_sel_r(kr * tk_ref[...], place_ref[...])
    k_ref[...] = k.astype(BF16)
    v_ref[...] = kv[:, hw:].astype(BF16)


def _rope_swap_perm():
    f = ROPE_FREQS
    return np.concatenate([np.arange(f, 2 * f), np.arange(0, f), np.arange(3 * f, 4 * f), np.arange(2 * f, 3 * f)])


def _mla_tables(n, ctx):
    t = np.arange(n)
    inv_freq = ROPE_BASE ** (-np.arange(ROPE_FREQS, dtype=np.float32) / ROPE_FREQS)
    ang_r = (t // GRID_W).astype(np.float32)[:, None] * inv_freq
    ang_c = (t % GRID_W).astype(np.float32)[:, None] * inv_freq
    cos32 = np.concatenate([np.cos(ang_r), np.cos(ang_r), np.cos(ang_c), np.cos(ang_c)], axis=1)
    sin32 = np.concatenate([-np.sin(ang_r), np.sin(ang_r), -np.sin(ang_c), np.sin(ang_c)], axis=1)
    cos32 = np.concatenate([np.ones((ctx, D_ROPE), np.float32), cos32.astype(np.float32)], axis=0)
    sin32 = np.concatenate([np.zeros((ctx, D_ROPE), np.float32), sin32.astype(np.float32)], axis=0)
    s_len = n + ctx
    cq = np.zeros((s_len, N_HEADS, HEAD_TILE), np.float32)
    sq = np.zeros((s_len, N_HEADS, HEAD_TILE), np.float32)
    cq[:, :, 0:D_NOPE] = MLA_SCALE
    cq[:, :, D_NOPE:D_NOPE + D_ROPE] = cos32[:, None, :] * MLA_SCALE
    sq[:, :, D_NOPE:D_NOPE + D_ROPE] = sin32[:, None, :] * MLA_SCALE
    tk = np.zeros((s_len, LANE), np.float32)
    tk[:, 0:D_ROPE] = cos32
    tk[:, D_ROPE:2 * D_ROPE] = sin32
    place = np.zeros((LANE, N_HEADS * HEAD_TILE), np.float32)
    for h in range(N_HEADS):
        for l in range(D_ROPE):
            place[l, h * HEAD_TILE + D_NOPE + l] = 1.0
            place[D_ROPE + l, h * HEAD_TILE + D_NOPE + l] = 1.0
    return (jnp.asarray(cq.reshape(s_len, -1)), jnp.asarray(sq.reshape(s_len, -1)), jnp.asarray(tk),
            jnp.asarray(place, BF16))


def _mla_weights(q_norm_g, w_uq, kv_norm_g, w_ukv):
    perm = _rope_swap_perm()
    wq = w_uq.reshape(D_Q_RANK, N_HEADS, D_NOPE + D_ROPE)
    wq1 = jnp.zeros((256, N_HEADS, HEAD_TILE), F32).at[0:D_Q_RANK, :, 0:D_NOPE + D_ROPE].set(wq)
    wq2 = jnp.zeros((256, N_HEADS, HEAD_TILE), F32).at[0:D_Q_RANK, :, D_NOPE:D_NOPE + D_ROPE].set(
        wq[:, :, D_NOPE:][:, :, perm])
    wkv = w_ukv.reshape(D_KV_RANK, N_HEADS, D_NOPE + D_V)
    wk = jnp.zeros((D_KV_RANK, N_HEADS, HEAD_TILE), F32).at[:, :, 0:D_NOPE].set(wkv[:, :, 0:D_NOPE])
    wv = jnp.zeros((D_KV_RANK, N_HEADS, HEAD_TILE), F32).at[:, :, 0:D_V].set(wkv[:, :, D_NOPE:])
    hw = N_HEADS * HEAD_TILE
    wkv_p = jnp.concatenate([wk.reshape(D_KV_RANK, hw), wv.reshape(D_KV_RANK, hw)], axis=1)
    gq = jnp.zeros((1, 256), F32).at[0, 0:D_Q_RANK].set(q_norm_g)
    return (gq, kv_norm_g.reshape(1, D_KV_RANK), wq1.reshape(256, hw).astype(BF16), wq2.reshape(256, hw).astype(BF16),
            wkv_p.astype(BF16))


def _mla_prep(pd, weights, tables, nblk):
    t_all = pd.shape[0]
    rb = ROW_BLOCK
    gq, gkv, wq1, wq2, wkv = weights
    cq, sq, tk, place = tables
    hw = N_HEADS * HEAD_TILE
    row = lambda w: pl.BlockSpec((rb, w), lambda i: (i, 0))
    pos = lambda w: pl.BlockSpec((rb, w), lambda i: (i % nblk, 0))
    full = lambda a: pl.BlockSpec(a.shape, lambda i: (0,) * a.ndim)
    return pl.pallas_call(
        _mla_prep_kernel,
        grid=(t_all // rb,),
        in_specs=[row(W_D), full(gq), full(gkv), full(wq1), full(wq2), full(wkv), pos(hw), pos(hw), pos(LANE), full(place)],
        out_specs=[row(hw)] * 3,
        out_shape=[jax.ShapeDtypeStruct((t_all, hw), BF16)] * 3,
        compiler_params=_cparams(("arbitrary",)),
        name="mla_prep",
    )(pd, gq, gkv, wq1, wq2, wkv, cq, sq, tk, place)


def _mla_attn_kernel(q_ref, k_ref, v_ref, o_ref, *, j0, ctx):
    j = pl.program_id(2) + j0
    q = q_ref[...]

    def attend(k, v):
        s = lax.dot_general(q, k, (((1,), (1,)), ((), ())), preferred_element_type=F32)
        p = jnp.exp(s - jnp.max(s, axis=-1, keepdims=True))
        inv = 1.0 / jnp.sum(p, axis=-1, keepdims=True)
        return (jnp.dot(p.astype(BF16), v, preferred_element_type=F32) * inv).astype(o_ref.dtype)

    @pl.when(j == 0)
    def _():
        o_ref[...] = attend(k_ref[0:ctx, :], v_ref[0:ctx, :])

    @pl.when(j > 0)
    def _():
        o_ref[...] = attend(k_ref[...], v_ref[...])


def _mla_attention(q, k, v, batch, s_len, ctx, keep_ctx):
    t_all = q.shape[0]
    rb = ROW_BLOCK
    nblk = s_len // rb
    j0 = 0 if keep_ctx else 1
    return pl.pallas_call(
        functools.partial(_mla_attn_kernel, j0=j0, ctx=ctx),
        grid=(batch, N_HEADS, nblk - j0),
        in_specs=[
            pl.BlockSpec((rb, HEAD_TILE), lambda b, h, jj: (b * nblk + jj + j0, h)),
            pl.BlockSpec((s_len, HEAD_TILE), lambda b, h, jj: (b, h)),
            pl.BlockSpec((s_len, HEAD_TILE), lambda b, h, jj: (b, h)),
        ],
        out_specs=pl.BlockSpec((rb, HEAD_TILE), lambda b, h, jj: (b * (nblk - j0) + jj, h)),
        out_shape=jax.ShapeDtypeStruct((batch * (nblk - j0) * rb, N_HEADS * HEAD_TILE), BF16),
        compiler_params=_cparams(("arbitrary", "arbitrary", "arbitrary")),
        name="mla_attention",
    )(q, k, v)


def _outproj_kernel(oaf_ref, oab_ref, ga_ref, ob_ref, ogf_ref, ogb_ref, gc_ref, od_ref, x_ref, m_ref, gha_ref, ghc_ref,
                    g2_ref, wa_ref, wb_ref, wc_ref, wd_ref, em_ref, x1_ref, h2_ref, *, j0, nblk_out):
    i = pl.program_id(0)
    b = i // nblk_out
    row = jnp.where(i % nblk_out + j0 == 0, 4, b)
    d = D_MODEL
    em = em_ref[...]

    def readout(o, g_norm, gate, dh):
        ms = _dot_sel_r(o * o, em) / dh
        return o * lax.rsqrt(ms + EPS) * g_norm * (gate * _sigmoid(gate))

    a = readout(oaf_ref[...] + oab_ref[...], gha_ref[...], ga_ref[...], A_DH)
    c = readout(ogf_ref[...] + ogb_ref[...], ghc_ref[...], gc_ref[...], C_DV)
    mix = (_bdot(a, wa_ref[...]) + jnp.dot(ob_ref[...], wb_ref[...], preferred_element_type=F32)
           + _bdot(c, wc_ref[...]) + jnp.dot(od_ref[...], wd_ref[...], preferred_element_type=F32))
    m = _mod_row(m_ref, row)
    x1 = x_ref[...] + m[:, 2 * d:3 * d] * mix
    x1_ref[...] = x1
    _store_token_rows(h2_ref, _rms(x1, d) * g2_ref[...] * (1.0 + m[:, 4 * d:5 * d]) + m[:, 3 * d:4 * d])


def _outproj(oaf, oab, pa, ob, ogf, ogb, pc, od, x, m, gha, ghc, g2, w_out, batch, nblk, keep_ctx):
    rb, d = ROW_BLOCK, D_MODEL
    j0 = 0 if keep_ctx else 1
    nblk_out = nblk - j0
    t_out = batch * nblk_out * rb
    wa = w_out[0:256].astype(BF16)
    wb = w_out[256:512].astype(BF16)
    wc = w_out[512:768].astype(BF16)
    wd = jnp.zeros((N_HEADS, HEAD_TILE, d), F32).at[:, 0:D_V, :].set(w_out[768:1024].reshape(N_HEADS, D_V, d))
    wd = wd.reshape(N_HEADS * HEAD_TILE, d).astype(BF16)
    em = jnp.asarray(_head_match(D_GROUP, D_GROUP, 64, 64), BF16)
    src = lambda i: (i // nblk_out) * nblk + i % nblk_out + j0
    row_in = lambda w, col=0: pl.BlockSpec((rb, w), lambda i: (src(i), col))
    row_out = lambda w: pl.BlockSpec((rb, w), lambda i: (i, 0))
    full = lambda a_: pl.BlockSpec(a_.shape, lambda i: (0,) * a_.ndim)
    return pl.pallas_call(
        functools.partial(_outproj_kernel, j0=j0, nblk_out=nblk_out),
        grid=(t_out // rb,),
        in_specs=[row_in(256), row_in(256), row_in(256, 4), row_out(256), row_in(256), row_in(256), row_in(256, 2),
                  row_out(N_HEADS * HEAD_TILE), row_in(d), full(m), full(gha), full(ghc), full(g2),
                  full(wa), full(wb), full(wc), full(wd), full(em)],
        out_specs=[row_out(d), pl.BlockSpec((rb * TOK_TILES, LANE), lambda i: (i, 0))],
        out_shape=[jax.ShapeDtypeStruct((t_out, d), F32), jax.ShapeDtypeStruct((t_out * TOK_TILES, LANE), F32)],
        compiler_params=_cparams(("arbitrary",)),
        name="out_projection",
    )(oaf, oab, pa, ob, ogf, ogb, pc, od, x, m, gha, ghc, g2, wa, wb, wc, wd, em)


def _load_token_rows(ref):
    rows = ref.shape[0] // TOK_TILES
    return jnp.concatenate([ref[pl.ds(k, rows, stride=TOK_TILES), :] for k in range(TOK_TILES)], axis=1)


def _store_token_rows(ref, val):
    rows = val.shape[0]
    for k in range(TOK_TILES):
        ref[pl.ds(k, rows, stride=TOK_TILES), :] = val[:, k * LANE:(k + 1) * LANE]


def _router_logits(h, wr_ref, br_ref):
    lg = _dot_f32(h, wr_ref[...]) + br_ref[...]
    lane = lax.broadcasted_iota(I32, lg.shape, 1).astype(F32)
    return lg, lane


def _top_group(lg, lane):
    gl = jnp.where(lane < N_GROUPS, lg, NEG_INF)
    gmax = jnp.max(gl, axis=-1, keepdims=True)
    gsel = jnp.min(jnp.where(gl == gmax, lane, float(LANE)), axis=-1, keepdims=True)
    p_group = 1.0 / jnp.sum(jnp.exp(gl - gmax), axis=-1, keepdims=True)
    return gsel, p_group


def _expert_gates(lg, lane, lo, p_group):
    big = float(LANE)
    el = jnp.where((lane >= lo) & (lane < lo + EXPERTS_PER_GROUP), lg, NEG_INF)
    m1 = jnp.max(el, axis=-1, keepdims=True)
    i1 = jnp.min(jnp.where(el == m1, lane, big), axis=-1, keepdims=True)
    el2 = jnp.where(lane == i1, NEG_INF, el)
    m2 = jnp.max(el2, axis=-1, keepdims=True)
    i2 = jnp.min(jnp.where(el2 == m2, lane, big), axis=-1, keepdims=True)
    t = jnp.exp(m2 - m1)
    w1 = p_group / (1.0 + t)
    w2 = p_group * t / (1.0 + t)
    return jnp.where(lane == i1 - lo, w1, jnp.where(lane == i2 - lo, w2, 0.0))


def _router_kernel(h_ref, wr_ref, br_ref, tril_ref, meta_ref, cnt_ref, carry):
    @pl.when(pl.program_id(0) == 0)
    def _():
        carry[...] = jnp.zeros_like(carry)

    lg, lane = _router_logits(_load_token_rows(h_ref), wr_ref, br_ref)
    gsel, _ = _top_group(lg, lane)
    onehot = jnp.where(lane == gsel, 1.0, 0.0)
    incl = jnp.dot(tril_ref[...], onehot.astype(BF16), preferred_element_type=F32)
    rank = jnp.sum(onehot * (incl - 1.0 + carry[...]), axis=-1, keepdims=True)
    carry[...] = carry[...] + jnp.sum(onehot, axis=0, keepdims=True)
    meta_ref[...] = jnp.where(lane == 0, gsel, jnp.where(lane == 1, rank, 0.0)).astype(I32)
    cnt_ref[...] = carry[...].astype(I32)


def _router_weights(w_rg, b_rg, w_re, b_re):
    d = w_rg.shape[0]
    ne = N_GROUPS * EXPERTS_PER_GROUP
    wr = jnp.zeros((d, LANE), F32).at[:, 0:N_GROUPS].set(w_rg).at[:, N_GROUPS:N_GROUPS + ne].set(w_re)
    br = jnp.zeros((1, LANE), F32).at[0, 0:N_GROUPS].set(b_rg).at[0, N_GROUPS:N_GROUPS + ne].set(b_re)
    return wr, br


def _router(h2t, wr, br):
    t = h2t.shape[0] // TOK_TILES
    rb = ROW_BLOCK
    tril = jnp.asarray(np.tril(np.ones((rb, rb), np.float32)), BF16)
    full = lambda a: pl.BlockSpec(a.shape, lambda i: (0,) * a.ndim)
    return pl.pallas_call(
        _router_kernel,
        grid=(t // rb,),
        in_specs=[pl.BlockSpec((rb * TOK_TILES, LANE), lambda i: (i, 0)), full(wr), full(br), full(tril)],
        out_specs=[pl.BlockSpec((rb, LANE), lambda i: (i, 0)), pl.BlockSpec((1, LANE), lambda i: (0, 0))],
        out_shape=[jax.ShapeDtypeStruct((t, LANE), I32), jax.ShapeDtypeStruct((1, LANE), I32)],
        scratch_shapes=[pltpu.VMEM((1, LANE), F32)],
        compiler_params=_cparams(("arbitrary",)),
        name="moe_router",
    )(h2t, wr, br, tril)


def _invert_kernel(dest_ref, inv_ref):
    def clear(s, c):
        inv_ref[s] = -1
        return c

    lax.fori_loop(0, inv_ref.shape[0], clear, 0, unroll=8)

    def put(t, c):
        inv_ref[dest_ref[t]] = t
        return c

    lax.fori_loop(0, dest_ref.shape[0], put, 0, unroll=8)


def _invert(dest, n_slots):
    smem = pl.BlockSpec(memory_space=pltpu.SMEM)
    return pl.pallas_call(
        _invert_kernel, in_specs=[smem], out_specs=smem,
        out_shape=jax.ShapeDtypeStruct((n_slots,), I32), name="moe_invert",
    )(dest)


def _token_copy(src_ref, dst_ref, s, d, sem):
    s8 = pl.multiple_of(s * TOK_TILES, TOK_TILES)
    d8 = pl.multiple_of(d * TOK_TILES, TOK_TILES)
    return pltpu.make_async_copy(src_ref.at[pl.ds(s8, TOK_TILES), :], dst_ref.at[pl.ds(d8, TOK_TILES), :], sem)


def _experts_kernel(bg_ref, nr_ref, inv_ref, h_ref, wr_ref, br_ref, wgu_ref, wdn_ref, y_ref, xbuf, ybuf, gsem, ssem):
    i = pl.program_id(0)
    n_steps = pl.num_programs(0)
    rb, d = ROW_BLOCK, D_MODEL
    slot = i % 2

    def gather(blk, sl, start):
        def one(r, c):
            tok = jnp.maximum(inv_ref[blk * rb + r], 0)
            cp = _token_copy(h_ref, xbuf.at[sl], tok, r, gsem.at[sl])
            cp.start() if start else cp.wait()
            return c
        lax.fori_loop(0, rb, one, 0, unroll=8)

    def scatter(blk, sl, start):
        def one(r, c):
            cp = _token_copy(ybuf.at[sl], y_ref, r, inv_ref[blk * rb + r], ssem.at[sl])
            cp.start() if start else cp.wait()
            return c
        lax.fori_loop(0, nr_ref[blk], one, 0)

    @pl.when((i == 0) & (nr_ref[0] > 0))
    def _():
        gather(0, 0, True)

    @pl.when(i + 1 < n_steps)
    def _():
        @pl.when(nr_ref[i + 1] > 0)
        def _():
            gather(i + 1, 1 - slot, True)

    @pl.when(i >= 2)
    def _():
        scatter(i - 2, slot, False)

    @pl.when(nr_ref[i] > 0)
    def _():
        gather(i, slot, False)
        xf = _load_token_rows(xbuf.at[slot])
        lg, lane = _router_logits(xf, wr_ref, br_ref)
        _, p_group = _top_group(lg, lane)
        lo = (N_GROUPS + bg_ref[i] * EXPERTS_PER_GROUP).astype(F32)
        gates = _expert_gates(lg, lane, lo, p_group)
        x = xf.astype(BF16)
        acc = jnp.zeros((rb, d), F32)
        for e in range(EXPERTS_PER_GROUP):
            gu = jnp.dot(x, wgu_ref[0, e], preferred_element_type=F32)
            g = gu[:, 0:D_EXPERT]
            act = g * _sigmoid(g) * gu[:, D_EXPERT:]
            y = jnp.dot(act.astype(BF16), wdn_ref[0, e], preferred_element_type=F32)
            acc = acc + gates[:, e:e + 1] * y
        _store_token_rows(ybuf.at[slot], acc)
        scatter(i, slot, True)

    @pl.when(i == n_steps - 1)
    def _():
        @pl.when(i >= 1)
        def _():
            scatter(i - 1, 1 - slot, False)
        scatter(i, slot, False)


def _experts(block_group, block_rows, inv, h2t, wr, br, w_gu, w_dn):
    n_blocks = block_group.shape[0]
    rb, d = ROW_BLOCK, D_MODEL
    wgu = w_gu.reshape(N_GROUPS, EXPERTS_PER_GROUP, d, 2 * D_EXPERT)
    wdn = w_dn.reshape(N_GROUPS, EXPERTS_PER_GROUP, D_EXPERT, d)
    any_spec = pl.BlockSpec(memory_space=pl.ANY)
    return pl.pallas_call(
        _experts_kernel,
        grid_spec=pltpu.PrefetchScalarGridSpec(
            num_scalar_prefetch=3, grid=(n_blocks,),
            in_specs=[
                any_spec,
                pl.BlockSpec(wr.shape, lambda i, bg, nr, inv_: (0, 0)),
                pl.BlockSpec(br.shape, lambda i, bg, nr, inv_: (0, 0)),
                pl.BlockSpec((1, EXPERTS_PER_GROUP, d, 2 * D_EXPERT), lambda i, bg, nr, inv_: (bg[i], 0, 0, 0)),
                pl.BlockSpec((1, EXPERTS_PER_GROUP, D_EXPERT, d), lambda i, bg, nr, inv_: (bg[i], 0, 0, 0)),
            ],
            out_specs=any_spec,
            scratch_shapes=[pltpu.VMEM((2, rb * TOK_TILES, LANE), F32), pltpu.VMEM((2, rb * TOK_TILES, LANE), F32),
                            pltpu.SemaphoreType.DMA((2,)), pltpu.SemaphoreType.DMA((2,))]),
        out_shape=jax.ShapeDtypeStruct(h2t.shape, F32),
        compiler_params=pltpu.CompilerParams(dimension_semantics=("arbitrary",), vmem_limit_bytes=VMEM_LIMIT,
                                             has_side_effects=True),
        name="moe_experts",
    )(block_group, block_rows, inv, h2t, wr, br, wgu, wdn)


def _moe(h2t, w_rg, b_rg, w_re, b_re, w_gu_bf16, w_dn_bf16):
    t = h2t.shape[0] // TOK_TILES
    rb = ROW_BLOCK
    wr, br = _router_weights(w_rg, b_rg, w_re, b_re)
    meta, counts = _router(h2t, wr, br)
    group, rank = meta[:, 0], meta[:, 1]
    cnt = counts[0, 0:N_GROUPS]
    padded = (cnt + rb - 1) // rb * rb
    seg_end = jnp.cumsum(padded)
    seg_start = seg_end - padded
    dest = seg_start[group] + rank
    n_blocks = t // rb + N_GROUPS
    block_start = jnp.arange(n_blocks, dtype=I32) * rb
    block_group = jnp.minimum(jnp.sum((block_start[:, None] >= seg_end[None, :]).astype(I32), axis=1), N_GROUPS - 1)
    block_rows = jnp.clip((seg_start + cnt)[block_group] - block_start, 0, rb).astype(I32)
    inv = _invert(dest, n_blocks * rb)
    return _experts(block_group, block_rows, inv, h2t, wr, br, w_gu_bf16, w_dn_bf16)


def _final_kernel(x_ref, y_ref, m_ref, g_ref, o_ref, *, nblk):
    b = pl.program_id(0) // nblk
    d = D_MODEL
    m = _mod_row(m_ref, b)
    x = x_ref[...] + m[:, 5 * d:6 * d] * _load_token_rows(y_ref)
    o_ref[...] = _rms(x, d) * g_ref[...]


def _final(x1, y, m, g, nblk):
    t, d = x1.shape
    rb = ROW_BLOCK
    row = pl.BlockSpec((rb, d), lambda i: (i, 0))
    tok = pl.BlockSpec((rb * TOK_TILES, LANE), lambda i: (i, 0))
    full = lambda a: pl.BlockSpec(a.shape, lambda i: (0,) * a.ndim)
    return pl.pallas_call(
        functools.partial(_final_kernel, nblk=nblk),
        grid=(t // rb,),
        in_specs=[row, tok, full(m), full(g)],
        out_specs=row,
        out_shape=jax.ShapeDtypeStruct((t, d), F32),
        compiler_params=_cparams(("arbitrary",)),
        name="final_norm",
    )(x1, y, m, g)


def _inproj_weight(w_in):
    d = w_in.shape[0]
    z = lambda n: jnp.zeros((d, n), w_in.dtype)
    perm = _rope_swap_perm()
    kr = w_in[:, 3168:3200]
    cols = [w_in[:, 0:2048], w_in[:, 2048:2848], z(W_C - 800),
            w_in[:, 2848:3040], z(256 - D_Q_RANK), w_in[:, 3040:3168], kr, kr[:, perm], z(LANE - 2 * D_ROPE)]
    return jnp.concatenate(cols, axis=1).astype(BF16)


def _hgrn_lower_bounds(logits):
    cum = jnp.cumsum(jax.nn.softmax(logits.astype(F32), axis=0), axis=0)
    return cum - cum[0]


def kernel(x, c, ctx, c_ctx, w_mod, b_mod, norm1_g, norm2_g, w_in, w_out, hgrn_lb_logits, hgrn_norm_g, na_rpb, gla_wg_f, gla_bg_f, gla_wg_b, gla_bg_b, gla_norm_g, mla_q_norm_g, mla_w_uq, mla_kv_norm_g, mla_w_ukv, moe_w_rg, moe_b_rg, moe_w_re, moe_b_re, moe_w_gu, moe_w_dn, final_norm_g):
    batch, n, d = x.shape
    l_ctx = ctx.shape[1]
    assert d == D_MODEL and l_ctx == ROW_BLOCK and n % ROW_BLOCK == 0 and batch <= 4
    s_len = l_ctx + n
    nblk = s_len // ROW_BLOCK
    depth = w_mod.shape[0]

    c8 = jnp.zeros((8, d), F32).at[0:batch].set(c).at[4].set(c_ctx)
    mods = _mod_vectors(c8, w_mod, b_mod)
    lower_bounds = _hgrn_lower_bounds(hgrn_lb_logits)
    tables = _mla_tables(n, l_ctx)

    xa = jnp.concatenate([ctx, x], axis=1).reshape(batch * s_len, d)
    y_prev = None
    for layer in range(depth):
        keep_ctx = layer < depth - 1
        m = mods[layer]
        xa, (pa, pb, pc, pd) = _inproj(xa, y_prev, mods[layer - 1] if layer else None, m,
                                       norm1_g[layer].reshape(1, d), _inproj_weight(w_in[layer]), nblk)
        oaf, oab = _hgrn_scan(pa, lower_bounds[layer], batch, nblk)
        ob = _neighbourhood_attention(pb, na_rpb[layer], batch, s_len, l_ctx, keep_ctx)
        ogf, ogb = _gla_scan(pc, gla_wg_f[layer], gla_bg_f[layer], gla_wg_b[layer], gla_bg_b[layer], batch, nblk)
        mla_w = _mla_weights(mla_q_norm_g[layer], mla_w_uq[layer], mla_kv_norm_g[layer], mla_w_ukv[layer])
        q, k, v = _mla_prep(pd, mla_w, tables, nblk)
        od = _mla_attention(q, k, v, batch, s_len, l_ctx, keep_ctx)
        xa, h2 = _outproj(oaf, oab, pa, ob, ogf, ogb, pc, od, xa, m, hgrn_norm_g[layer].reshape(1, -1),
                          gla_norm_g[layer].reshape(1, -1), norm2_g[layer].reshape(1, d), w_out[layer],
                          batch, nblk, keep_ctx)
        y_prev = _moe(h2, moe_w_rg[layer], moe_b_rg[layer], moe_w_re[layer], moe_b_re[layer],
                      moe_w_gu[layer].astype(BF16), moe_w_dn[layer].astype(BF16))
    out = _final(xa, y_prev, mods[depth - 1], final_norm_g.reshape(1, d), n // ROW_BLOCK)
    return out.reshape(batch, n, d)
```

```python
import functools

import numpy as np
import jax
import jax.numpy as jnp
from jax import lax
from jax.experimental import pallas as pl
from jax.experimental.pallas import tpu as pltpu

F32 = jnp.float32
BF16 = jnp.bfloat16
I32 = jnp.int32

D_MODEL = 1024
DEPTH = 2
GRID_W = 64
EPS = 1e-6
D_GROUP = 256
N_HEADS = 4
A_DH = 64
B_DH = 64
WIN_H = 8
WIN_W = 16
C_DK = 32
C_DV = 64
C_GATE_RANK = 16
C_GATE_NORM = 16.0
D_NOPE = 64
D_V = 64
D_ROPE = 32
ROPE_FREQS = 8
ROPE_BASE = 10000.0
D_Q_RANK = 192
D_KV_RANK = 128
MLA_SCALE = (D_NOPE + D_ROPE) ** -0.5
N_GROUPS = 4
EXPERTS_PER_GROUP = 8
D_EXPERT = 256

ROW_BLOCK = 256
SUB = 16
MACRO = 64
DECAY_GUARD = 60.0
N_SCAN_CONSTS = 9
LANE = 128
HEAD_TILE = 128
VT_ROWS = 80
W_A, W_B, W_C, W_D = 1280, 768, 896, 512
TOK_TILES = D_MODEL // LANE
VMEM_LIMIT = 52 * 1024 * 1024
NEG_INF = float("-inf")


def _bdot(a, b):
    return jnp.dot(a.astype(BF16), b.astype(BF16), preferred_element_type=F32)


def _bdot_nt(a, b):
    return lax.dot_general(a.astype(BF16), b.astype(BF16), (((1,), (1,)), ((), ())), preferred_element_type=F32)


def _bdot_tn(a, b):
    return lax.dot_general(a.astype(BF16), b.astype(BF16), (((0,), (0,)), ((), ())), preferred_element_type=F32)


def _split3(a):
    hi = a.astype(BF16)
    r1 = a - hi.astype(F32)
    mid = r1.astype(BF16)
    lo = (r1 - mid.astype(F32)).astype(BF16)
    return hi, mid, lo


def _dot_f32(a, b):
    ah, am, al = _split3(a)
    bh, bm, bl = _split3(b)
    d = lambda u, v: jnp.dot(u, v, preferred_element_type=F32)
    return d(ah, bh) + (d(ah, bm) + d(am, bh)) + (d(am, bm) + d(ah, bl) + d(al, bh))


def _dot_sel_l(sel, a):
    ah, am, al = _split3(a)
    d = lambda v: jnp.dot(sel, v, preferred_element_type=F32)
    return d(ah) + d(am) + d(al)


def _dot_sel_r(a, sel):
    ah, am, al = _split3(a)
    d = lambda u: jnp.dot(u, sel, preferred_element_type=F32)
    return d(ah) + d(am) + d(al)


def _sigmoid(x):
    return 1.0 / (1.0 + jnp.exp(-x))


def _log_sigmoid(x):
    return jnp.minimum(x, 0.0) - jnp.log1p(jnp.exp(-jnp.abs(x)))


def _logaddexp(a, b):
    amax = jnp.maximum(a, b)
    delta = a - b
    return jnp.where(jnp.isnan(delta), a + b, amax + jnp.log1p(jnp.exp(-jnp.abs(delta))))


def _rms(x, width):
    return x * lax.rsqrt(jnp.sum(x * x, axis=-1, keepdims=True) / width + EPS)


def _cparams(sem, vmem=VMEM_LIMIT):
    return pltpu.CompilerParams(dimension_semantics=sem, vmem_limit_bytes=vmem)


def _mod_kernel(c_ref, w_ref, b_ref, o_ref):
    c = c_ref[...]
    act = c * _sigmoid(c)
    o_ref[0] = _dot_f32(act, w_ref[0]) + b_ref[0]


def _mod_vectors(c8, w_mod, b_mod):
    depth, d, six_d = w_mod.shape
    nj = six_d // d
    return pl.pallas_call(
        _mod_kernel,
        grid=(depth, nj),
        in_specs=[
            pl.BlockSpec((8, d), lambda l, j: (0, 0)),
            pl.BlockSpec((1, d, d), lambda l, j: (l, 0, j)),
            pl.BlockSpec((1, 1, d), lambda l, j: (l, 0, j)),
        ],
        out_specs=pl.BlockSpec((1, 8, d), lambda l, j: (l, 0, j)),
        out_shape=jax.ShapeDtypeStruct((depth, 8, six_d), F32),
        compiler_params=_cparams(("arbitrary", "arbitrary")),
        name="mod_vectors",
    )(c8, w_mod, b_mod.reshape(depth, 1, six_d))


def _mod_row(m_ref, row):
    return m_ref[pl.ds(row, 1), :]


def _inproj_kernel(*refs, fuse_res, nblk):
    if fuse_res:
        x_ref, y_ref, mprev_ref, m_ref, g_ref, w_ref, xo_ref, pa_ref, pb_ref, pc_ref, pd_ref = refs
    else:
        x_ref, m_ref, g_ref, w_ref, pa_ref, pb_ref, pc_ref, pd_ref = refs
    i = pl.program_id(0)
    b = i // nblk
    row = jnp.where(i % nblk == 0, 4, b)
    d = D_MODEL
    x = x_ref[...]
    if fuse_res:
        mp = _mod_row(mprev_ref, row)
        x = x + mp[:, 5 * d:6 * d] * _load_token_rows(y_ref)
        xo_ref[...] = x
    m = _mod_row(m_ref, row)
    h = _rms(x, d) * g_ref[...] * (1.0 + m[:, d:2 * d]) + m[:, 0:d]
    p = _bdot(h, w_ref[...])
    pa_ref[...] = p[:, 0:W_A]
    pb_ref[...] = p[:, W_A:W_A + W_B].astype(pb_ref.dtype)
    pc_ref[...] = p[:, W_A + W_B:W_A + W_B + W_C]
    pd_ref[...] = p[:, W_A + W_B + W_C:]


def _inproj(x, y, m_prev, m, g, w, nblk):
    t, d = x.shape
    fuse = y is not None
    rb = ROW_BLOCK
    row_spec = lambda w_: pl.BlockSpec((rb, w_), lambda i: (i, 0))
    full = lambda a: pl.BlockSpec(a.shape, lambda i: (0,) * a.ndim)
    ins, specs = [x], [row_spec(d)]
    if fuse:
        ins += [y, m_prev]
        specs += [pl.BlockSpec((rb * TOK_TILES, LANE), lambda i: (i, 0)), full(m_prev)]
    ins += [m, g, w]
    specs += [full(m), full(g), full(w)]
    outs, ospecs = [], []
    if fuse:
        outs.append(jax.ShapeDtypeStruct((t, d), F32))
        ospecs.append(row_spec(d))
    for w_ in (W_A, W_B, W_C, W_D):
        outs.append(jax.ShapeDtypeStruct((t, w_), BF16 if w_ == W_B else F32))
        ospecs.append(row_spec(w_))
    res = pl.pallas_call(
        functools.partial(_inproj_kernel, fuse_res=fuse, nblk=nblk),
        grid=(t // rb,),
        in_specs=specs,
        out_specs=ospecs,
        out_shape=outs,
        compiler_params=_cparams(("arbitrary",)),
        name="in_projection",
    )(*ins)
    if fuse:
        return res[0], res[1:]
    return x, res


def _sub_chunk(refs, i, tri, emat, emask_t, reverse):
    q_ref, k_ref, la_ref, v_ref, o_ref, st_ref, r_ref = refs
    hv = v_ref.shape[1]
    row_id = lax.broadcasted_iota(I32, (SUB, 1), 0)
    off = pl.multiple_of(i * SUB, SUB)
    qs = q_ref[pl.ds(off, SUB), :]
    ks = k_ref[pl.ds(off, SUB), :]
    las = la_ref[pl.ds(off, SUB), :]
    vs = v_ref[pl.ds(off, SUB), :]
    cum = _dot_sel_l(tri, las)
    last = cum[0:1, :] if reverse else cum[SUB - 1:SUB, :]
    for j in range(SUB):
        valid = (row_id <= j) if reverse else (row_id >= j)
        dlt = jnp.where(valid, cum - cum[j:j + 1, :], NEG_INF)
        r_ref[j * SUB:(j + 1) * SUB, :] = (qs * ks[j:j + 1, :] * jnp.exp(dlt)).astype(BF16)
    att = jnp.dot(r_ref[...], emat, preferred_element_type=F32)
    o = jnp.zeros((SUB, hv), F32)
    for j in range(SUB):
        o = o + att[j * SUB:(j + 1) * SUB, :] * vs[j:j + 1, :]
    st = st_ref[...]
    o = o + _bdot_nt(qs * jnp.exp(cum), st)
    o_ref[pl.ds(off, SUB), :] = o
    kd = ks * jnp.exp(last - cum)
    st_ref[...] = st * jnp.exp(last) + _bdot_tn(vs, kd) * emask_t


def _macro_step(refs, m, tri, emask_t, hmk, hmv, reverse):
    q_ref, k_ref, la_ref, v_ref, o_ref, st_ref, _ = refs
    hv = v_ref.shape[1]
    n_sub = MACRO // SUB
    off = pl.multiple_of(m * MACRO, MACRO)
    q = q_ref[pl.ds(off, MACRO), :]
    k = k_ref[pl.ds(off, MACRO), :]
    v = v_ref[pl.ds(off, MACRO), :]
    cum = _dot_sel_l(tri, la_ref[pl.ds(off, MACRO), :])
    last = cum[0:1, :] if reverse else cum[MACRO - 1:MACRO, :]
    st = st_ref[...]
    o_inter = _bdot_nt(q * jnp.exp(cum), st)
    st_ref[...] = st * jnp.exp(last) + _bdot_tn(v, k * jnp.exp(last - cum)) * emask_t
    vb = v.astype(BF16)
    row_i = lax.broadcasted_iota(I32, (N_HEADS * SUB, 1), 0) % SUB
    for s in range(n_sub):
        lo = s * SUB
        if reverse:
            k_lo, k_hi = lo, MACRO
            ref = cum[lo + SUB:lo + SUB + 1, :] if s < n_sub - 1 else jnp.zeros_like(last)
        else:
            k_lo, k_hi = 0, lo + SUB
            ref = cum[lo - 1:lo, :] if s > 0 else jnp.zeros_like(last)
        qt = q[lo:lo + SUB, :] * jnp.exp(cum[lo:lo + SUB, :] - ref)
        qs = jnp.concatenate([qt * hmk[h:h + 1, :] for h in range(N_HEADS)], axis=0).astype(BF16)
        kt = (k[k_lo:k_hi, :] * jnp.exp(ref - cum[k_lo:k_hi, :])).astype(BF16)
        att = lax.dot_general(qs, kt, (((1,), (1,)), ((), ())), preferred_element_type=F32)
        col = lax.broadcasted_iota(I32, (1, k_hi - k_lo), 1) + k_lo
        valid = (col >= lo + row_i) if reverse else (col <= lo + row_i)
        att = jnp.where(valid, att, 0.0)
        o_heads = jnp.dot(att.astype(BF16), vb[k_lo:k_hi, :], preferred_element_type=F32)
        o = o_inter[lo:lo + SUB, :]
        for h in range(N_HEADS):
            o = o + o_heads[h * SUB:(h + 1) * SUB, :] * hmv[h:h + 1, :]
        o_ref[pl.ds(off + lo, SUB), :] = o


def _scan_pair(fwd_refs, bwd_refs, consts):
    tri_f, tri_b, trim_f, trim_b, emat, emask_t, hmk, hmv, sub_sum = consts
    rows = fwd_refs[0].shape[0]
    n_sub, n_macro = rows // SUB, rows // MACRO
    tot_f = jnp.dot(sub_sum, fwd_refs[2][...].astype(BF16), preferred_element_type=F32)
    tot_b = jnp.dot(sub_sum, bwd_refs[2][...].astype(BF16), preferred_element_type=F32)
    factorisable = jnp.min(jnp.minimum(tot_f, tot_b)) > -DECAY_GUARD

    @pl.when(factorisable)
    def _():
        def body(step, carry):
            _macro_step(fwd_refs, step, trim_f, emask_t, hmk, hmv, False)
            _macro_step(bwd_refs, n_macro - 1 - step, trim_b, emask_t, hmk, hmv, True)
            return carry

        lax.fori_loop(0, n_macro, body, 0, unroll=True)

    @pl.when(jnp.logical_not(factorisable))
    def _():
        def body(step, carry):
            _sub_chunk(fwd_refs, step, tri_f, emat, emask_t, False)
            _sub_chunk(bwd_refs, n_sub - 1 - step, tri_b, emat, emask_t, True)
            return carry

        lax.fori_loop(0, n_sub, body, 0)


def _scan_consts(hk, hv, dk, dv):
    tri = lambda n, low: jnp.asarray(np.tril(np.ones((n, n), np.float32)) if low else np.triu(np.ones((n, n), np.float32)), BF16)
    em = _head_match(hk, hv, dk, dv)
    heads = np.arange(N_HEADS)[:, None]
    hmk = (np.arange(hk)[None, :] // dk == heads).astype(np.float32)
    hmv = (np.arange(hv)[None, :] // dv == heads).astype(np.float32)
    sub_sum = (np.arange(ROW_BLOCK)[None, :] // SUB == np.arange(ROW_BLOCK // SUB)[:, None]).astype(np.float32)
    return [tri(SUB, True), tri(SUB, False), tri(MACRO, True), tri(MACRO, False), jnp.asarray(em, BF16),
            jnp.asarray(em.T, F32), jnp.asarray(hmk), jnp.asarray(hmv), jnp.asarray(sub_sum, BF16)]


def _hgrn_prep(q_ref, v_ref, f_ref, lb, qo, ko, lo, vo):
    qr = q_ref[...]
    qo[...] = qr * _sigmoid(qr) * (A_DH ** -0.5)
    vo[...] = v_ref[...]
    z = f_ref[...]
    lo[...] = _logaddexp(jnp.log(lb), jnp.log1p(-lb) + _log_sigmoid(z))
    ko[...] = (1.0 - lb) * _sigmoid(-z)


def _hgrn_kernel(qf_ref, vf_ref, ff_ref, qb_ref, vb_ref, fb_ref, lb_ref, *rest):
    consts, (of_ref, ob_ref, stf, stb, rf, rb_, q1, k1, l1, v1, q2, k2, l2, v2) = rest[:N_SCAN_CONSTS], rest[N_SCAN_CONSTS:]
    @pl.when(pl.program_id(1) == 0)
    def _():
        stf[...] = jnp.zeros_like(stf)
        stb[...] = jnp.zeros_like(stb)

    _hgrn_prep(qf_ref, vf_ref, ff_ref, lb_ref[0:1, :], q1, k1, l1, v1)
    _hgrn_prep(qb_ref, vb_ref, fb_ref, lb_ref[1:2, :], q2, k2, l2, v2)
    _scan_pair((q1, k1, l1, v1, of_ref, stf, rf), (q2, k2, l2, v2, ob_ref, stb, rb_), [c[...] for c in consts])


def _head_match(hk, hv, dk, dv):
    m = (np.arange(hk)[:, None] // dk == np.arange(hv)[None, :] // dv).astype(np.float32)
    return m


def _bwd_block(t, nblk):
    return jnp.where(t == 0, 0, nblk - t)


def _hgrn_scan(pa, lb, batch, nblk):
    t_all = pa.shape[0]
    rb, w = ROW_BLOCK, D_GROUP
    consts = _scan_consts(w, w, A_DH, A_DH)
    fwd = lambda col: pl.BlockSpec((rb, w), lambda b, t: (b * nblk + t, col))
    bwd = lambda col: pl.BlockSpec((rb, w), lambda b, t: (b * nblk + _bwd_block(t, nblk), col))
    full = lambda a: pl.BlockSpec(a.shape, lambda b, t: (0,) * a.ndim)
    vm = lambda shape, dt=F32: pltpu.VMEM(shape, dt)
    return pl.pallas_call(
        _hgrn_kernel,
        grid=(batch, nblk),
        in_specs=[fwd(0), fwd(1), fwd(2), bwd(0), bwd(1), bwd(3), full(lb)] + [full(c) for c in consts],
        out_specs=[fwd(0), bwd(0)],
        out_shape=[jax.ShapeDtypeStruct((t_all, w), F32)] * 2,
        scratch_shapes=[vm((w, w)), vm((w, w)), vm((SUB * SUB, w), BF16), vm((SUB * SUB, w), BF16)]
        + [vm((rb, w))] * 8,
        compiler_params=_cparams(("arbitrary", "arbitrary")),
        name="hgrn_scan",
    )(pa, pa, pa, pa, pa, pa, lb, *consts)


def _gla_prep(q_ref, k_ref, v_ref, z_ref, wg_ref, bg_ref, qo, ko, lo, vo):
    qo[...] = q_ref[...] * (C_DK ** -0.5)
    ko[...] = k_ref[...]
    vo[...] = v_ref[...]
    zl = _dot_f32(z_ref[...], wg_ref[...]) + bg_ref[...]
    lo[...] = _log_sigmoid(zl) / C_GATE_NORM


def _gla_kernel(qf_ref, kf_ref, vf_ref, zf_ref, qb_ref, kb_ref, vb_ref, zb_ref, wgf_ref, bgf_ref, wgb_ref, bgb_ref, *rest):
    consts, (of_ref, ob_ref, stf, stb, rf, rb_, q1, k1, l1, v1, q2, k2, l2, v2) = rest[:N_SCAN_CONSTS], rest[N_SCAN_CONSTS:]
    @pl.when(pl.program_id(1) == 0)
    def _():
        stf[...] = jnp.zeros_like(stf)
        stb[...] = jnp.zeros_like(stb)

    _gla_prep(qf_ref, kf_ref, vf_ref, zf_ref, wgf_ref, bgf_ref, q1, k1, l1, v1)
    _gla_prep(qb_ref, kb_ref, vb_ref, zb_ref, wgb_ref, bgb_ref, q2, k2, l2, v2)
    _scan_pair((q1, k1, l1, v1, of_ref, stf, rf), (q2, k2, l2, v2, ob_ref, stb, rb_), [c[...] for c in consts])


def _gla_scan(pc, wg_f, bg_f, wg_b, bg_b, batch, nblk):
    t_all = pc.shape[0]
    rb = ROW_BLOCK
    hk, hv = N_HEADS * C_DK, N_HEADS * C_DV
    consts = _scan_consts(hk, hv, C_DK, C_DV)
    wgf = jnp.zeros((LANE, hk), F32).at[0:C_GATE_RANK].set(wg_f)
    wgb = jnp.zeros((LANE, hk), F32).at[C_GATE_RANK:2 * C_GATE_RANK].set(wg_b)
    bgf, bgb = bg_f.reshape(1, hk), bg_b.reshape(1, hk)
    fwd = lambda w, col: pl.BlockSpec((rb, w), lambda b, t: (b * nblk + t, col))
    bwd = lambda w, col: pl.BlockSpec((rb, w), lambda b, t: (b * nblk + _bwd_block(t, nblk), col))
    full = lambda a: pl.BlockSpec(a.shape, lambda b, t: (0,) * a.ndim)
    vm = lambda shape, dt=F32: pltpu.VMEM(shape, dt)
    return pl.pallas_call(
        _gla_kernel,
        grid=(batch, nblk),
        in_specs=[fwd(hk, 0), fwd(hk, 1), fwd(hv, 1), fwd(LANE, 6), bwd(hk, 0), bwd(hk, 1), bwd(hv, 1), bwd(LANE, 6),
                  full(wgf), full(bgf), full(wgb), full(bgb)] + [full(c) for c in consts],
        out_specs=[fwd(hv, 0), bwd(hv, 0)],
        out_shape=[jax.ShapeDtypeStruct((t_all, hv), F32)] * 2,
        scratch_shapes=[vm((hv, hk)), vm((hv, hk)), vm((SUB * SUB, hk), BF16), vm((SUB * SUB, hk), BF16)]
        + [vm((rb, hk)), vm((rb, hk)), vm((rb, hk)), vm((rb, hv))] * 2,
        compiler_params=_cparams(("arbitrary", "arbitrary")),
        name="gla_scan",
    )(pc, pc, pc, pc, pc, pc, pc, pc, wgf, bgf, wgb, bgb, *consts)


NA_QROWS = ROW_BLOCK // GRID_W
NA_WROWS = WIN_H + NA_QROWS


def _na_kernel(q_ref, k_ref, v_ref, bias_ref, hm_ref, o_ref, *, j0, rows, ctx):
    j = pl.program_id(1) + j0
    q = q_ref[...] * (B_DH ** -0.5)
    hm = hm_ref[...]
    kc = k_ref[0:ctx, :]
    vc = v_ref[0:ctx, :]
    nt = lambda a, b: lax.dot_general(a, b, (((1,), (1,)), ((), ())), preferred_element_type=F32)

    @pl.when(j == 0)
    def _():
        acc = jnp.zeros(q.shape, F32)
        for h in range(N_HEADS):
            mh = hm[h:h + 1, :]
            s = nt(q * mh.astype(BF16), kc)
            p = jnp.exp(s - jnp.max(s, axis=-1, keepdims=True))
            inv = 1.0 / jnp.sum(p, axis=-1, keepdims=True)
            acc = acc + jnp.dot(p.astype(BF16), vc, preferred_element_type=F32) * (mh * inv)
        o_ref[...] = acc.astype(o_ref.dtype)

    @pl.when(j > 0)
    def _():
        r0 = (j - 1) * NA_QROWS
        start = jnp.clip(r0 - WIN_H // 2, 0, rows - NA_WROWS)
        off = pl.multiple_of(ctx + start * GRID_W, GRID_W)
        kw = k_ref[pl.ds(off, NA_WROWS * GRID_W), :]
        vw = v_ref[pl.ds(off, NA_WROWS * GRID_W), :]
        acc = jnp.zeros(q.shape, F32)
        for h in range(N_HEADS):
            mh = hm[h:h + 1, :]
            qh = q * mh.astype(BF16)
            sw = nt(qh, kw) + bias_ref[0, h]
            sc = nt(qh, kc)
            m = jnp.maximum(jnp.max(sw, axis=-1, keepdims=True), jnp.max(sc, axis=-1, keepdims=True))
            pw = jnp.exp(sw - m)
            pc_ = jnp.exp(sc - m)
            inv = 1.0 / (jnp.sum(pw, axis=-1, keepdims=True) + jnp.sum(pc_, axis=-1, keepdims=True))
            o = (jnp.dot(pw.astype(BF16), vw, preferred_element_type=F32)
                 + jnp.dot(pc_.astype(BF16), vc, preferred_element_type=F32))
            acc = acc + o * (mh * inv)
        o_ref[...] = acc.astype(o_ref.dtype)


def _na_bias_table(rpb, rows):
    kh = WIN_H
    cidx = np.arange(GRID_W)
    c_start = np.clip(cidx - WIN_W // 2, 0, GRID_W - WIN_W)
    col_in = (cidx[None] >= c_start[:, None]) & (cidx[None] < c_start[:, None] + WIN_W)
    dc = np.clip(cidx[None] - cidx[:, None], -(WIN_W - 1), WIN_W - 1) + (WIN_W - 1)
    sel = (dc[None] == np.arange(2 * WIN_W - 1)[:, None, None]).astype(np.float32)
    by_col = jnp.einsum("hrc,cqw->hrqw", rpb.astype(F32), jnp.asarray(sel), precision=lax.Precision.HIGHEST)
    by_col = jnp.where(jnp.asarray(col_in)[None, None], by_col, NEG_INF)
    blocked = jnp.full((N_HEADS, GRID_W, GRID_W), NEG_INF, F32)
    tables = []
    for r0 in (0, NA_QROWS, rows - NA_QROWS):
        ws = int(np.clip(r0 - kh // 2, 0, rows - NA_WROWS))
        per_q = []
        for a in range(NA_QROWS):
            r = r0 + a
            s = int(np.clip(r - kh // 2, 0, rows - kh))
            per_w = [by_col[:, ws + jj - r + kh - 1] if s <= ws + jj < s + kh else blocked for jj in range(NA_WROWS)]
            per_q.append(jnp.stack(per_w, axis=2))
        tables.append(jnp.stack(per_q, axis=1))
    return jnp.stack(tables, axis=0).reshape(3, N_HEADS, NA_QROWS * GRID_W, NA_WROWS * GRID_W)


def _head_masks(width, dh):
    return jnp.asarray((np.arange(width)[None, :] // dh == np.arange(N_HEADS)[:, None]).astype(np.float32))


def _neighbourhood_attention(pb, rpb, batch, s_len, ctx, keep_ctx):
    rows = (s_len - ctx) // GRID_W
    assert rows >= NA_WROWS and rows % NA_QROWS == 0 and ctx == ROW_BLOCK
    rb = ROW_BLOCK
    j0 = 0 if keep_ctx else 1
    per_b = s_len // rb
    bias = _na_bias_table(rpb, rows)
    hm = _head_masks(D_GROUP, B_DH)

    def bias_idx(b, jj):
        r0 = (jj + j0 - 1) * NA_QROWS
        return (jnp.where(r0 <= 0, 0, jnp.where(r0 == rows - NA_QROWS, 2, 1)), 0, 0, 0)

    return pl.pallas_call(
        functools.partial(_na_kernel, j0=j0, rows=rows, ctx=ctx),
        grid=(batch, per_b - j0),
        in_specs=[
            pl.BlockSpec((rb, D_GROUP), lambda b, jj: (b * per_b + jj + j0, 0)),
            pl.BlockSpec((s_len, D_GROUP), lambda b, jj: (b, 1)),
            pl.BlockSpec((s_len, D_GROUP), lambda b, jj: (b, 2)),
            pl.BlockSpec((1, N_HEADS, rb, NA_WROWS * GRID_W), bias_idx),
            pl.BlockSpec(hm.shape, lambda b, jj: (0, 0)),
        ],
        out_specs=pl.BlockSpec((rb, D_GROUP), lambda b, jj: (b * (per_b - j0) + jj, 0)),
        out_shape=jax.ShapeDtypeStruct((batch * (per_b - j0) * rb, D_GROUP), BF16),
        compiler_params=_cparams(("arbitrary", "arbitrary")),
        name="neighbourhood_attention",
    )(pb, pb, pb, bias, hm)


def _mla_prep_kernel(pd_ref, gq_ref, gkv_ref, wq1_ref, wq2_ref, wk_ref, wvt_ref, ones_ref, cq_ref, sq_ref, tk_ref,
                     place_ref, q_ref, k_ref, vt_ref):
    pd = pd_ref[...]
    cq = pd[:, 0:256]
    ckv = pd[:, 256:384]
    kr = pd[:, 384:512]
    qn = _rms(cq, D_Q_RANK) * gq_ref[...]
    q = _bdot(qn, wq1_ref[...]) * cq_ref[...] + _bdot(qn, wq2_ref[...]) * sq_ref[...]
    q_ref[...] = q.astype(BF16)
    kvn = (_rms(ckv, D_KV_RANK) * gkv_ref[...]).astype(BF16)
    k = jnp.dot(kvn, wk_ref[...], preferred_element_type=F32) + _dot_sel_r(kr * tk_ref[...], place_ref[...])
    k_ref[...] = k.astype(BF16)
    vt = lax.dot_general(wvt_ref[...], kvn, (((1,), (1,)), ((), ())), preferred_element_type=F32) + ones_ref[...]
    vt_ref[0] = vt.astype(BF16)


def _rope_swap_perm():
    f = ROPE_FREQS
    return np.concatenate([np.arange(f, 2 * f), np.arange(0, f), np.arange(3 * f, 4 * f), np.arange(2 * f, 3 * f)])


def _mla_tables(n, ctx):
    t = np.arange(n)
    inv_freq = ROPE_BASE ** (-np.arange(ROPE_FREQS, dtype=np.float32) / ROPE_FREQS)
    ang_r = (t // GRID_W).astype(np.float32)[:, None] * inv_freq
    ang_c = (t % GRID_W).astype(np.float32)[:, None] * inv_freq
    cos32 = np.concatenate([np.cos(ang_r), np.cos(ang_r), np.cos(ang_c), np.cos(ang_c)], axis=1)
    sin32 = np.concatenate([-np.sin(ang_r), np.sin(ang_r), -np.sin(ang_c), np.sin(ang_c)], axis=1)
    cos32 = np.concatenate([np.ones((ctx, D_ROPE), np.float32), cos32.astype(np.float32)], axis=0)
    sin32 = np.concatenate([np.zeros((ctx, D_ROPE), np.float32), sin32.astype(np.float32)], axis=0)
    s_len = n + ctx
    cq = np.zeros((s_len, N_HEADS, HEAD_TILE), np.float32)
    sq = np.zeros((s_len, N_HEADS, HEAD_TILE), np.float32)
    cq[:, :, 0:D_NOPE] = MLA_SCALE
    cq[:, :, D_NOPE:D_NOPE + D_ROPE] = cos32[:, None, :] * MLA_SCALE
    sq[:, :, D_NOPE:D_NOPE + D_ROPE] = sin32[:, None, :] * MLA_SCALE
    tk = np.zeros((s_len, LANE), np.float32)
    tk[:, 0:D_ROPE] = cos32
    tk[:, D_ROPE:2 * D_ROPE] = sin32
    place = np.zeros((LANE, N_HEADS * HEAD_TILE), np.float32)
    for h in range(N_HEADS):
        for l in range(D_ROPE):
            place[l, h * HEAD_TILE + D_NOPE + l] = 1.0
            place[D_ROPE + l, h * HEAD_TILE + D_NOPE + l] = 1.0
    return (jnp.asarray(cq.reshape(s_len, -1)), jnp.asarray(sq.reshape(s_len, -1)), jnp.asarray(tk),
            jnp.asarray(place, BF16))


def _mla_weights(q_norm_g, w_uq, kv_norm_g, w_ukv):
    perm = _rope_swap_perm()
    wq = w_uq.reshape(D_Q_RANK, N_HEADS, D_NOPE + D_ROPE)
    wq1 = jnp.zeros((256, N_HEADS, HEAD_TILE), F32).at[0:D_Q_RANK, :, 0:D_NOPE + D_ROPE].set(wq)
    wq2 = jnp.zeros((256, N_HEADS, HEAD_TILE), F32).at[0:D_Q_RANK, :, D_NOPE:D_NOPE + D_ROPE].set(
        wq[:, :, D_NOPE:][:, :, perm])
    wkv = w_ukv.reshape(D_KV_RANK, N_HEADS, D_NOPE + D_V)
    wk = jnp.zeros((D_KV_RANK, N_HEADS, HEAD_TILE), F32).at[:, :, 0:D_NOPE].set(wkv[:, :, 0:D_NOPE])
    wvt = jnp.zeros((N_HEADS, VT_ROWS, D_KV_RANK), F32).at[:, 0:D_V, :].set(jnp.transpose(wkv[:, :, D_NOPE:], (1, 2, 0)))
    wvt = wvt.reshape(N_HEADS * VT_ROWS, D_KV_RANK)
    ones_rows = jnp.asarray((np.arange(N_HEADS * VT_ROWS) % VT_ROWS >= D_V).astype(np.float32)).reshape(-1, 1)
    hw = N_HEADS * HEAD_TILE
    gq = jnp.zeros((1, 256), F32).at[0, 0:D_Q_RANK].set(q_norm_g)
    return (gq, kv_norm_g.reshape(1, D_KV_RANK), wq1.reshape(256, hw).astype(BF16), wq2.reshape(256, hw).astype(BF16),
            wk.reshape(D_KV_RANK, hw).astype(BF16), wvt.astype(BF16), ones_rows)


def _mla_prep(pd, weights, tables, nblk):
    t_all = pd.shape[0]
    rb = ROW_BLOCK
    gq, gkv, wq1, wq2, wk, wvt, ones_rows = weights
    cq, sq, tk, place = tables
    hw = N_HEADS * HEAD_TILE
    hv = N_HEADS * VT_ROWS
    row = lambda w: pl.BlockSpec((rb, w), lambda i: (i, 0))
    pos = lambda w: pl.BlockSpec((rb, w), lambda i: (i % nblk, 0))
    full = lambda a: pl.BlockSpec(a.shape, lambda i: (0,) * a.ndim)
    return pl.pallas_call(
        _mla_prep_kernel,
        grid=(t_all // rb,),
        in_specs=[row(W_D), full(gq), full(gkv), full(wq1), full(wq2), full(wk), full(wvt), full(ones_rows),
                  pos(hw), pos(hw), pos(LANE), full(place)],
        out_specs=[row(hw), row(hw), pl.BlockSpec((1, hv, rb), lambda i: (i // nblk, 0, i % nblk))],
        out_shape=[jax.ShapeDtypeStruct((t_all, hw), BF16), jax.ShapeDtypeStruct((t_all, hw), BF16),
                   jax.ShapeDtypeStruct((t_all // (nblk * rb), hv, nblk * rb), BF16)],
        compiler_params=_cparams(("arbitrary",)),
        name="mla_prep",
    )(pd, gq, gkv, wq1, wq2, wk, wvt, ones_rows, cq, sq, tk, place)


MLA_HEADS_PER_STEP = 2
MLA_KEY_CHUNK = 256


def _mla_attn_kernel(q_ref, k_ref, vt_ref, o_ref, s_scr, *, j0, ctx):
    j = pl.program_id(2) + j0
    rb = q_ref.shape[0]

    def heads(n_keys):
        hs = range(MLA_HEADS_PER_STEP)
        chunks = [slice(c * MLA_KEY_CHUNK, (c + 1) * MLA_KEY_CHUNK) for c in range(n_keys // MLA_KEY_CHUNK)]
        lanes = [slice(h * HEAD_TILE, (h + 1) * HEAD_TILE) for h in hs]
        qs = [q_ref[:, lanes[h]] for h in hs]
        m = [jnp.full((1, rb), NEG_INF, F32) for _ in hs]
        for keys in chunks:
            for h in hs:
                st = lax.dot_general(k_ref[keys, lanes[h]], qs[h], (((1,), (1,)), ((), ())),
                                     preferred_element_type=F32)
                s_scr[h, keys, :] = st
                m[h] = jnp.maximum(m[h], jnp.max(st, axis=0, keepdims=True))
        acc = [jnp.zeros((VT_ROWS, rb), F32) for _ in hs]
        for keys in chunks:
            for h in hs:
                pt = jnp.exp(s_scr[h, keys, :] - m[h]).astype(BF16)
                acc[h] = acc[h] + jnp.dot(vt_ref[0, h * VT_ROWS:(h + 1) * VT_ROWS, keys], pt,
                                          preferred_element_type=F32)
        for h in hs:
            o_ref[0, h * D_V:(h + 1) * D_V, :] = (acc[h][0:D_V, :] * (1.0 / acc[h][D_V:D_V + 1, :])).astype(o_ref.dtype)

    @pl.when(j == 0)
    def _():
        heads(ctx)

    @pl.when(j > 0)
    def _():
        heads(k_ref.shape[0])


def _mla_attention(q, k, vt, batch, s_len, ctx, keep_ctx):
    rb = ROW_BLOCK
    nblk = s_len // rb
    j0 = 0 if keep_ctx else 1
    hps = MLA_HEADS_PER_STEP
    return pl.pallas_call(
        functools.partial(_mla_attn_kernel, j0=j0, ctx=ctx),
        grid=(batch, N_HEADS // hps, nblk - j0),
        in_specs=[
            pl.BlockSpec((rb, hps * HEAD_TILE), lambda b, h, jj: (b * nblk + jj + j0, h)),
            pl.BlockSpec((s_len, hps * HEAD_TILE), lambda b, h, jj: (b, h)),
            pl.BlockSpec((1, hps * VT_ROWS, s_len), lambda b, h, jj: (b, h, 0)),
        ],
        out_specs=pl.BlockSpec((1, hps * D_V, rb), lambda b, h, jj: (b * (nblk - j0) + jj, h, 0)),
        out_shape=jax.ShapeDtypeStruct((batch * (nblk - j0), N_HEADS * D_V, rb), BF16),
        scratch_shapes=[pltpu.VMEM((hps, s_len, rb), F32)],
        compiler_params=_cparams(("arbitrary", "arbitrary", "arbitrary")),
        name="mla_attention",
    )(q, k, vt)


def _outproj_kernel(oaf_ref, oab_ref, ga_ref, ob_ref, ogf_ref, ogb_ref, gc_ref, od_ref, x_ref, m_ref, gha_ref, ghc_ref,
                    g2_ref, wa_ref, wb_ref, wc_ref, wd_ref, em_ref, x1_ref, h2_ref, *, j0, nblk_out):
    i = pl.program_id(0)
    b = i // nblk_out
    row = jnp.where(i % nblk_out + j0 == 0, 4, b)
    d = D_MODEL
    em = em_ref[...]

    def readout(o, g_norm, gate, dh):
        ms = _dot_sel_r(o * o, em) / dh
        return o * lax.rsqrt(ms + EPS) * g_norm * (gate * _sigmoid(gate))

    a = readout(oaf_ref[...] + oab_ref[...], gha_ref[...], ga_ref[...], A_DH)
    c = readout(ogf_ref[...] + ogb_ref[...], ghc_ref[...], gc_ref[...], C_DV)
    mix = (_bdot(a, wa_ref[...]) + jnp.dot(ob_ref[...], wb_ref[...], preferred_element_type=F32)
           + _bdot(c, wc_ref[...])
           + lax.dot_general(od_ref[0], wd_ref[...], (((0,), (0,)), ((), ())), preferred_element_type=F32))
    m = _mod_row(m_ref, row)
    x1 = x_ref[...] + m[:, 2 * d:3 * d] * mix
    x1_ref[...] = x1
    _store_token_rows(h2_ref, _rms(x1, d) * g2_ref[...] * (1.0 + m[:, 4 * d:5 * d]) + m[:, 3 * d:4 * d])


def _outproj(oaf, oab, pa, ob, ogf, ogb, pc, od, x, m, gha, ghc, g2, w_out, batch, nblk, keep_ctx):
    rb, d = ROW_BLOCK, D_MODEL
    j0 = 0 if keep_ctx else 1
    nblk_out = nblk - j0
    t_out = batch * nblk_out * rb
    wa = w_out[0:256].astype(BF16)
    wb = w_out[256:512].astype(BF16)
    wc = w_out[512:768].astype(BF16)
    wd = w_out[768:1024].astype(BF16)
    em = jnp.asarray(_head_match(D_GROUP, D_GROUP, 64, 64), BF16)
    src = lambda i: (i // nblk_out) * nblk + i % nblk_out + j0
    row_in = lambda w, col=0: pl.BlockSpec((rb, w), lambda i: (src(i), col))
    row_out = lambda w: pl.BlockSpec((rb, w), lambda i: (i, 0))
    full = lambda a_: pl.BlockSpec(a_.shape, lambda i: (0,) * a_.ndim)
    return pl.pallas_call(
        functools.partial(_outproj_kernel, j0=j0, nblk_out=nblk_out),
        grid=(t_out // rb,),
        in_specs=[row_in(256), row_in(256), row_in(256, 4), row_out(256), row_in(256), row_in(256), row_in(256, 2),
                  pl.BlockSpec((1, N_HEADS * D_V, rb), lambda i: (i, 0, 0)), row_in(d), full(m), full(gha), full(ghc), full(g2),
                  full(wa), full(wb), full(wc), full(wd), full(em)],
        out_specs=[row_out(d), pl.BlockSpec((rb * TOK_TILES, LANE), lambda i: (i, 0))],
        out_shape=[jax.ShapeDtypeStruct((t_out, d), F32), jax.ShapeDtypeStruct((t_out * TOK_TILES, LANE), F32)],
        compiler_params=_cparams(("arbitrary",)),
        name="out_projection",
    )(oaf, oab, pa, ob, ogf, ogb, pc, od, x, m, gha, ghc, g2, wa, wb, wc, wd, em)


def _load_token_rows(ref):
    rows = ref.shape[0] // TOK_TILES
    return jnp.concatenate([ref[pl.ds(k, rows, stride=TOK_TILES), :] for k in range(TOK_TILES)], axis=1)


def _store_token_rows(ref, val):
    rows = val.shape[0]
    for k in range(TOK_TILES):
        ref[pl.ds(k, rows, stride=TOK_TILES), :] = val[:, k * LANE:(k + 1) * LANE]


def _router_logits(h, wr_ref, br_ref):
    lg = _dot_f32(h, wr_ref[...]) + br_ref[...]
    lane = lax.broadcasted_iota(I32, lg.shape, 1).astype(F32)
    return lg, lane


def _top_group(lg, lane):
    gl = jnp.where(lane < N_GROUPS, lg, NEG_INF)
    gmax = jnp.max(gl, axis=-1, keepdims=True)
    gsel = jnp.min(jnp.where(gl == gmax, lane, float(LANE)), axis=-1, keepdims=True)
    p_group = 1.0 / jnp.sum(jnp.exp(gl - gmax), axis=-1, keepdims=True)
    return gsel, p_group


def _expert_gates(lg, lane, lo, p_group):
    big = float(LANE)
    el = jnp.where((lane >= lo) & (lane < lo + EXPERTS_PER_GROUP), lg, NEG_INF)
    m1 = jnp.max(el, axis=-1, keepdims=True)
    i1 = jnp.min(jnp.where(el == m1, lane, big), axis=-1, keepdims=True)
    el2 = jnp.where(lane == i1, NEG_INF, el)
    m2 = jnp.max(el2, axis=-1, keepdims=True)
    i2 = jnp.min(jnp.where(el2 == m2, lane, big), axis=-1, keepdims=True)
    t = jnp.exp(m2 - m1)
    w1 = p_group / (1.0 + t)
    w2 = p_group * t / (1.0 + t)
    return jnp.where(lane == i1 - lo, w1, jnp.where(lane == i2 - lo, w2, 0.0))


def _router_kernel(h_ref, wr_ref, br_ref, tril_ref, meta_ref, cnt_ref, carry):
    @pl.when(pl.program_id(0) == 0)
    def _():
        carry[...] = jnp.zeros_like(carry)

    lg, lane = _router_logits(_load_token_rows(h_ref), wr_ref, br_ref)
    gsel, _ = _top_group(lg, lane)
    onehot = jnp.where(lane == gsel, 1.0, 0.0)
    incl = jnp.dot(tril_ref[...], onehot.astype(BF16), preferred_element_type=F32)
    rank = jnp.sum(onehot * (incl - 1.0 + carry[...]), axis=-1, keepdims=True)
    carry[...] = carry[...] + jnp.sum(onehot, axis=0, keepdims=True)
    meta_ref[...] = jnp.where(lane == 0, gsel, jnp.where(lane == 1, rank, 0.0)).astype(I32)
    cnt_ref[...] = carry[...].astype(I32)


def _router_weights(w_rg, b_rg, w_re, b_re):
    d = w_rg.shape[0]
    ne = N_GROUPS * EXPERTS_PER_GROUP
    wr = jnp.zeros((d, LANE), F32).at[:, 0:N_GROUPS].set(w_rg).at[:, N_GROUPS:N_GROUPS + ne].set(w_re)
    br = jnp.zeros((1, LANE), F32).at[0, 0:N_GROUPS].set(b_rg).at[0, N_GROUPS:N_GROUPS + ne].set(b_re)
    return wr, br


def _router(h2t, wr, br):
    t = h2t.shape[0] // TOK_TILES
    rb = ROW_BLOCK
    tril = jnp.asarray(np.tril(np.ones((rb, rb), np.float32)), BF16)
    full = lambda a: pl.BlockSpec(a.shape, lambda i: (0,) * a.ndim)
    return pl.pallas_call(
        _router_kernel,
        grid=(t // rb,),
        in_specs=[pl.BlockSpec((rb * TOK_TILES, LANE), lambda i: (i, 0)), full(wr), full(br), full(tril)],
        out_specs=[pl.BlockSpec((rb, LANE), lambda i: (i, 0)), pl.BlockSpec((1, LANE), lambda i: (0, 0))],
        out_shape=[jax.ShapeDtypeStruct((t, LANE), I32), jax.ShapeDtypeStruct((1, LANE), I32)],
        scratch_shapes=[pltpu.VMEM((1, LANE), F32)],
        compiler_params=_cparams(("arbitrary",)),
        name="moe_router",
    )(h2t, wr, br, tril)


def _invert_kernel(dest_ref, inv_ref, *, n_tok):
    def spare(s, c):
        inv_ref[s] = n_tok + s % ROW_BLOCK
        return c

    lax.fori_loop(0, inv_ref.shape[0], spare, 0, unroll=8)

    def put(t, c):
        inv_ref[dest_ref[t]] = t
        return c

    lax.fori_loop(0, n_tok, put, 0, unroll=8)


def _invert(dest, n_slots):
    smem = pl.BlockSpec(memory_space=pltpu.SMEM)
    return pl.pallas_call(
        functools.partial(_invert_kernel, n_tok=dest.shape[0]), in_specs=[smem], out_specs=smem,
        out_shape=jax.ShapeDtypeStruct((n_slots,), I32), name="moe_invert",
    )(dest)


def _token_copy(src_ref, dst_ref, s, d, sem):
    s8 = pl.multiple_of(s * TOK_TILES, TOK_TILES)
    d8 = pl.multiple_of(d * TOK_TILES, TOK_TILES)
    return pltpu.make_async_copy(src_ref.at[pl.ds(s8, TOK_TILES), :], dst_ref.at[pl.ds(d8, TOK_TILES), :], sem)


def _experts_kernel(bg_ref, inv_ref, h_ref, wr_ref, br_ref, wgu_ref, wdn_ref, y_ref, xbuf, ybuf, gsem, ssem, *, n_tok):
    i = pl.program_id(0)
    n_steps = pl.num_programs(0)
    rb, d = ROW_BLOCK, D_MODEL
    slot = i % 2
    other = 1 - slot
    nxt = jnp.minimum(i + 1, n_steps - 1)
    prev = jnp.maximum(i - 1, 0)
    block_rows = rb * TOK_TILES

    def gather_start(blk, sl, r):
        tok = jnp.minimum(inv_ref[blk * rb + r], n_tok - 1)
        _token_copy(h_ref, xbuf.at[sl], tok, r, gsem.at[sl]).start()

    def scatter_start(blk, sl, r, to_spare):
        dst = jnp.where(to_spare, n_tok + r, inv_ref[blk * rb + r])
        _token_copy(ybuf.at[sl], y_ref, r, dst, ssem.at[sl]).start()

    def gather_wait(sl):
        pltpu.make_async_copy(h_ref.at[pl.ds(0, block_rows), :], xbuf.at[sl], gsem.at[sl]).wait()

    def scatter_wait(sl):
        pltpu.make_async_copy(ybuf.at[sl], y_ref.at[pl.ds(0, block_rows), :], ssem.at[sl]).wait()

    @pl.when(i == 0)
    def _():
        ybuf[...] = jnp.zeros_like(ybuf)

        def one(r, c):
            gather_start(0, 0, r)
            return c
        lax.fori_loop(0, rb, one, 0, unroll=8)

    gather_wait(slot)

    @pl.when(i >= 1)
    def _():
        scatter_wait(slot)

    xf = _load_token_rows(xbuf.at[slot])
    lg, lane = _router_logits(xf, wr_ref, br_ref)
    _, p_group = _top_group(lg, lane)
    lo = (N_GROUPS + bg_ref[i] * EXPERTS_PER_GROUP).astype(F32)
    gates = _expert_gates(lg, lane, lo, p_group)
    x = xf.astype(BF16)
    acc = jnp.zeros((rb, d), F32)
    per_expert = rb // EXPERTS_PER_GROUP
    for e in range(EXPERTS_PER_GROUP):
        gu = jnp.dot(x, wgu_ref[0, e], preferred_element_type=F32)
        g = gu[:, 0:D_EXPERT]
        act = g * _sigmoid(g) * gu[:, D_EXPERT:]
        y = jnp.dot(act.astype(BF16), wdn_ref[0, e], preferred_element_type=F32)
        acc = acc + gates[:, e:e + 1] * y
        for r in range(e * per_expert, (e + 1) * per_expert):
            gather_start(nxt, other, r)
            scatter_start(prev, other, r, i == 0)
    _store_token_rows(ybuf.at[slot], acc)

    @pl.when(i == n_steps - 1)
    def _():
        scatter_wait(other)

        def one(r, c):
            scatter_start(i, slot, r, False)
            return c
        lax.fori_loop(0, rb, one, 0, unroll=8)
        scatter_wait(slot)
        gather_wait(other)


def _experts(block_group, inv, h2t, wr, br, w_gu, w_dn):
    n_blocks = block_group.shape[0]
    rb, d = ROW_BLOCK, D_MODEL
    n_tok = h2t.shape[0] // TOK_TILES
    wgu = w_gu.reshape(N_GROUPS, EXPERTS_PER_GROUP, d, 2 * D_EXPERT)
    wdn = w_dn.reshape(N_GROUPS, EXPERTS_PER_GROUP, D_EXPERT, d)
    any_spec = pl.BlockSpec(memory_space=pl.ANY)
    return pl.pallas_call(
        functools.partial(_experts_kernel, n_tok=n_tok),
        grid_spec=pltpu.PrefetchScalarGridSpec(
            num_scalar_prefetch=2, grid=(n_blocks,),
            in_specs=[
                any_spec,
                pl.BlockSpec(wr.shape, lambda i, bg, inv_: (0, 0)),
                pl.BlockSpec(br.shape, lambda i, bg, inv_: (0, 0)),
                pl.BlockSpec((1, EXPERTS_PER_GROUP, d, 2 * D_EXPERT), lambda i, bg, inv_: (bg[i], 0, 0, 0)),
                pl.BlockSpec((1, EXPERTS_PER_GROUP, D_EXPERT, d), lambda i, bg, inv_: (bg[i], 0, 0, 0)),
            ],
            out_specs=any_spec,
            scratch_shapes=[pltpu.VMEM((2, rb * TOK_TILES, LANE), F32), pltpu.VMEM((2, rb * TOK_TILES, LANE), F32),
                            pltpu.SemaphoreType.DMA((2,)), pltpu.SemaphoreType.DMA((2,))]),
        out_shape=jax.ShapeDtypeStruct(((n_tok + rb) * TOK_TILES, LANE), F32),
        compiler_params=pltpu.CompilerParams(dimension_semantics=("arbitrary",), vmem_limit_bytes=VMEM_LIMIT,
                                             has_side_effects=True),
        name="moe_experts",
    )(block_group, inv, h2t, wr, br, wgu, wdn)


def _moe(h2t, w_rg, b_rg, w_re, b_re, w_gu_bf16, w_dn_bf16):
    t = h2t.shape[0] // TOK_TILES
    rb = ROW_BLOCK
    wr, br = _router_weights(w_rg, b_rg, w_re, b_re)
    meta, counts = _router(h2t, wr, br)
    group, rank = meta[:, 0], meta[:, 1]
    cnt = counts[0, 0:N_GROUPS]
    padded = (cnt + rb - 1) // rb * rb
    seg_end = jnp.cumsum(padded)
    seg_start = seg_end - padded
    dest = seg_start[group] + rank
    n_blocks = t // rb + N_GROUPS
    block_start = jnp.arange(n_blocks, dtype=I32) * rb
    block_group = jnp.minimum(jnp.sum((block_start[:, None] >= seg_end[None, :]).astype(I32), axis=1), N_GROUPS - 1)
    inv = _invert(dest, n_blocks * rb)
    return _experts(block_group, inv, h2t, wr, br, w_gu_bf16, w_dn_bf16)


def _final_kernel(x_ref, y_ref, m_ref, g_ref, o_ref, *, nblk):
    b = pl.program_id(0) // nblk
    d = D_MODEL
    m = _mod_row(m_ref, b)
    x = x_ref[...] + m[:, 5 * d:6 * d] * _load_token_rows(y_ref)
    o_ref[...] = _rms(x, d) * g_ref[...]


def _final(x1, y, m, g, nblk):
    t, d = x1.shape
    rb = ROW_BLOCK
    row = pl.BlockSpec((rb, d), lambda i: (i, 0))
    tok = pl.BlockSpec((rb * TOK_TILES, LANE), lambda i: (i, 0))
    full = lambda a: pl.BlockSpec(a.shape, lambda i: (0,) * a.ndim)
    return pl.pallas_call(
        functools.partial(_final_kernel, nblk=nblk),
        grid=(t // rb,),
        in_specs=[row, tok, full(m), full(g)],
        out_specs=row,
        out_shape=jax.ShapeDtypeStruct((t, d), F32),
        compiler_params=_cparams(("arbitrary",)),
        name="final_norm",
    )(x1, y, m, g)


def _inproj_weight(w_in):
    d = w_in.shape[0]
    z = lambda n: jnp.zeros((d, n), w_in.dtype)
    perm = _rope_swap_perm()
    kr = w_in[:, 3168:3200]
    cols = [w_in[:, 0:2048], w_in[:, 2048:2848], z(W_C - 800),
            w_in[:, 2848:3040], z(256 - D_Q_RANK), w_in[:, 3040:3168], kr, kr[:, perm], z(LANE - 2 * D_ROPE)]
    return jnp.concatenate(cols, axis=1).astype(BF16)


def _hgrn_lower_bounds(logits):
    cum = jnp.cumsum(jax.nn.softmax(logits.astype(F32), axis=0), axis=0)
    return cum - cum[0]


def kernel(x, c, ctx, c_ctx, w_mod, b_mod, norm1_g, norm2_g, w_in, w_out, hgrn_lb_logits, hgrn_norm_g, na_rpb, gla_wg_f, gla_bg_f, gla_wg_b, gla_bg_b, gla_norm_g, mla_q_norm_g, mla_w_uq, mla_kv_norm_g, mla_w_ukv, moe_w_rg, moe_b_rg, moe_w_re, moe_b_re, moe_w_gu, moe_w_dn, final_norm_g):
    batch, n, d = x.shape
    l_ctx = ctx.shape[1]
    assert d == D_MODEL and l_ctx == ROW_BLOCK and n % ROW_BLOCK == 0 and batch <= 4
    s_len = l_ctx + n
    nblk = s_len // ROW_BLOCK
    depth = w_mod.shape[0]

    c8 = jnp.zeros((8, d), F32).at[0:batch].set(c).at[4].set(c_ctx)
    mods = _mod_vectors(c8, w_mod, b_mod)
    lower_bounds = _hgrn_lower_bounds(hgrn_lb_logits)
    tables = _mla_tables(n, l_ctx)

    xa = jnp.concatenate([ctx, x], axis=1).reshape(batch * s_len, d)
    y_prev = None
    for layer in range(depth):
        keep_ctx = layer < depth - 1
        m = mods[layer]
        xa, (pa, pb, pc, pd) = _inproj(xa, y_prev, mods[layer - 1] if layer else None, m,
                                       norm1_g[layer].reshape(1, d), _inproj_weight(w_in[layer]), nblk)
        oaf, oab = _hgrn_scan(pa, lower_bounds[layer], batch, nblk)
        ob = _neighbourhood_attention(pb, na_rpb[layer], batch, s_len, l_ctx, keep_ctx)
        ogf, ogb = _gla_scan(pc, gla_wg_f[layer], gla_bg_f[layer], gla_wg_b[layer], gla_bg_b[layer], batch, nblk)
        mla_w = _mla_weights(mla_q_norm_g[layer], mla_w_uq[layer], mla_kv_norm_g[layer], mla_w_ukv[layer])
        q, k, v = _mla_prep(pd, mla_w, tables, nblk)
        od = _mla_attention(q, k, v, batch, s_len, l_ctx, keep_ctx)
        xa, h2 = _outproj(oaf, oab, pa, ob, ogf, ogb, pc, od, xa, m, hgrn_norm_g[layer].reshape(1, -1),
                          gla_norm_g[layer].reshape(1, -1), norm2_g[layer].reshape(1, d), w_out[layer],
                          batch, nblk, keep_ctx)
        y_prev = _moe(h2, moe_w_rg[layer], moe_b_rg[layer], moe_w_re[layer], moe_b_re[layer],
                      moe_w_gu[layer].astype(BF16), moe_w_dn[layer].astype(BF16))
    out = _final(xa, y_prev, mods[depth - 1], final_norm_g.reshape(1, d), n // ROW_BLOCK)
    return out.reshape(batch, n, d)
```

```python
import functools

import numpy as np
import jax
import jax.numpy as jnp
from jax import lax
from jax.experimental import pallas as pl
from jax.experimental.pallas import tpu as pltpu

F32 = jnp.float32
BF16 = jnp.bfloat16
I32 = jnp.int32

D_MODEL = 1024
DEPTH = 2
GRID_W = 64
EPS = 1e-6
D_GROUP = 256
N_HEADS = 4
A_DH = 64
B_DH = 64
WIN_H = 8
WIN_W = 16
C_DK = 32
C_DV = 64
C_GATE_RANK = 16
C_GATE_NORM = 16.0
D_NOPE = 64
D_V = 64
D_ROPE = 32
ROPE_FREQS = 8
ROPE_BASE = 10000.0
D_Q_RANK = 192
D_KV_RANK = 128
MLA_SCALE = (D_NOPE + D_ROPE) ** -0.5
N_GROUPS = 4
EXPERTS_PER_GROUP = 8
D_EXPERT = 256

ROW_BLOCK = 256
SUB = 16
MACRO = 64
DECAY_GUARD = 60.0
N_SCAN_CONSTS = 9
SCAN_BATCHES = 2
LANE = 128
HEAD_TILE = 128
VT_ROWS = 80
W_A, W_B, W_C, W_D = 1280, 768, 896, 512
TOK_TILES = D_MODEL // LANE
VMEM_LIMIT = 52 * 1024 * 1024
NEG_INF = float("-inf")


def _bdot(a, b):
    return jnp.dot(a.astype(BF16), b.astype(BF16), preferred_element_type=F32)


def _bdot_nt(a, b):
    return lax.dot_general(a.astype(BF16), b.astype(BF16), (((1,), (1,)), ((), ())), preferred_element_type=F32)


def _bdot_tn(a, b):
    return lax.dot_general(a.astype(BF16), b.astype(BF16), (((0,), (0,)), ((), ())), preferred_element_type=F32)


def _split3(a):
    hi = a.astype(BF16)
    r1 = a - hi.astype(F32)
    mid = r1.astype(BF16)
    lo = (r1 - mid.astype(F32)).astype(BF16)
    return hi, mid, lo


def _dot_f32(a, b):
    ah, am, al = _split3(a)
    bh, bm, bl = _split3(b)
    d = lambda u, v: jnp.dot(u, v, preferred_element_type=F32)
    return d(ah, bh) + (d(ah, bm) + d(am, bh)) + (d(am, bm) + d(ah, bl) + d(al, bh))


def _dot_sel_l(sel, a):
    ah, am, al = _split3(a)
    d = lambda v: jnp.dot(sel, v, preferred_element_type=F32)
    return d(ah) + d(am) + d(al)


def _dot_sel_r(a, sel):
    ah, am, al = _split3(a)
    d = lambda u: jnp.dot(u, sel, preferred_element_type=F32)
    return d(ah) + d(am) + d(al)


def _sigmoid(x):
    return 1.0 / (1.0 + jnp.exp(-x))


def _log_sigmoid(x):
    return jnp.minimum(x, 0.0) - jnp.log1p(jnp.exp(-jnp.abs(x)))


def _logaddexp(a, b):
    amax = jnp.maximum(a, b)
    delta = a - b
    return jnp.where(jnp.isnan(delta), a + b, amax + jnp.log1p(jnp.exp(-jnp.abs(delta))))


def _rms(x, width):
    return x * lax.rsqrt(jnp.sum(x * x, axis=-1, keepdims=True) / width + EPS)


def _cparams(sem, vmem=VMEM_LIMIT):
    return pltpu.CompilerParams(dimension_semantics=sem, vmem_limit_bytes=vmem)


def _mod_kernel(c_ref, w_ref, b_ref, o_ref):
    c = c_ref[...]
    act = c * _sigmoid(c)
    o_ref[0] = _dot_f32(act, w_ref[0]) + b_ref[0]


def _mod_vectors(c8, w_mod, b_mod):
    depth, d, six_d = w_mod.shape
    nj = six_d // d
    return pl.pallas_call(
        _mod_kernel,
        grid=(depth, nj),
        in_specs=[
            pl.BlockSpec((8, d), lambda l, j: (0, 0)),
            pl.BlockSpec((1, d, d), lambda l, j: (l, 0, j)),
            pl.BlockSpec((1, 1, d), lambda l, j: (l, 0, j)),
        ],
        out_specs=pl.BlockSpec((1, 8, d), lambda l, j: (l, 0, j)),
        out_shape=jax.ShapeDtypeStruct((depth, 8, six_d), F32),
        compiler_params=_cparams(("arbitrary", "arbitrary")),
        name="mod_vectors",
    )(c8, w_mod, b_mod.reshape(depth, 1, six_d))


def _mod_row(m_ref, row):
    return m_ref[pl.ds(row, 1), :]


def _inproj_kernel(*refs, first, nblk):
    if first:
        ctx_ref, lat_ref, m_ref, g_ref, w_ref, xo_ref, pa_ref, pb_ref, pc_ref, pd_ref = refs
    else:
        x_ref, y_ref, mprev_ref, m_ref, g_ref, w_ref, xo_ref, pa_ref, pb_ref, pc_ref, pd_ref = refs
    i = pl.program_id(0)
    b = i // nblk
    is_ctx = i % nblk == 0
    row = jnp.where(is_ctx, 4, b)
    d = D_MODEL
    if first:
        x = jnp.where(is_ctx, ctx_ref[0], lat_ref[0])
    else:
        mp = _mod_row(mprev_ref, row)
        x = x_ref[...] + mp[:, 5 * d:6 * d] * _load_token_rows(y_ref)
    xo_ref[...] = x
    m = _mod_row(m_ref, row)
    h = _rms(x, d) * g_ref[...] * (1.0 + m[:, d:2 * d]) + m[:, 0:d]
    p = _bdot(h, w_ref[...])
    pa_ref[...] = p[:, 0:W_A]
    pb_ref[...] = p[:, W_A:W_A + W_B].astype(pb_ref.dtype)
    pc_ref[...] = p[:, W_A + W_B:W_A + W_B + W_C]
    pd_ref[...] = p[:, W_A + W_B + W_C:]


def _inproj(stream, m, g, w, nblk):
    first = len(stream) == 2
    rb, d = ROW_BLOCK, D_MODEL
    row_spec = lambda w_: pl.BlockSpec((rb, w_), lambda i: (i, 0))
    full = lambda a: pl.BlockSpec(a.shape, lambda i: (0,) * a.ndim)
    if first:
        ctx, lat = stream
        t = ctx.shape[0] * nblk * rb
        specs = [pl.BlockSpec((1, rb, d), lambda i: (i // nblk, 0, 0)),
                 pl.BlockSpec((1, rb, d), lambda i: (i // nblk, jnp.maximum(i % nblk - 1, 0), 0))]
    else:
        t = stream[0].shape[0]
        specs = [row_spec(d), pl.BlockSpec((rb * TOK_TILES, LANE), lambda i: (i, 0)), full(stream[2])]
    ins = list(stream) + [m, g, w]
    specs += [full(m), full(g), full(w)]
    outs, ospecs = [jax.ShapeDtypeStruct((t, d), F32)], [row_spec(d)]
    for w_ in (W_A, W_B, W_C, W_D):
        outs.append(jax.ShapeDtypeStruct((t, w_), BF16 if w_ == W_B else F32))
        ospecs.append(row_spec(w_))
    res = pl.pallas_call(
        functools.partial(_inproj_kernel, first=first, nblk=nblk),
        grid=(t // rb,),
        in_specs=specs,
        out_specs=ospecs,
        out_shape=outs,
        compiler_params=_cparams(("arbitrary",)),
        name="in_projection",
    )(*ins)
    return res[0], res[1:]


def _sub_chunk(refs, i, tri, emat, emask_t, reverse):
    q_ref, k_ref, la_ref, v_ref, o_ref, st_ref, r_ref = refs
    hv = v_ref.shape[1]
    row_id = lax.broadcasted_iota(I32, (SUB, 1), 0)
    off = pl.multiple_of(i * SUB, SUB)
    qs = q_ref[pl.ds(off, SUB), :]
    ks = k_ref[pl.ds(off, SUB), :]
    las = la_ref[pl.ds(off, SUB), :]
    vs = v_ref[pl.ds(off, SUB), :]
    cum = _dot_sel_l(tri, las)
    last = cum[0:1, :] if reverse else cum[SUB - 1:SUB, :]
    for j in range(SUB):
        valid = (row_id <= j) if reverse else (row_id >= j)
        dlt = jnp.where(valid, cum - cum[j:j + 1, :], NEG_INF)
        r_ref[j * SUB:(j + 1) * SUB, :] = (qs * ks[j:j + 1, :] * jnp.exp(dlt)).astype(BF16)
    att = jnp.dot(r_ref[...], emat, preferred_element_type=F32)
    o = jnp.zeros((SUB, hv), F32)
    for j in range(SUB):
        o = o + att[j * SUB:(j + 1) * SUB, :] * vs[j:j + 1, :]
    st = st_ref[...]
    o = o + _bdot_nt(qs * jnp.exp(cum), st)
    o_ref[pl.ds(off, SUB), :] = o
    kd = ks * jnp.exp(last - cum)
    st_ref[...] = st * jnp.exp(last) + _bdot_tn(vs, kd) * emask_t


def _macro_step(refs, m, tri, emask_t, hmk, hmv, reverse):
    q_ref, k_ref, la_ref, v_ref, o_ref, st_ref, _ = refs
    hv = v_ref.shape[1]
    n_sub = MACRO // SUB
    off = pl.multiple_of(m * MACRO, MACRO)
    q = q_ref[pl.ds(off, MACRO), :]
    k = k_ref[pl.ds(off, MACRO), :]
    v = v_ref[pl.ds(off, MACRO), :]
    cum = _dot_sel_l(tri, la_ref[pl.ds(off, MACRO), :])
    last = cum[0:1, :] if reverse else cum[MACRO - 1:MACRO, :]
    st = st_ref[...]
    o_inter = _bdot_nt(q * jnp.exp(cum), st)
    st_ref[...] = st * jnp.exp(last) + _bdot_tn(v, k * jnp.exp(last - cum)) * emask_t
    vb = v.astype(BF16)
    row_i = lax.broadcasted_iota(I32, (N_HEADS * SUB, 1), 0) % SUB
    for s in range(n_sub):
        lo = s * SUB
        if reverse:
            k_lo, k_hi = lo, MACRO
            ref = cum[lo + SUB:lo + SUB + 1, :] if s < n_sub - 1 else jnp.zeros_like(last)
        else:
            k_lo, k_hi = 0, lo + SUB
            ref = cum[lo - 1:lo, :] if s > 0 else jnp.zeros_like(last)
        qt = q[lo:lo + SUB, :] * jnp.exp(cum[lo:lo + SUB, :] - ref)
        qs = jnp.concatenate([qt * hmk[h:h + 1, :] for h in range(N_HEADS)], axis=0).astype(BF16)
        kt = (k[k_lo:k_hi, :] * jnp.exp(ref - cum[k_lo:k_hi, :])).astype(BF16)
        att = lax.dot_general(qs, kt, (((1,), (1,)), ((), ())), preferred_element_type=F32)
        col = lax.broadcasted_iota(I32, (1, k_hi - k_lo), 1) + k_lo
        valid = (col >= lo + row_i) if reverse else (col <= lo + row_i)
        att = jnp.where(valid, att, 0.0)
        o_heads = jnp.dot(att.astype(BF16), vb[k_lo:k_hi, :], preferred_element_type=F32)
        o = o_inter[lo:lo + SUB, :]
        for h in range(N_HEADS):
            o = o + o_heads[h * SUB:(h + 1) * SUB, :] * hmv[h:h + 1, :]
        o_ref[pl.ds(off + lo, SUB), :] = o


def _scan_chains(chains, consts):
    tri_f, tri_b, trim_f, trim_b, emat, emask_t, hmk, hmv, sub_sum = consts
    rows = chains[0][0][0].shape[0]
    n_sub, n_macro = rows // SUB, rows // MACRO
    tot = None
    for refs, _ in chains:
        block_tot = jnp.dot(sub_sum, refs[2][...].astype(BF16), preferred_element_type=F32)
        tot = block_tot if tot is None else jnp.minimum(tot, block_tot)
    factorisable = jnp.min(tot) > -DECAY_GUARD

    @pl.when(factorisable)
    def _():
        def body(step, carry):
            for refs, rev in chains:
                _macro_step(refs, n_macro - 1 - step if rev else step, trim_b if rev else trim_f, emask_t, hmk, hmv, rev)
            return carry

        lax.fori_loop(0, n_macro, body, 0, unroll=True)

    @pl.when(jnp.logical_not(factorisable))
    def _():
        def body(step, carry):
            for refs, rev in chains:
                _sub_chunk(refs, n_sub - 1 - step if rev else step, tri_b if rev else tri_f, emat, emask_t, rev)
            return carry

        lax.fori_loop(0, n_sub, body, 0)


def _zero_at_first_block(*state_refs):
    @pl.when(pl.program_id(1) == 0)
    def _():
        for st in state_refs:
            st[...] = jnp.zeros_like(st)


def _scan_batches(batch):
    return SCAN_BATCHES if batch % SCAN_BATCHES == 0 else 1


def _scan_consts(hk, hv, dk, dv):
    tri = lambda n, low: jnp.asarray(np.tril(np.ones((n, n), np.float32)) if low else np.triu(np.ones((n, n), np.float32)), BF16)
    em = _head_match(hk, hv, dk, dv)
    heads = np.arange(N_HEADS)[:, None]
    hmk = (np.arange(hk)[None, :] // dk == heads).astype(np.float32)
    hmv = (np.arange(hv)[None, :] // dv == heads).astype(np.float32)
    sub_sum = (np.arange(ROW_BLOCK)[None, :] // SUB == np.arange(ROW_BLOCK // SUB)[:, None]).astype(np.float32)
    return [tri(SUB, True), tri(SUB, False), tri(MACRO, True), tri(MACRO, False), jnp.asarray(em, BF16),
            jnp.asarray(em.T, F32), jnp.asarray(hmk), jnp.asarray(hmv), jnp.asarray(sub_sum, BF16)]


def _hgrn_prep(q_ref, v_ref, f_ref, lb, qo, ko, lo, vo):
    qr = q_ref[...]
    qo[...] = qr * _sigmoid(qr) * (A_DH ** -0.5)
    vo[...] = v_ref[...]
    z = f_ref[...]
    lo[...] = _logaddexp(jnp.log(lb), jnp.log1p(-lb) + _log_sigmoid(z))
    ko[...] = (1.0 - lb) * _sigmoid(-z)


def _hgrn_kernel(qf_ref, vf_ref, ff_ref, qb_ref, vb_ref, fb_ref, lb_ref, *rest):
    consts, (of_ref, ob_ref), scratch = rest[:N_SCAN_CONSTS], rest[N_SCAN_CONSTS:N_SCAN_CONSTS + 2], rest[N_SCAN_CONSTS + 2:]
    chains = []
    for s in range(qf_ref.shape[0]):
        stf, stb, rf, rb_, q1, k1, l1, v1, q2, k2, l2, v2 = scratch[12 * s:12 * s + 12]
        _zero_at_first_block(stf, stb)
        _hgrn_prep(qf_ref.at[s], vf_ref.at[s], ff_ref.at[s], lb_ref[0:1, :], q1, k1, l1, v1)
        _hgrn_prep(qb_ref.at[s], vb_ref.at[s], fb_ref.at[s], lb_ref[1:2, :], q2, k2, l2, v2)
        chains += [((q1, k1, l1, v1, of_ref.at[s], stf, rf), False), ((q2, k2, l2, v2, ob_ref.at[s], stb, rb_), True)]
    _scan_chains(chains, [c[...] for c in consts])


def _head_match(hk, hv, dk, dv):
    m = (np.arange(hk)[:, None] // dk == np.arange(hv)[None, :] // dv).astype(np.float32)
    return m


def _bwd_block(t, nblk):
    return jnp.where(t == 0, 0, nblk - t)


def _hgrn_scan(pa, lb, batch, nblk):
    rb, w = ROW_BLOCK, D_GROUP
    consts = _scan_consts(w, w, A_DH, A_DH)
    nb = _scan_batches(batch)
    pa3 = pa.reshape(batch, nblk * rb, pa.shape[1])
    fwd = lambda col: pl.BlockSpec((nb, rb, w), lambda b, t: (b, t, col))
    bwd = lambda col: pl.BlockSpec((nb, rb, w), lambda b, t: (b, _bwd_block(t, nblk), col))
    full = lambda a: pl.BlockSpec(a.shape, lambda b, t: (0,) * a.ndim)
    vm = lambda shape, dt=F32: pltpu.VMEM(shape, dt)
    of, ob = pl.pallas_call(
        _hgrn_kernel,
        grid=(batch // nb, nblk),
        in_specs=[fwd(0), fwd(1), fwd(2), bwd(0), bwd(1), bwd(3), full(lb)] + [full(c) for c in consts],
        out_specs=[fwd(0), bwd(0)],
        out_shape=[jax.ShapeDtypeStruct((batch, nblk * rb, w), F32)] * 2,
        scratch_shapes=([vm((w, w)), vm((w, w)), vm((SUB * SUB, w), BF16), vm((SUB * SUB, w), BF16)]
                        + [vm((rb, w))] * 8) * nb,
        compiler_params=_cparams(("arbitrary", "arbitrary")),
        name="hgrn_scan",
    )(pa3, pa3, pa3, pa3, pa3, pa3, lb, *consts)
    return of.reshape(-1, w), ob.reshape(-1, w)


def _gla_prep(q_ref, k_ref, v_ref, z_ref, wg_ref, bg_ref, qo, ko, lo, vo):
    qo[...] = q_ref[...] * (C_DK ** -0.5)
    ko[...] = k_ref[...]
    vo[...] = v_ref[...]
    zl = _dot_f32(z_ref[...], wg_ref[...]) + bg_ref[...]
    lo[...] = _log_sigmoid(zl) / C_GATE_NORM


def _gla_kernel(qf_ref, kf_ref, vf_ref, zf_ref, qb_ref, kb_ref, vb_ref, zb_ref, wgf_ref, bgf_ref, wgb_ref, bgb_ref, *rest):
    consts, (of_ref, ob_ref), scratch = rest[:N_SCAN_CONSTS], rest[N_SCAN_CONSTS:N_SCAN_CONSTS + 2], rest[N_SCAN_CONSTS + 2:]
    chains = []
    for s in range(qf_ref.shape[0]):
        stf, stb, rf, rb_, q1, k1, l1, v1, q2, k2, l2, v2 = scratch[12 * s:12 * s + 12]
        _zero_at_first_block(stf, stb)
        _gla_prep(qf_ref.at[s], kf_ref.at[s], vf_ref.at[s], zf_ref.at[s], wgf_ref, bgf_ref, q1, k1, l1, v1)
        _gla_prep(qb_ref.at[s], kb_ref.at[s], vb_ref.at[s], zb_ref.at[s], wgb_ref, bgb_ref, q2, k2, l2, v2)
        chains += [((q1, k1, l1, v1, of_ref.at[s], stf, rf), False), ((q2, k2, l2, v2, ob_ref.at[s], stb, rb_), True)]
    _scan_chains(chains, [c[...] for c in consts])


def _gla_scan(pc, wg_f, bg_f, wg_b, bg_b, batch, nblk):
    rb = ROW_BLOCK
    hk, hv = N_HEADS * C_DK, N_HEADS * C_DV
    consts = _scan_consts(hk, hv, C_DK, C_DV)
    wgf = jnp.zeros((LANE, hk), F32).at[0:C_GATE_RANK].set(wg_f)
    wgb = jnp.zeros((LANE, hk), F32).at[C_GATE_RANK:2 * C_GATE_RANK].set(wg_b)
    bgf, bgb = bg_f.reshape(1, hk), bg_b.reshape(1, hk)
    nb = _scan_batches(batch)
    pc3 = pc.reshape(batch, nblk * rb, pc.shape[1])
    fwd = lambda w, col: pl.BlockSpec((nb, rb, w), lambda b, t: (b, t, col))
    bwd = lambda w, col: pl.BlockSpec((nb, rb, w), lambda b, t: (b, _bwd_block(t, nblk), col))
    full = lambda a: pl.BlockSpec(a.shape, lambda b, t: (0,) * a.ndim)
    vm = lambda shape, dt=F32: pltpu.VMEM(shape, dt)
    of, ob = pl.pallas_call(
        _gla_kernel,
        grid=(batch // nb, nblk),
        in_specs=[fwd(hk, 0), fwd(hk, 1), fwd(hv, 1), fwd(LANE, 6), bwd(hk, 0), bwd(hk, 1), bwd(hv, 1), bwd(LANE, 6),
                  full(wgf), full(bgf), full(wgb), full(bgb)] + [full(c) for c in consts],
        out_specs=[fwd(hv, 0), bwd(hv, 0)],
        out_shape=[jax.ShapeDtypeStruct((batch, nblk * rb, hv), F32)] * 2,
        scratch_shapes=([vm((hv, hk)), vm((hv, hk)), vm((SUB * SUB, hk), BF16), vm((SUB * SUB, hk), BF16)]
                        + [vm((rb, hk)), vm((rb, hk)), vm((rb, hk)), vm((rb, hv))] * 2) * nb,
        compiler_params=_cparams(("arbitrary", "arbitrary")),
        name="gla_scan",
    )(pc3, pc3, pc3, pc3, pc3, pc3, pc3, pc3, wgf, bgf, wgb, bgb, *consts)
    return of.reshape(-1, hv), ob.reshape(-1, hv)


NA_QROWS = ROW_BLOCK // GRID_W
NA_WROWS = WIN_H + NA_QROWS


def _na_kernel(q_ref, k_ref, v_ref, bias_ref, hm_ref, o_ref, *, j0, rows, ctx):
    j = pl.program_id(1) + j0
    q = q_ref[...] * (B_DH ** -0.5)
    hm = hm_ref[...]
    kc = k_ref[0:ctx, :]
    vc = v_ref[0:ctx, :]
    nt = lambda a, b: lax.dot_general(a, b, (((1,), (1,)), ((), ())), preferred_element_type=F32)

    @pl.when(j == 0)
    def _():
        acc = jnp.zeros(q.shape, F32)
        for h in range(N_HEADS):
            mh = hm[h:h + 1, :]
            s = nt(q * mh.astype(BF16), kc)
            p = jnp.exp(s - jnp.max(s, axis=-1, keepdims=True))
            inv = 1.0 / jnp.sum(p, axis=-1, keepdims=True)
            acc = acc + jnp.dot(p.astype(BF16), vc, preferred_element_type=F32) * (mh * inv)
        o_ref[...] = acc.astype(o_ref.dtype)

    @pl.when(j > 0)
    def _():
        r0 = (j - 1) * NA_QROWS
        start = jnp.clip(r0 - WIN_H // 2, 0, rows - NA_WROWS)
        off = pl.multiple_of(ctx + start * GRID_W, GRID_W)
        kw = k_ref[pl.ds(off, NA_WROWS * GRID_W), :]
        vw = v_ref[pl.ds(off, NA_WROWS * GRID_W), :]
        acc = jnp.zeros(q.shape, F32)
        for h in range(N_HEADS):
            mh = hm[h:h + 1, :]
            qh = q * mh.astype(BF16)
            sw = nt(qh, kw) + bias_ref[0, h]
            sc = nt(qh, kc)
            m = jnp.maximum(jnp.max(sw, axis=-1, keepdims=True), jnp.max(sc, axis=-1, keepdims=True))
            pw = jnp.exp(sw - m)
            pc_ = jnp.exp(sc - m)
            inv = 1.0 / (jnp.sum(pw, axis=-1, keepdims=True) + jnp.sum(pc_, axis=-1, keepdims=True))
            o = (jnp.dot(pw.astype(BF16), vw, preferred_element_type=F32)
                 + jnp.dot(pc_.astype(BF16), vc, preferred_element_type=F32))
            acc = acc + o * (mh * inv)
        o_ref[...] = acc.astype(o_ref.dtype)


def _na_bias_table(rpb, rows):
    kh = WIN_H
    cidx = np.arange(GRID_W)
    c_start = np.clip(cidx - WIN_W // 2, 0, GRID_W - WIN_W)
    col_in = (cidx[None] >= c_start[:, None]) & (cidx[None] < c_start[:, None] + WIN_W)
    dc = np.clip(cidx[None] - cidx[:, None], -(WIN_W - 1), WIN_W - 1) + (WIN_W - 1)
    sel = (dc[None] == np.arange(2 * WIN_W - 1)[:, None, None]).astype(np.float32)
    by_col = jnp.einsum("hrc,cqw->hrqw", rpb.astype(F32), jnp.asarray(sel), precision=lax.Precision.HIGHEST)
    sel_row = np.zeros((3, NA_QROWS, NA_WROWS, 2 * kh - 1), np.float32)
    for p, r0 in enumerate((0, NA_QROWS, rows - NA_QROWS)):
        ws = int(np.clip(r0 - kh // 2, 0, rows - NA_WROWS))
        for a in range(NA_QROWS):
            r = r0 + a
            s = int(np.clip(r - kh // 2, 0, rows - kh))
            for jj in range(NA_WROWS):
                if s <= ws + jj < s + kh:
                    sel_row[p, a, jj, ws + jj - r + kh - 1] = 1.0
    bias = jnp.einsum("pajr,hrqw->phaqjw", jnp.asarray(sel_row), by_col, precision=lax.Precision.HIGHEST)
    visible = (sel_row.sum(axis=-1) > 0)[:, None, :, None, :, None] & col_in[None, None, None, :, None, :]
    bias = jnp.where(jnp.asarray(visible), bias, NEG_INF)
    return bias.reshape(3, N_HEADS, NA_QROWS * GRID_W, NA_WROWS * GRID_W)


def _head_masks(width, dh):
    return jnp.asarray((np.arange(width)[None, :] // dh == np.arange(N_HEADS)[:, None]).astype(np.float32))


def _neighbourhood_attention(pb, rpb, batch, s_len, ctx, keep_ctx):
    rows = (s_len - ctx) // GRID_W
    assert rows >= NA_WROWS and rows % NA_QROWS == 0 and ctx == ROW_BLOCK
    rb = ROW_BLOCK
    j0 = 0 if keep_ctx else 1
    per_b = s_len // rb
    bias = _na_bias_table(rpb, rows)
    hm = _head_masks(D_GROUP, B_DH)

    def bias_idx(b, jj):
        r0 = (jj + j0 - 1) * NA_QROWS
        return (jnp.where(r0 <= 0, 0, jnp.where(r0 == rows - NA_QROWS, 2, 1)), 0, 0, 0)

    return pl.pallas_call(
        functools.partial(_na_kernel, j0=j0, rows=rows, ctx=ctx),
        grid=(batch, per_b - j0),
        in_specs=[
            pl.BlockSpec((rb, D_GROUP), lambda b, jj: (b * per_b + jj + j0, 0)),
            pl.BlockSpec((s_len, D_GROUP), lambda b, jj: (b, 1)),
            pl.BlockSpec((s_len, D_GROUP), lambda b, jj: (b, 2)),
            pl.BlockSpec((1, N_HEADS, rb, NA_WROWS * GRID_W), bias_idx),
            pl.BlockSpec(hm.shape, lambda b, jj: (0, 0)),
        ],
        out_specs=pl.BlockSpec((rb, D_GROUP), lambda b, jj: (b * (per_b - j0) + jj, 0)),
        out_shape=jax.ShapeDtypeStruct((batch * (per_b - j0) * rb, D_GROUP), BF16),
        compiler_params=_cparams(("arbitrary", "arbitrary")),
        name="neighbourhood_attention",
    )(pb, pb, pb, bias, hm)


def _mla_prep_kernel(pd_ref, gq_ref, gkv_ref, wq1_ref, wq2_ref, wk_ref, wvt_ref, ones_ref, cq_ref, sq_ref, tk_ref,
                     place_ref, q_ref, k_ref, vt_ref):
    pd = pd_ref[...]
    cq = pd[:, 0:256]
    ckv = pd[:, 256:384]
    kr = pd[:, 384:512]
    qn = _rms(cq, D_Q_RANK) * gq_ref[...]
    q = _bdot(qn, wq1_ref[...]) * cq_ref[...] + _bdot(qn, wq2_ref[...]) * sq_ref[...]
    q_ref[...] = q.astype(BF16)
    kvn = (_rms(ckv, D_KV_RANK) * gkv_ref[...]).astype(BF16)
    k = jnp.dot(kvn, wk_ref[...], preferred_element_type=F32) + _dot_sel_r(kr * tk_ref[...], place_ref[...])
    k_ref[...] = k.astype(BF16)
    vt = lax.dot_general(wvt_ref[...], kvn, (((1,), (1,)), ((), ())), preferred_element_type=F32) + ones_ref[...]
    vt_ref[0] = vt.astype(BF16)


def _rope_swap_perm():
    f = ROPE_FREQS
    return np.concatenate([np.arange(f, 2 * f), np.arange(0, f), np.arange(3 * f, 4 * f), np.arange(2 * f, 3 * f)])


def _mla_tables(n, ctx):
    t = np.arange(n)
    inv_freq = ROPE_BASE ** (-np.arange(ROPE_FREQS, dtype=np.float32) / ROPE_FREQS)
    ang_r = (t // GRID_W).astype(np.float32)[:, None] * inv_freq
    ang_c = (t % GRID_W).astype(np.float32)[:, None] * inv_freq
    cos32 = np.concatenate([np.cos(ang_r), np.cos(ang_r), np.cos(ang_c), np.cos(ang_c)], axis=1)
    sin32 = np.concatenate([-np.sin(ang_r), np.sin(ang_r), -np.sin(ang_c), np.sin(ang_c)], axis=1)
    cos32 = np.concatenate([np.ones((ctx, D_ROPE), np.float32), cos32.astype(np.float32)], axis=0)
    sin32 = np.concatenate([np.zeros((ctx, D_ROPE), np.float32), sin32.astype(np.float32)], axis=0)
    s_len = n + ctx
    cq = np.zeros((s_len, N_HEADS, HEAD_TILE), np.float32)
    sq = np.zeros((s_len, N_HEADS, HEAD_TILE), np.float32)
    cq[:, :, 0:D_NOPE] = MLA_SCALE
    cq[:, :, D_NOPE:D_NOPE + D_ROPE] = cos32[:, None, :] * MLA_SCALE
    sq[:, :, D_NOPE:D_NOPE + D_ROPE] = sin32[:, None, :] * MLA_SCALE
    tk = np.zeros((s_len, LANE), np.float32)
    tk[:, 0:D_ROPE] = cos32
    tk[:, D_ROPE:2 * D_ROPE] = sin32
    place = np.zeros((LANE, N_HEADS * HEAD_TILE), np.float32)
    for h in range(N_HEADS):
        for l in range(D_ROPE):
            place[l, h * HEAD_TILE + D_NOPE + l] = 1.0
            place[D_ROPE + l, h * HEAD_TILE + D_NOPE + l] = 1.0
    return (jnp.asarray(cq.reshape(s_len, -1)), jnp.asarray(sq.reshape(s_len, -1)), jnp.asarray(tk),
            jnp.asarray(place, BF16))


def _mla_weights(q_norm_g, w_uq, kv_norm_g, w_ukv):
    perm = _rope_swap_perm()
    wq = w_uq.reshape(D_Q_RANK, N_HEADS, D_NOPE + D_ROPE)
    wq1 = jnp.zeros((256, N_HEADS, HEAD_TILE), F32).at[0:D_Q_RANK, :, 0:D_NOPE + D_ROPE].set(wq)
    wq2 = jnp.zeros((256, N_HEADS, HEAD_TILE), F32).at[0:D_Q_RANK, :, D_NOPE:D_NOPE + D_ROPE].set(
        wq[:, :, D_NOPE:][:, :, perm])
    wkv = w_ukv.reshape(D_KV_RANK, N_HEADS, D_NOPE + D_V)
    wk = jnp.zeros((D_KV_RANK, N_HEADS, HEAD_TILE), F32).at[:, :, 0:D_NOPE].set(wkv[:, :, 0:D_NOPE])
    wvt = jnp.zeros((N_HEADS, VT_ROWS, D_KV_RANK), F32).at[:, 0:D_V, :].set(jnp.transpose(wkv[:, :, D_NOPE:], (1, 2, 0)))
    wvt = wvt.reshape(N_HEADS * VT_ROWS, D_KV_RANK)
    ones_rows = jnp.asarray((np.arange(N_HEADS * VT_ROWS) % VT_ROWS >= D_V).astype(np.float32)).reshape(-1, 1)
    hw = N_HEADS * HEAD_TILE
    gq = jnp.zeros((1, 256), F32).at[0, 0:D_Q_RANK].set(q_norm_g)
    return (gq, kv_norm_g.reshape(1, D_KV_RANK), wq1.reshape(256, hw).astype(BF16), wq2.reshape(256, hw).astype(BF16),
            wk.reshape(D_KV_RANK, hw).astype(BF16), wvt.astype(BF16), ones_rows)


def _mla_prep(pd, weights, tables, nblk):
    t_all = pd.shape[0]
    rb = ROW_BLOCK
    gq, gkv, wq1, wq2, wk, wvt, ones_rows = weights
    cq, sq, tk, place = tables
    hw = N_HEADS * HEAD_TILE
    hv = N_HEADS * VT_ROWS
    row = lambda w: pl.BlockSpec((rb, w), lambda i: (i, 0))
    pos = lambda w: pl.BlockSpec((rb, w), lambda i: (i % nblk, 0))
    full = lambda a: pl.BlockSpec(a.shape, lambda i: (0,) * a.ndim)
    return pl.pallas_call(
        _mla_prep_kernel,
        grid=(t_all // rb,),
        in_specs=[row(W_D), full(gq), full(gkv), full(wq1), full(wq2), full(wk), full(wvt), full(ones_rows),
                  pos(hw), pos(hw), pos(LANE), full(place)],
        out_specs=[row(hw), row(hw), pl.BlockSpec((1, hv, rb), lambda i: (i // nblk, 0, i % nblk))],
        out_shape=[jax.ShapeDtypeStruct((t_all, hw), BF16), jax.ShapeDtypeStruct((t_all, hw), BF16),
                   jax.ShapeDtypeStruct((t_all // (nblk * rb), hv, nblk * rb), BF16)],
        compiler_params=_cparams(("arbitrary",)),
        name="mla_prep",
    )(pd, gq, gkv, wq1, wq2, wk, wvt, ones_rows, cq, sq, tk, place)


MLA_HEADS_PER_STEP = 2
MLA_KEY_CHUNK = 256


def _mla_attn_kernel(q_ref, k_ref, vt_ref, o_ref, s_scr, *, j0, ctx):
    j = pl.program_id(2) + j0
    rb = q_ref.shape[0]

    def heads(n_keys):
        hs = range(MLA_HEADS_PER_STEP)
        chunks = [slice(c * MLA_KEY_CHUNK, (c + 1) * MLA_KEY_CHUNK) for c in range(n_keys // MLA_KEY_CHUNK)]
        lanes = [slice(h * HEAD_TILE, (h + 1) * HEAD_TILE) for h in hs]
        qs = [q_ref[:, lanes[h]] for h in hs]
        m = [jnp.full((1, rb), NEG_INF, F32) for _ in hs]
        for keys in chunks:
            for h in hs:
                st = lax.dot_general(k_ref[keys, lanes[h]], qs[h], (((1,), (1,)), ((), ())),
                                     preferred_element_type=F32)
                s_scr[h, keys, :] = st
                m[h] = jnp.maximum(m[h], jnp.max(st, axis=0, keepdims=True))
        acc = [jnp.zeros((VT_ROWS, rb), F32) for _ in hs]
        for keys in chunks:
            for h in hs:
                pt = jnp.exp(s_scr[h, keys, :] - m[h]).astype(BF16)
                acc[h] = acc[h] + jnp.dot(vt_ref[0, h * VT_ROWS:(h + 1) * VT_ROWS, keys], pt,
                                          preferred_element_type=F32)
        for h in hs:
            o_ref[0, h * D_V:(h + 1) * D_V, :] = (acc[h][0:D_V, :] * (1.0 / acc[h][D_V:D_V + 1, :])).astype(o_ref.dtype)

    @pl.when(j == 0)
    def _():
        heads(ctx)

    @pl.when(j > 0)
    def _():
        heads(k_ref.shape[0])


def _mla_attention(q, k, vt, batch, s_len, ctx, keep_ctx):
    rb = ROW_BLOCK
    nblk = s_len // rb
    j0 = 0 if keep_ctx else 1
    hps = MLA_HEADS_PER_STEP
    return pl.pallas_call(
        functools.partial(_mla_attn_kernel, j0=j0, ctx=ctx),
        grid=(batch, N_HEADS // hps, nblk - j0),
        in_specs=[
            pl.BlockSpec((rb, hps * HEAD_TILE), lambda b, h, jj: (b * nblk + jj + j0, h)),
            pl.BlockSpec((s_len, hps * HEAD_TILE), lambda b, h, jj: (b, h)),
            pl.BlockSpec((1, hps * VT_ROWS, s_len), lambda b, h, jj: (b, h, 0)),
        ],
        out_specs=pl.BlockSpec((1, hps * D_V, rb), lambda b, h, jj: (b * (nblk - j0) + jj, h, 0)),
        out_shape=jax.ShapeDtypeStruct((batch * (nblk - j0), N_HEADS * D_V, rb), BF16),
        scratch_shapes=[pltpu.VMEM((hps, s_len, rb), F32)],
        compiler_params=_cparams(("arbitrary", "arbitrary", "arbitrary")),
        name="mla_attention",
    )(q, k, vt)


def _outproj_kernel(oaf_ref, oab_ref, ga_ref, ob_ref, ogf_ref, ogb_ref, gc_ref, od_ref, x_ref, m_ref, gha_ref, ghc_ref,
                    g2_ref, wa_ref, wb_ref, wc_ref, wd_ref, em_ref, x1_ref, h2_ref, *, j0, nblk_out):
    i = pl.program_id(0)
    b = i // nblk_out
    row = jnp.where(i % nblk_out + j0 == 0, 4, b)
    d = D_MODEL
    em = em_ref[...]

    def readout(o, g_norm, gate, dh):
        ms = _dot_sel_r(o * o, em) / dh
        return o * lax.rsqrt(ms + EPS) * g_norm * (gate * _sigmoid(gate))

    a = readout(oaf_ref[...] + oab_ref[...], gha_ref[...], ga_ref[...], A_DH)
    c = readout(ogf_ref[...] + ogb_ref[...], ghc_ref[...], gc_ref[...], C_DV)
    mix = (_bdot(a, wa_ref[...]) + jnp.dot(ob_ref[...], wb_ref[...], preferred_element_type=F32)
           + _bdot(c, wc_ref[...])
           + lax.dot_general(od_ref[0], wd_ref[...], (((0,), (0,)), ((), ())), preferred_element_type=F32))
    m = _mod_row(m_ref, row)
    x1 = x_ref[...] + m[:, 2 * d:3 * d] * mix
    x1_ref[...] = x1
    _store_token_rows(h2_ref, _rms(x1, d) * g2_ref[...] * (1.0 + m[:, 4 * d:5 * d]) + m[:, 3 * d:4 * d])


def _outproj(oaf, oab, pa, ob, ogf, ogb, pc, od, x, m, gha, ghc, g2, w_out, batch, nblk, keep_ctx):
    rb, d = ROW_BLOCK, D_MODEL
    j0 = 0 if keep_ctx else 1
    nblk_out = nblk - j0
    t_out = batch * nblk_out * rb
    wa = w_out[0:256].astype(BF16)
    wb = w_out[256:512].astype(BF16)
    wc = w_out[512:768].astype(BF16)
    wd = w_out[768:1024].astype(BF16)
    em = jnp.asarray(_head_match(D_GROUP, D_GROUP, 64, 64), BF16)
    src = lambda i: (i // nblk_out) * nblk + i % nblk_out + j0
    row_in = lambda w, col=0: pl.BlockSpec((rb, w), lambda i: (src(i), col))
    row_out = lambda w: pl.BlockSpec((rb, w), lambda i: (i, 0))
    full = lambda a_: pl.BlockSpec(a_.shape, lambda i: (0,) * a_.ndim)
    return pl.pallas_call(
        functools.partial(_outproj_kernel, j0=j0, nblk_out=nblk_out),
        grid=(t_out // rb,),
        in_specs=[row_in(256), row_in(256), row_in(256, 4), row_out(256), row_in(256), row_in(256), row_in(256, 2),
                  pl.BlockSpec((1, N_HEADS * D_V, rb), lambda i: (i, 0, 0)), row_in(d), full(m), full(gha), full(ghc), full(g2),
                  full(wa), full(wb), full(wc), full(wd), full(em)],
        out_specs=[row_out(d), pl.BlockSpec((rb * TOK_TILES, LANE), lambda i: (i, 0))],
        out_shape=[jax.ShapeDtypeStruct((t_out, d), F32), jax.ShapeDtypeStruct((t_out * TOK_TILES, LANE), F32)],
        compiler_params=_cparams(("arbitrary",)),
        name="out_projection",
    )(oaf, oab, pa, ob, ogf, ogb, pc, od, x, m, gha, ghc, g2, wa, wb, wc, wd, em)


def _load_token_rows(ref):
    rows = ref.shape[0] // TOK_TILES
    return jnp.concatenate([ref[pl.ds(k, rows, stride=TOK_TILES), :] for k in range(TOK_TILES)], axis=1)


def _store_token_rows(ref, val):
    rows = val.shape[0]
    for k in range(TOK_TILES):
        ref[pl.ds(k, rows, stride=TOK_TILES), :] = val[:, k * LANE:(k + 1) * LANE]


def _router_logits(h, wr_ref, br_ref):
    lg = _dot_f32(h, wr_ref[...]) + br_ref[...]
    lane = lax.broadcasted_iota(I32, lg.shape, 1).astype(F32)
    return lg, lane


def _top_group(lg, lane):
    gl = jnp.where(lane < N_GROUPS, lg, NEG_INF)
    gmax = jnp.max(gl, axis=-1, keepdims=True)
    gsel = jnp.min(jnp.where(gl == gmax, lane, float(LANE)), axis=-1, keepdims=True)
    p_group = 1.0 / jnp.sum(jnp.exp(gl - gmax), axis=-1, keepdims=True)
    return gsel, p_group


def _expert_gates(lg, lane, lo, p_group):
    big = float(LANE)
    el = jnp.where((lane >= lo) & (lane < lo + EXPERTS_PER_GROUP), lg, NEG_INF)
    m1 = jnp.max(el, axis=-1, keepdims=True)
    i1 = jnp.min(jnp.where(el == m1, lane, big), axis=-1, keepdims=True)
    el2 = jnp.where(lane == i1, NEG_INF, el)
    m2 = jnp.max(el2, axis=-1, keepdims=True)
    i2 = jnp.min(jnp.where(el2 == m2, lane, big), axis=-1, keepdims=True)
    t = jnp.exp(m2 - m1)
    w1 = p_group / (1.0 + t)
    w2 = p_group * t / (1.0 + t)
    return jnp.where(lane == i1 - lo, w1, jnp.where(lane == i2 - lo, w2, 0.0))


def _router_kernel(h_ref, wr_ref, br_ref, tril_ref, meta_ref, cnt_ref, carry):
    @pl.when(pl.program_id(0) == 0)
    def _():
        carry[...] = jnp.zeros_like(carry)

    lg, lane = _router_logits(_load_token_rows(h_ref), wr_ref, br_ref)
    gsel, _ = _top_group(lg, lane)
    onehot = jnp.where(lane == gsel, 1.0, 0.0)
    incl = jnp.dot(tril_ref[...], onehot.astype(BF16), preferred_element_type=F32)
    rank = jnp.sum(onehot * (incl - 1.0 + carry[...]), axis=-1, keepdims=True)
    carry[...] = carry[...] + jnp.sum(onehot, axis=0, keepdims=True)
    meta_ref[...] = jnp.where(lane == 0, gsel, jnp.where(lane == 1, rank, 0.0)).astype(I32)
    cnt_ref[...] = carry[...].astype(I32)


def _router_weights(w_rg, b_rg, w_re, b_re):
    d = w_rg.shape[0]
    ne = N_GROUPS * EXPERTS_PER_GROUP
    wr = jnp.zeros((d, LANE), F32).at[:, 0:N_GROUPS].set(w_rg).at[:, N_GROUPS:N_GROUPS + ne].set(w_re)
    br = jnp.zeros((1, LANE), F32).at[0, 0:N_GROUPS].set(b_rg).at[0, N_GROUPS:N_GROUPS + ne].set(b_re)
    return wr, br


def _router(h2t, wr, br):
    t = h2t.shape[0] // TOK_TILES
    rb = ROW_BLOCK
    tril = jnp.asarray(np.tril(np.ones((rb, rb), np.float32)), BF16)
    full = lambda a: pl.BlockSpec(a.shape, lambda i: (0,) * a.ndim)
    return pl.pallas_call(
        _router_kernel,
        grid=(t // rb,),
        in_specs=[pl.BlockSpec((rb * TOK_TILES, LANE), lambda i: (i, 0)), full(wr), full(br), full(tril)],
        out_specs=[pl.BlockSpec((rb, LANE), lambda i: (i, 0)), pl.BlockSpec((1, LANE), lambda i: (0, 0))],
        out_shape=[jax.ShapeDtypeStruct((t, LANE), I32), jax.ShapeDtypeStruct((1, LANE), I32)],
        scratch_shapes=[pltpu.VMEM((1, LANE), F32)],
        compiler_params=_cparams(("arbitrary",)),
        name="moe_router",
    )(h2t, wr, br, tril)


def _invert_kernel(dest_ref, inv_ref, *, n_tok):
    def spare(s, c):
        inv_ref[s] = n_tok + (s & (ROW_BLOCK - 1))
        return c

    lax.fori_loop(0, inv_ref.shape[0], spare, 0, unroll=8)

    def put(t, c):
        inv_ref[dest_ref[t]] = t
        return c

    lax.fori_loop(0, n_tok, put, 0, unroll=8)


def _invert(dest, n_slots):
    smem = pl.BlockSpec(memory_space=pltpu.SMEM)
    return pl.pallas_call(
        functools.partial(_invert_kernel, n_tok=dest.shape[0]), in_specs=[smem], out_specs=smem,
        out_shape=jax.ShapeDtypeStruct((n_slots,), I32), name="moe_invert",
    )(dest)


def _token_copy(src_ref, dst_ref, s, d, sem):
    s8 = pl.multiple_of(s * TOK_TILES, TOK_TILES)
    d8 = pl.multiple_of(d * TOK_TILES, TOK_TILES)
    return pltpu.make_async_copy(src_ref.at[pl.ds(s8, TOK_TILES), :], dst_ref.at[pl.ds(d8, TOK_TILES), :], sem)


def _experts_kernel(bg_ref, inv_ref, h_ref, wr_ref, br_ref, wgu_ref, wdn_ref, y_ref, xbuf, ybuf, gsem, ssem, *, n_tok):
    i = pl.program_id(0)
    n_steps = pl.num_programs(0)
    rb, d = ROW_BLOCK, D_MODEL
    slot = i % 2
    other = 1 - slot
    nxt = jnp.minimum(i + 1, n_steps - 1)
    prev = jnp.maximum(i - 1, 0)
    block_rows = rb * TOK_TILES

    def gather_start(blk, sl, r):
        tok = jnp.minimum(inv_ref[blk * rb + r], n_tok - 1)
        _token_copy(h_ref, xbuf.at[sl], tok, r, gsem.at[sl]).start()

    def scatter_start(blk, sl, r, to_spare):
        dst = jnp.where(to_spare, n_tok + r, inv_ref[blk * rb + r])
        _token_copy(ybuf.at[sl], y_ref, r, dst, ssem.at[sl]).start()

    def gather_wait(sl):
        pltpu.make_async_copy(h_ref.at[pl.ds(0, block_rows), :], xbuf.at[sl], gsem.at[sl]).wait()

    def scatter_wait(sl):
        pltpu.make_async_copy(ybuf.at[sl], y_ref.at[pl.ds(0, block_rows), :], ssem.at[sl]).wait()

    @pl.when(i == 0)
    def _():
        ybuf[...] = jnp.zeros_like(ybuf)

        def one(r, c):
            gather_start(0, 0, r)
            return c
        lax.fori_loop(0, rb, one, 0, unroll=8)

    gather_wait(slot)

    @pl.when(i >= 1)
    def _():
        scatter_wait(slot)

    xf = _load_token_rows(xbuf.at[slot])
    lg, lane = _router_logits(xf, wr_ref, br_ref)
    _, p_group = _top_group(lg, lane)
    lo = (N_GROUPS + bg_ref[i] * EXPERTS_PER_GROUP).astype(F32)
    gates = _expert_gates(lg, lane, lo, p_group)
    x = xf.astype(BF16)
    per_expert = rb // EXPERTS_PER_GROUP
    hidden = []
    for e in range(EXPERTS_PER_GROUP):
        gu = jnp.dot(x, wgu_ref[0, e], preferred_element_type=F32)
        g = gu[:, 0:D_EXPERT]
        hidden.append((g * _sigmoid(g) * gu[:, D_EXPERT:] * gates[:, e:e + 1]).astype(BF16))
        for r in range(e * per_expert, (e + 1) * per_expert):
            gather_start(nxt, other, r)
            scatter_start(prev, other, r, i == 0)
    y = jnp.dot(jnp.concatenate(hidden, axis=1), wdn_ref[0], preferred_element_type=F32)
    _store_token_rows(ybuf.at[slot], y)

    @pl.when(i == n_steps - 1)
    def _():
        scatter_wait(other)

        def one(r, c):
            scatter_start(i, slot, r, False)
            return c
        lax.fori_loop(0, rb, one, 0, unroll=8)
        scatter_wait(slot)
        gather_wait(other)


def _experts(block_group, inv, h2t, wr, br, w_gu, w_dn):
    n_blocks = block_group.shape[0]
    rb, d = ROW_BLOCK, D_MODEL
    n_tok = h2t.shape[0] // TOK_TILES
    wgu = w_gu.reshape(N_GROUPS, EXPERTS_PER_GROUP, d, 2 * D_EXPERT)
    wdn = w_dn.reshape(N_GROUPS, EXPERTS_PER_GROUP * D_EXPERT, d)
    any_spec = pl.BlockSpec(memory_space=pl.ANY)
    return pl.pallas_call(
        functools.partial(_experts_kernel, n_tok=n_tok),
        grid_spec=pltpu.PrefetchScalarGridSpec(
            num_scalar_prefetch=2, grid=(n_blocks,),
            in_specs=[
                any_spec,
                pl.BlockSpec(wr.shape, lambda i, bg, inv_: (0, 0)),
                pl.BlockSpec(br.shape, lambda i, bg, inv_: (0, 0)),
                pl.BlockSpec((1, EXPERTS_PER_GROUP, d, 2 * D_EXPERT), lambda i, bg, inv_: (bg[i], 0, 0, 0)),
                pl.BlockSpec((1, EXPERTS_PER_GROUP * D_EXPERT, d), lambda i, bg, inv_: (bg[i], 0, 0)),
            ],
            out_specs=any_spec,
            scratch_shapes=[pltpu.VMEM((2, rb * TOK_TILES, LANE), F32), pltpu.VMEM((2, rb * TOK_TILES, LANE), F32),
                            pltpu.SemaphoreType.DMA((2,)), pltpu.SemaphoreType.DMA((2,))]),
        out_shape=jax.ShapeDtypeStruct(((n_tok + rb) * TOK_TILES, LANE), F32),
        compiler_params=pltpu.CompilerParams(dimension_semantics=("arbitrary",), vmem_limit_bytes=VMEM_LIMIT,
                                             has_side_effects=True),
        name="moe_experts",
    )(block_group, inv, h2t, wr, br, wgu, wdn)


def _moe(h2t, w_rg, b_rg, w_re, b_re, w_gu_bf16, w_dn_bf16):
    t = h2t.shape[0] // TOK_TILES
    rb = ROW_BLOCK
    wr, br = _router_weights(w_rg, b_rg, w_re, b_re)
    meta, counts = _router(h2t, wr, br)
    group, rank = meta[:, 0], meta[:, 1]
    cnt = counts[0, 0:N_GROUPS]
    padded = (cnt + rb - 1) // rb * rb
    seg_end = jnp.cumsum(padded)
    seg_start = seg_end - padded
    dest = seg_start[group] + rank
    n_blocks = t // rb + N_GROUPS
    block_start = jnp.arange(n_blocks, dtype=I32) * rb
    block_group = jnp.minimum(jnp.sum((block_start[:, None] >= seg_end[None, :]).astype(I32), axis=1), N_GROUPS - 1)
    inv = _invert(dest, n_blocks * rb)
    return _experts(block_group, inv, h2t, wr, br, w_gu_bf16, w_dn_bf16)


def _final_kernel(x_ref, y_ref, m_ref, g_ref, o_ref, *, nblk):
    b = pl.program_id(0) // nblk
    d = D_MODEL
    m = _mod_row(m_ref, b)
    x = x_ref[...] + m[:, 5 * d:6 * d] * _load_token_rows(y_ref)
    o_ref[...] = _rms(x, d) * g_ref[...]


def _final(x1, y, m, g, nblk):
    t, d = x1.shape
    rb = ROW_BLOCK
    row = pl.BlockSpec((rb, d), lambda i: (i, 0))
    tok = pl.BlockSpec((rb * TOK_TILES, LANE), lambda i: (i, 0))
    full = lambda a: pl.BlockSpec(a.shape, lambda i: (0,) * a.ndim)
    return pl.pallas_call(
        functools.partial(_final_kernel, nblk=nblk),
        grid=(t // rb,),
        in_specs=[row, tok, full(m), full(g)],
        out_specs=row,
        out_shape=jax.ShapeDtypeStruct((t, d), F32),
        compiler_params=_cparams(("arbitrary",)),
        name="final_norm",
    )(x1, y, m, g)


def _inproj_weight(w_in):
    d = w_in.shape[0]
    z = lambda n: jnp.zeros((d, n), w_in.dtype)
    perm = _rope_swap_perm()
    kr = w_in[:, 3168:3200]
    cols = [w_in[:, 0:2048], w_in[:, 2048:2848], z(W_C - 800),
            w_in[:, 2848:3040], z(256 - D_Q_RANK), w_in[:, 3040:3168], kr, kr[:, perm], z(LANE - 2 * D_ROPE)]
    return jnp.concatenate(cols, axis=1).astype(BF16)


def _hgrn_lower_bounds(logits):
    cum = jnp.cumsum(jax.nn.softmax(logits.astype(F32), axis=0), axis=0)
    return cum - cum[0]


def kernel(x, c, ctx, c_ctx, w_mod, b_mod, norm1_g, norm2_g, w_in, w_out, hgrn_lb_logits, hgrn_norm_g, na_rpb, gla_wg_f, gla_bg_f, gla_wg_b, gla_bg_b, gla_norm_g, mla_q_norm_g, mla_w_uq, mla_kv_norm_g, mla_w_ukv, moe_w_rg, moe_b_rg, moe_w_re, moe_b_re, moe_w_gu, moe_w_dn, final_norm_g):
    batch, n, d = x.shape
    l_ctx = ctx.shape[1]
    assert d == D_MODEL and l_ctx == ROW_BLOCK and n % ROW_BLOCK == 0 and batch <= 4
    s_len = l_ctx + n
    nblk = s_len // ROW_BLOCK
    depth = w_mod.shape[0]

    c8 = jnp.zeros((8, d), F32).at[0:batch].set(c).at[4].set(c_ctx)
    mods = _mod_vectors(c8, w_mod, b_mod)
    lower_bounds = _hgrn_lower_bounds(hgrn_lb_logits)
    tables = _mla_tables(n, l_ctx)

    xa = y_prev = None
    for layer in range(depth):
        keep_ctx = layer < depth - 1
        m = mods[layer]
        stream = (xa, y_prev, mods[layer - 1]) if layer else (ctx, x)
        xa, (pa, pb, pc, pd) = _inproj(stream, m, norm1_g[layer].reshape(1, d), _inproj_weight(w_in[layer]), nblk)
        oaf, oab = _hgrn_scan(pa, lower_bounds[layer], batch, nblk)
        ob = _neighbourhood_attention(pb, na_rpb[layer], batch, s_len, l_ctx, keep_ctx)
        ogf, ogb = _gla_scan(pc, gla_wg_f[layer], gla_bg_f[layer], gla_wg_b[layer], gla_bg_b[layer], batch, nblk)
        mla_w = _mla_weights(mla_q_norm_g[layer], mla_w_uq[layer], mla_kv_norm_g[layer], mla_w_ukv[layer])
        q, k, v = _mla_prep(pd, mla_w, tables, nblk)
        od = _mla_attention(q, k, v, batch, s_len, l_ctx, keep_ctx)
        xa, h2 = _outproj(oaf, oab, pa, ob, ogf, ogb, pc, od, xa, m, hgrn_norm_g[layer].reshape(1, -1),
                          gla_norm_g[layer].reshape(1, -1), norm2_g[layer].reshape(1, d), w_out[layer],
                          batch, nblk, keep_ctx)
        y_prev = _moe(h2, moe_w_rg[layer], moe_b_rg[layer], moe_w_re[layer], moe_b_re[layer],
                      moe_w_gu[layer].astype(BF16), moe_w_dn[layer].astype(BF16))
    out = _final(xa, y_prev, mods[depth - 1], final_norm_g.reshape(1, d), n // ROW_BLOCK)
    return out.reshape(batch, n, d)
```

```python
import functools

import numpy as np
import jax
import jax.numpy as jnp
from jax import lax
from jax.experimental import pallas as pl
from jax.experimental.pallas import tpu as pltpu

F32 = jnp.float32
BF16 = jnp.bfloat16
I32 = jnp.int32

D_MODEL = 1024
DEPTH = 2
GRID_W = 64
EPS = 1e-6
D_GROUP = 256
N_HEADS = 4
A_DH = 64
B_DH = 64
WIN_H = 8
WIN_W = 16
C_DK = 32
C_DV = 64
C_GATE_RANK = 16
C_GATE_NORM = 16.0
D_NOPE = 64
D_V = 64
D_ROPE = 32
ROPE_FREQS = 8
ROPE_BASE = 10000.0
D_Q_RANK = 192
D_KV_RANK = 128
MLA_SCALE = (D_NOPE + D_ROPE) ** -0.5
N_GROUPS = 4
EXPERTS_PER_GROUP = 8
D_EXPERT = 256

ROW_BLOCK = 256
SUB = 16
MACRO = 64
DECAY_GUARD = 60.0
N_SCAN_CONSTS = 9
SCAN_BATCHES = 2
LANE = 128
HEAD_TILE = 128
VT_ROWS = 80
W_A, W_B, W_C, W_D = 1280, 768, 896, 512
TOK_TILES = D_MODEL // LANE
VMEM_LIMIT = 52 * 1024 * 1024
NEG_INF = float("-inf")


def _bdot(a, b):
    return jnp.dot(a.astype(BF16), b.astype(BF16), preferred_element_type=F32)


def _bdot_nt(a, b):
    return lax.dot_general(a.astype(BF16), b.astype(BF16), (((1,), (1,)), ((), ())), preferred_element_type=F32)


def _bdot_tn(a, b):
    return lax.dot_general(a.astype(BF16), b.astype(BF16), (((0,), (0,)), ((), ())), preferred_element_type=F32)


def _split3(a):
    hi = a.astype(BF16)
    r1 = a - hi.astype(F32)
    mid = r1.astype(BF16)
    lo = (r1 - mid.astype(F32)).astype(BF16)
    return hi, mid, lo


def _dot_f32(a, b):
    ah, am, al = _split3(a)
    bh, bm, bl = _split3(b)
    d = lambda u, v: jnp.dot(u, v, preferred_element_type=F32)
    return d(ah, bh) + (d(ah, bm) + d(am, bh)) + (d(am, bm) + d(ah, bl) + d(al, bh))


def _dot_3pass(a, b):
    ah, am, _ = _split3(a)
    bh, bm, _ = _split3(b)
    d = lambda u, v: jnp.dot(u, v, preferred_element_type=F32)
    return d(ah, bh) + (d(ah, bm) + d(am, bh))


def _dot_sel_l(sel, a):
    ah, am, al = _split3(a)
    d = lambda v: jnp.dot(sel, v, preferred_element_type=F32)
    return d(ah) + d(am) + d(al)


def _dot_sel_r(a, sel):
    ah, am, al = _split3(a)
    d = lambda u: jnp.dot(u, sel, preferred_element_type=F32)
    return d(ah) + d(am) + d(al)


def _sigmoid(x):
    return 1.0 / (1.0 + jnp.exp(-x))


def _log_sigmoid(x):
    return jnp.minimum(x, 0.0) - jnp.log1p(jnp.exp(-jnp.abs(x)))


def _logaddexp(a, b):
    amax = jnp.maximum(a, b)
    delta = a - b
    return jnp.where(jnp.isnan(delta), a + b, amax + jnp.log1p(jnp.exp(-jnp.abs(delta))))


def _rms(x, width):
    return x * lax.rsqrt(jnp.sum(x * x, axis=-1, keepdims=True) / width + EPS)


def _cparams(sem, vmem=VMEM_LIMIT):
    return pltpu.CompilerParams(dimension_semantics=sem, vmem_limit_bytes=vmem)


def _mod_kernel(c_ref, w_ref, b_ref, o_ref):
    c = c_ref[...]
    act = c * _sigmoid(c)
    o_ref[0] = _dot_f32(act, w_ref[0]) + b_ref[0]


def _mod_vectors(c8, w_mod, b_mod):
    depth, d, six_d = w_mod.shape
    nj = six_d // d
    return pl.pallas_call(
        _mod_kernel,
        grid=(depth, nj),
        in_specs=[
            pl.BlockSpec((8, d), lambda l, j: (0, 0)),
            pl.BlockSpec((1, d, d), lambda l, j: (l, 0, j)),
            pl.BlockSpec((1, 1, d), lambda l, j: (l, 0, j)),
        ],
        out_specs=pl.BlockSpec((1, 8, d), lambda l, j: (l, 0, j)),
        out_shape=jax.ShapeDtypeStruct((depth, 8, six_d), F32),
        compiler_params=_cparams(("arbitrary", "arbitrary")),
        name="mod_vectors",
    )(c8, w_mod, b_mod.reshape(depth, 1, six_d))


def _mod_row(m_ref, row):
    return m_ref[pl.ds(row, 1), :]


def _inproj_kernel(*refs, first, nblk):
    if first:
        ctx_ref, lat_ref, m_ref, g_ref, w_ref, xo_ref, pa_ref, pb_ref, pc_ref, pd_ref = refs
    else:
        x_ref, y_ref, mprev_ref, m_ref, g_ref, w_ref, xo_ref, pa_ref, pb_ref, pc_ref, pd_ref = refs
    i = pl.program_id(0)
    b = i // nblk
    is_ctx = i % nblk == 0
    row = jnp.where(is_ctx, 4, b)
    d = D_MODEL
    if first:
        x = jnp.where(is_ctx, ctx_ref[0], lat_ref[0])
    else:
        mp = _mod_row(mprev_ref, row)
        x = x_ref[...] + mp[:, 5 * d:6 * d] * _load_token_rows(y_ref)
    xo_ref[...] = x
    m = _mod_row(m_ref, row)
    h = _rms(x, d) * g_ref[...] * (1.0 + m[:, d:2 * d]) + m[:, 0:d]
    p = _bdot(h, w_ref[...])
    pa_ref[...] = p[:, 0:W_A]
    pb_ref[...] = p[:, W_A:W_A + W_B].astype(pb_ref.dtype)
    pc_ref[...] = p[:, W_A + W_B:W_A + W_B + W_C]
    pd_ref[...] = p[:, W_A + W_B + W_C:]


def _inproj(stream, m, g, w, nblk):
    first = len(stream) == 2
    rb, d = ROW_BLOCK, D_MODEL
    row_spec = lambda w_: pl.BlockSpec((rb, w_), lambda i: (i, 0))
    full = lambda a: pl.BlockSpec(a.shape, lambda i: (0,) * a.ndim)
    if first:
        ctx, lat = stream
        t = ctx.shape[0] * nblk * rb
        specs = [pl.BlockSpec((1, rb, d), lambda i: (i // nblk, 0, 0)),
                 pl.BlockSpec((1, rb, d), lambda i: (i // nblk, jnp.maximum(i % nblk - 1, 0), 0))]
    else:
        t = stream[0].shape[0]
        specs = [row_spec(d), pl.BlockSpec((rb * TOK_TILES, LANE), lambda i: (i, 0)), full(stream[2])]
    ins = list(stream) + [m, g, w]
    specs += [full(m), full(g), full(w)]
    outs, ospecs = [jax.ShapeDtypeStruct((t, d), F32)], [row_spec(d)]
    for w_ in (W_A, W_B, W_C, W_D):
        outs.append(jax.ShapeDtypeStruct((t, w_), BF16 if w_ == W_B else F32))
        ospecs.append(row_spec(w_))
    res = pl.pallas_call(
        functools.partial(_inproj_kernel, first=first, nblk=nblk),
        grid=(t // rb,),
        in_specs=specs,
        out_specs=ospecs,
        out_shape=outs,
        compiler_params=_cparams(("arbitrary",)),
        name="in_projection",
    )(*ins)
    return res[0], res[1:]


def _sub_chunk(refs, i, tri, emat, emask_t, reverse):
    q_ref, k_ref, la_ref, v_ref, o_ref, st_ref, r_ref = refs
    hv = v_ref.shape[1]
    row_id = lax.broadcasted_iota(I32, (SUB, 1), 0)
    off = pl.multiple_of(i * SUB, SUB)
    qs = q_ref[pl.ds(off, SUB), :]
    ks = k_ref[pl.ds(off, SUB), :]
    las = la_ref[pl.ds(off, SUB), :]
    vs = v_ref[pl.ds(off, SUB), :]
    cum = _dot_sel_l(tri, las)
    last = cum[0:1, :] if reverse else cum[SUB - 1:SUB, :]
    for j in range(SUB):
        valid = (row_id <= j) if reverse else (row_id >= j)
        dlt = jnp.where(valid, cum - cum[j:j + 1, :], NEG_INF)
        r_ref[j * SUB:(j + 1) * SUB, :] = (qs * ks[j:j + 1, :] * jnp.exp(dlt)).astype(BF16)
    att = jnp.dot(r_ref[...], emat, preferred_element_type=F32)
    o = jnp.zeros((SUB, hv), F32)
    for j in range(SUB):
        o = o + att[j * SUB:(j + 1) * SUB, :] * vs[j:j + 1, :]
    st = st_ref[...]
    o = o + _bdot_nt(qs * jnp.exp(cum), st)
    o_ref[pl.ds(off, SUB), :] = o
    kd = ks * jnp.exp(last - cum)
    st_ref[...] = st * jnp.exp(last) + _bdot_tn(vs, kd) * emask_t


def _macro_step(refs, m, tri, emask_t, hmk, hmv, reverse):
    q_ref, k_ref, la_ref, v_ref, o_ref, st_ref, _ = refs
    hv = v_ref.shape[1]
    n_sub = MACRO // SUB
    off = pl.multiple_of(m * MACRO, MACRO)
    q = q_ref[pl.ds(off, MACRO), :]
    k = k_ref[pl.ds(off, MACRO), :]
    v = v_ref[pl.ds(off, MACRO), :]
    cum = _dot_sel_l(tri, la_ref[pl.ds(off, MACRO), :])
    last = cum[0:1, :] if reverse else cum[MACRO - 1:MACRO, :]
    st = st_ref[...]
    o_inter = _bdot_nt(q * jnp.exp(cum), st)
    st_ref[...] = st * jnp.exp(last) + _bdot_tn(v, k * jnp.exp(last - cum)) * emask_t
    vb = v.astype(BF16)
    row_i = lax.broadcasted_iota(I32, (N_HEADS * SUB, 1), 0) % SUB
    for s in range(n_sub):
        lo = s * SUB
        if reverse:
            k_lo, k_hi = lo, MACRO
            ref = cum[lo + SUB:lo + SUB + 1, :] if s < n_sub - 1 else jnp.zeros_like(last)
        else:
            k_lo, k_hi = 0, lo + SUB
            ref = cum[lo - 1:lo, :] if s > 0 else jnp.zeros_like(last)
        qt = q[lo:lo + SUB, :] * jnp.exp(cum[lo:lo + SUB, :] - ref)
        qs = jnp.concatenate([qt * hmk[h:h + 1, :] for h in range(N_HEADS)], axis=0).astype(BF16)
        kt = (k[k_lo:k_hi, :] * jnp.exp(ref - cum[k_lo:k_hi, :])).astype(BF16)
        att = lax.dot_general(qs, kt, (((1,), (1,)), ((), ())), preferred_element_type=F32)
        col = lax.broadcasted_iota(I32, (1, k_hi - k_lo), 1) + k_lo
        valid = (col >= lo + row_i) if reverse else (col <= lo + row_i)
        att = jnp.where(valid, att, 0.0)
        o_heads = jnp.dot(att.astype(BF16), vb[k_lo:k_hi, :], preferred_element_type=F32)
        o = o_inter[lo:lo + SUB, :]
        for h in range(N_HEADS):
            o = o + o_heads[h * SUB:(h + 1) * SUB, :] * hmv[h:h + 1, :]
        o_ref[pl.ds(off + lo, SUB), :] = o


def _scan_chains(chains, consts):
    tri_f, tri_b, trim_f, trim_b, emat, emask_t, hmk, hmv, sub_sum = consts
    rows = chains[0][0][0].shape[0]
    n_sub, n_macro = rows // SUB, rows // MACRO
    tot = None
    for refs, _ in chains:
        block_tot = jnp.dot(sub_sum, refs[2][...].astype(BF16), preferred_element_type=F32)
        tot = block_tot if tot is None else jnp.minimum(tot, block_tot)
    factorisable = jnp.min(tot) > -DECAY_GUARD

    @pl.when(factorisable)
    def _():
        def body(step, carry):
            for refs, rev in chains:
                _macro_step(refs, n_macro - 1 - step if rev else step, trim_b if rev else trim_f, emask_t, hmk, hmv, rev)
            return carry

        lax.fori_loop(0, n_macro, body, 0, unroll=True)

    @pl.when(jnp.logical_not(factorisable))
    def _():
        def body(step, carry):
            for refs, rev in chains:
                _sub_chunk(refs, n_sub - 1 - step if rev else step, tri_b if rev else tri_f, emat, emask_t, rev)
            return carry

        lax.fori_loop(0, n_sub, body, 0)


def _zero_at_first_block(*state_refs):
    @pl.when(pl.program_id(1) == 0)
    def _():
        for st in state_refs:
            st[...] = jnp.zeros_like(st)


def _scan_batches(batch):
    return SCAN_BATCHES if batch % SCAN_BATCHES == 0 else 1


def _scan_consts(hk, hv, dk, dv):
    tri = lambda n, low: jnp.asarray(np.tril(np.ones((n, n), np.float32)) if low else np.triu(np.ones((n, n), np.float32)), BF16)
    em = _head_match(hk, hv, dk, dv)
    heads = np.arange(N_HEADS)[:, None]
    hmk = (np.arange(hk)[None, :] // dk == heads).astype(np.float32)
    hmv = (np.arange(hv)[None, :] // dv == heads).astype(np.float32)
    sub_sum = (np.arange(ROW_BLOCK)[None, :] // SUB == np.arange(ROW_BLOCK // SUB)[:, None]).astype(np.float32)
    return [tri(SUB, True), tri(SUB, False), tri(MACRO, True), tri(MACRO, False), jnp.asarray(em, BF16),
            jnp.asarray(em.T, F32), jnp.asarray(hmk), jnp.asarray(hmv), jnp.asarray(sub_sum, BF16)]


def _hgrn_prep(q_ref, v_ref, f_ref, lb, qo, ko, lo, vo):
    qr = q_ref[...]
    qo[...] = qr * _sigmoid(qr) * (A_DH ** -0.5)
    vo[...] = v_ref[...]
    z = f_ref[...]
    lo[...] = _logaddexp(jnp.log(lb), jnp.log1p(-lb) + _log_sigmoid(z))
    ko[...] = (1.0 - lb) * _sigmoid(-z)


def _hgrn_kernel(qf_ref, vf_ref, ff_ref, qb_ref, vb_ref, fb_ref, lb_ref, *rest):
    consts, (of_ref, ob_ref), scratch = rest[:N_SCAN_CONSTS], rest[N_SCAN_CONSTS:N_SCAN_CONSTS + 2], rest[N_SCAN_CONSTS + 2:]
    chains = []
    for s in range(qf_ref.shape[0]):
        stf, stb, rf, rb_, q1, k1, l1, v1, q2, k2, l2, v2 = scratch[12 * s:12 * s + 12]
        _zero_at_first_block(stf, stb)
        _hgrn_prep(qf_ref.at[s], vf_ref.at[s], ff_ref.at[s], lb_ref[0:1, :], q1, k1, l1, v1)
        _hgrn_prep(qb_ref.at[s], vb_ref.at[s], fb_ref.at[s], lb_ref[1:2, :], q2, k2, l2, v2)
        chains += [((q1, k1, l1, v1, of_ref.at[s], stf, rf), False), ((q2, k2, l2, v2, ob_ref.at[s], stb, rb_), True)]
    _scan_chains(chains, [c[...] for c in consts])


def _head_match(hk, hv, dk, dv):
    m = (np.arange(hk)[:, None] // dk == np.arange(hv)[None, :] // dv).astype(np.float32)
    return m


def _bwd_block(t, nblk):
    return jnp.where(t == 0, 0, nblk - t)


def _hgrn_scan(pa, lb, batch, nblk):
    rb, w = ROW_BLOCK, D_GROUP
    consts = _scan_consts(w, w, A_DH, A_DH)
    nb = _scan_batches(batch)
    pa3 = pa.reshape(batch, nblk * rb, pa.shape[1])
    fwd = lambda col: pl.BlockSpec((nb, rb, w), lambda b, t: (b, t, col))
    bwd = lambda col: pl.BlockSpec((nb, rb, w), lambda b, t: (b, _bwd_block(t, nblk), col))
    full = lambda a: pl.BlockSpec(a.shape, lambda b, t: (0,) * a.ndim)
    vm = lambda shape, dt=F32: pltpu.VMEM(shape, dt)
    of, ob = pl.pallas_call(
        _hgrn_kernel,
        grid=(batch // nb, nblk),
        in_specs=[fwd(0), fwd(1), fwd(2), bwd(0), bwd(1), bwd(3), full(lb)] + [full(c) for c in consts],
        out_specs=[fwd(0), bwd(0)],
        out_shape=[jax.ShapeDtypeStruct((batch, nblk * rb, w), F32)] * 2,
        scratch_shapes=([vm((w, w)), vm((w, w)), vm((SUB * SUB, w), BF16), vm((SUB * SUB, w), BF16)]
                        + [vm((rb, w))] * 8) * nb,
        compiler_params=_cparams(("arbitrary", "arbitrary")),
        name="hgrn_scan",
    )(pa3, pa3, pa3, pa3, pa3, pa3, lb, *consts)
    return of.reshape(-1, w), ob.reshape(-1, w)


def _gla_prep(q_ref, k_ref, v_ref, z_ref, wg_ref, bg_ref, qo, ko, lo, vo):
    qo[...] = q_ref[...] * (C_DK ** -0.5)
    ko[...] = k_ref[...]
    vo[...] = v_ref[...]
    zl = _dot_f32(z_ref[...], wg_ref[...]) + bg_ref[...]
    lo[...] = _log_sigmoid(zl) / C_GATE_NORM


def _gla_kernel(qf_ref, kf_ref, vf_ref, zf_ref, qb_ref, kb_ref, vb_ref, zb_ref, wgf_ref, bgf_ref, wgb_ref, bgb_ref, *rest):
    consts, (of_ref, ob_ref), scratch = rest[:N_SCAN_CONSTS], rest[N_SCAN_CONSTS:N_SCAN_CONSTS + 2], rest[N_SCAN_CONSTS + 2:]
    chains = []
    for s in range(qf_ref.shape[0]):
        stf, stb, rf, rb_, q1, k1, l1, v1, q2, k2, l2, v2 = scratch[12 * s:12 * s + 12]
        _zero_at_first_block(stf, stb)
        _gla_prep(qf_ref.at[s], kf_ref.at[s], vf_ref.at[s], zf_ref.at[s], wgf_ref, bgf_ref, q1, k1, l1, v1)
        _gla_prep(qb_ref.at[s], kb_ref.at[s], vb_ref.at[s], zb_ref.at[s], wgb_ref, bgb_ref, q2, k2, l2, v2)
        chains += [((q1, k1, l1, v1, of_ref.at[s], stf, rf), False), ((q2, k2, l2, v2, ob_ref.at[s], stb, rb_), True)]
    _scan_chains(chains, [c[...] for c in consts])


def _gla_scan(pc, wg_f, bg_f, wg_b, bg_b, batch, nblk):
    rb = ROW_BLOCK
    hk, hv = N_HEADS * C_DK, N_HEADS * C_DV
    consts = _scan_consts(hk, hv, C_DK, C_DV)
    wgf = jnp.zeros((LANE, hk), F32).at[0:C_GATE_RANK].set(wg_f)
    wgb = jnp.zeros((LANE, hk), F32).at[C_GATE_RANK:2 * C_GATE_RANK].set(wg_b)
    bgf, bgb = bg_f.reshape(1, hk), bg_b.reshape(1, hk)
    nb = _scan_batches(batch)
    pc3 = pc.reshape(batch, nblk * rb, pc.shape[1])
    fwd = lambda w, col: pl.BlockSpec((nb, rb, w), lambda b, t: (b, t, col))
    bwd = lambda w, col: pl.BlockSpec((nb, rb, w), lambda b, t: (b, _bwd_block(t, nblk), col))
    full = lambda a: pl.BlockSpec(a.shape, lambda b, t: (0,) * a.ndim)
    vm = lambda shape, dt=F32: pltpu.VMEM(shape, dt)
    of, ob = pl.pallas_call(
        _gla_kernel,
        grid=(batch // nb, nblk),
        in_specs=[fwd(hk, 0), fwd(hk, 1), fwd(hv, 1), fwd(LANE, 6), bwd(hk, 0), bwd(hk, 1), bwd(hv, 1), bwd(LANE, 6),
                  full(wgf), full(bgf), full(wgb), full(bgb)] + [full(c) for c in consts],
        out_specs=[fwd(hv, 0), bwd(hv, 0)],
        out_shape=[jax.ShapeDtypeStruct((batch, nblk * rb, hv), F32)] * 2,
        scratch_shapes=([vm((hv, hk)), vm((hv, hk)), vm((SUB * SUB, hk), BF16), vm((SUB * SUB, hk), BF16)]
                        + [vm((rb, hk)), vm((rb, hk)), vm((rb, hk)), vm((rb, hv))] * 2) * nb,
        compiler_params=_cparams(("arbitrary", "arbitrary")),
        name="gla_scan",
    )(pc3, pc3, pc3, pc3, pc3, pc3, pc3, pc3, wgf, bgf, wgb, bgb, *consts)
    return of.reshape(-1, hv), ob.reshape(-1, hv)


NA_QROWS = ROW_BLOCK // GRID_W
NA_WROWS = WIN_H + NA_QROWS


def _na_kernel(q_ref, k_ref, v_ref, bias_ref, hm_ref, o_ref, *, j0, rows, ctx):
    j = pl.program_id(1) + j0
    q = q_ref[...] * (B_DH ** -0.5)
    hm = hm_ref[...]
    kc = k_ref[0:ctx, :]
    vc = v_ref[0:ctx, :]
    nt = lambda a, b: lax.dot_general(a, b, (((1,), (1,)), ((), ())), preferred_element_type=F32)

    @pl.when(j == 0)
    def _():
        acc = jnp.zeros(q.shape, F32)
        for h in range(N_HEADS):
            mh = hm[h:h + 1, :]
            s = nt(q * mh.astype(BF16), kc)
            p = jnp.exp(s - jnp.max(s, axis=-1, keepdims=True))
            inv = 1.0 / jnp.sum(p, axis=-1, keepdims=True)
            acc = acc + jnp.dot(p.astype(BF16), vc, preferred_element_type=F32) * (mh * inv)
        o_ref[...] = acc.astype(o_ref.dtype)

    @pl.when(j > 0)
    def _():
        r0 = (j - 1) * NA_QROWS
        start = jnp.clip(r0 - WIN_H // 2, 0, rows - NA_WROWS)
        off = pl.multiple_of(ctx + start * GRID_W, GRID_W)
        kw = k_ref[pl.ds(off, NA_WROWS * GRID_W), :]
        vw = v_ref[pl.ds(off, NA_WROWS * GRID_W), :]
        acc = jnp.zeros(q.shape, F32)
        for h in range(N_HEADS):
            mh = hm[h:h + 1, :]
            qh = q * mh.astype(BF16)
            sw = nt(qh, kw) + bias_ref[0, h]
            sc = nt(qh, kc)
            m = jnp.maximum(jnp.max(sw, axis=-1, keepdims=True), jnp.max(sc, axis=-1, keepdims=True))
            pw = jnp.exp(sw - m)
            pc_ = jnp.exp(sc - m)
            inv = 1.0 / (jnp.sum(pw, axis=-1, keepdims=True) + jnp.sum(pc_, axis=-1, keepdims=True))
            o = (jnp.dot(pw.astype(BF16), vw, preferred_element_type=F32)
                 + jnp.dot(pc_.astype(BF16), vc, preferred_element_type=F32))
            acc = acc + o * (mh * inv)
        o_ref[...] = acc.astype(o_ref.dtype)


def _na_bias_table(rpb, rows):
    kh = WIN_H
    cidx = np.arange(GRID_W)
    c_start = np.clip(cidx - WIN_W // 2, 0, GRID_W - WIN_W)
    col_in = (cidx[None] >= c_start[:, None]) & (cidx[None] < c_start[:, None] + WIN_W)
    dc = np.clip(cidx[None] - cidx[:, None], -(WIN_W - 1), WIN_W - 1) + (WIN_W - 1)
    sel = (dc[None] == np.arange(2 * WIN_W - 1)[:, None, None]).astype(np.float32)
    by_col = jnp.einsum("hrc,cqw->hrqw", rpb.astype(F32), jnp.asarray(sel), precision=lax.Precision.HIGHEST)
    sel_row = np.zeros((3, NA_QROWS, NA_WROWS, 2 * kh - 1), np.float32)
    for p, r0 in enumerate((0, NA_QROWS, rows - NA_QROWS)):
        ws = int(np.clip(r0 - kh // 2, 0, rows - NA_WROWS))
        for a in range(NA_QROWS):
            r = r0 + a
            s = int(np.clip(r - kh // 2, 0, rows - kh))
            for jj in range(NA_WROWS):
                if s <= ws + jj < s + kh:
                    sel_row[p, a, jj, ws + jj - r + kh - 1] = 1.0
    bias = jnp.einsum("pajr,hrqw->phaqjw", jnp.asarray(sel_row), by_col, precision=lax.Precision.HIGHEST)
    visible = (sel_row.sum(axis=-1) > 0)[:, None, :, None, :, None] & col_in[None, None, None, :, None, :]
    bias = jnp.where(jnp.asarray(visible), bias, NEG_INF)
    return bias.reshape(3, N_HEADS, NA_QROWS * GRID_W, NA_WROWS * GRID_W)


def _head_masks(width, dh):
    return jnp.asarray((np.arange(width)[None, :] // dh == np.arange(N_HEADS)[:, None]).astype(np.float32))


def _neighbourhood_attention(pb, rpb, batch, s_len, ctx, keep_ctx):
    rows = (s_len - ctx) // GRID_W
    assert rows >= NA_WROWS and rows % NA_QROWS == 0 and ctx == ROW_BLOCK
    rb = ROW_BLOCK
    j0 = 0 if keep_ctx else 1
    per_b = s_len // rb
    bias = _na_bias_table(rpb, rows)
    hm = _head_masks(D_GROUP, B_DH)

    def bias_idx(b, jj):
        r0 = (jj + j0 - 1) * NA_QROWS
        return (jnp.where(r0 <= 0, 0, jnp.where(r0 == rows - NA_QROWS, 2, 1)), 0, 0, 0)

    return pl.pallas_call(
        functools.partial(_na_kernel, j0=j0, rows=rows, ctx=ctx),
        grid=(batch, per_b - j0),
        in_specs=[
            pl.BlockSpec((rb, D_GROUP), lambda b, jj: (b * per_b + jj + j0, 0)),
            pl.BlockSpec((s_len, D_GROUP), lambda b, jj: (b, 1)),
            pl.BlockSpec((s_len, D_GROUP), lambda b, jj: (b, 2)),
            pl.BlockSpec((1, N_HEADS, rb, NA_WROWS * GRID_W), bias_idx),
            pl.BlockSpec(hm.shape, lambda b, jj: (0, 0)),
        ],
        out_specs=pl.BlockSpec((rb, D_GROUP), lambda b, jj: (b * (per_b - j0) + jj, 0)),
        out_shape=jax.ShapeDtypeStruct((batch * (per_b - j0) * rb, D_GROUP), BF16),
        compiler_params=_cparams(("arbitrary", "arbitrary")),
        name="neighbourhood_attention",
    )(pb, pb, pb, bias, hm)


def _mla_prep_kernel(pd_ref, gq_ref, gkv_ref, wq1_ref, wq2_ref, wk_ref, wvt_ref, ones_ref, cq_ref, sq_ref, tk_ref,
                     place_ref, q_ref, k_ref, vt_ref):
    pd = pd_ref[...]
    cq = pd[:, 0:256]
    ckv = pd[:, 256:384]
    kr = pd[:, 384:512]
    qn = _rms(cq, D_Q_RANK) * gq_ref[...]
    q = _bdot(qn, wq1_ref[...]) * cq_ref[...] + _bdot(qn, wq2_ref[...]) * sq_ref[...]
    q_ref[...] = q.astype(BF16)
    kvn = (_rms(ckv, D_KV_RANK) * gkv_ref[...]).astype(BF16)
    k = jnp.dot(kvn, wk_ref[...], preferred_element_type=F32) + _dot_sel_r(kr * tk_ref[...], place_ref[...])
    k_ref[...] = k.astype(BF16)
    vt = lax.dot_general(wvt_ref[...], kvn, (((1,), (1,)), ((), ())), preferred_element_type=F32) + ones_ref[...]
    vt_ref[0] = vt.astype(BF16)


def _rope_swap_perm():
    f = ROPE_FREQS
    return np.concatenate([np.arange(f, 2 * f), np.arange(0, f), np.arange(3 * f, 4 * f), np.arange(2 * f, 3 * f)])


def _mla_tables(n, ctx):
    t = np.arange(n)
    inv_freq = ROPE_BASE ** (-np.arange(ROPE_FREQS, dtype=np.float32) / ROPE_FREQS)
    ang_r = (t // GRID_W).astype(np.float32)[:, None] * inv_freq
    ang_c = (t % GRID_W).astype(np.float32)[:, None] * inv_freq
    cos32 = np.concatenate([np.cos(ang_r), np.cos(ang_r), np.cos(ang_c), np.cos(ang_c)], axis=1)
    sin32 = np.concatenate([-np.sin(ang_r), np.sin(ang_r), -np.sin(ang_c), np.sin(ang_c)], axis=1)
    cos32 = np.concatenate([np.ones((ctx, D_ROPE), np.float32), cos32.astype(np.float32)], axis=0)
    sin32 = np.concatenate([np.zeros((ctx, D_ROPE), np.float32), sin32.astype(np.float32)], axis=0)
    s_len = n + ctx
    cq = np.zeros((s_len, N_HEADS, HEAD_TILE), np.float32)
    sq = np.zeros((s_len, N_HEADS, HEAD_TILE), np.float32)
    cq[:, :, 0:D_NOPE] = MLA_SCALE
    cq[:, :, D_NOPE:D_NOPE + D_ROPE] = cos32[:, None, :] * MLA_SCALE
    sq[:, :, D_NOPE:D_NOPE + D_ROPE] = sin32[:, None, :] * MLA_SCALE
    tk = np.zeros((s_len, LANE), np.float32)
    tk[:, 0:D_ROPE] = cos32
    tk[:, D_ROPE:2 * D_ROPE] = sin32
    place = np.zeros((LANE, N_HEADS * HEAD_TILE), np.float32)
    for h in range(N_HEADS):
        for l in range(D_ROPE):
            place[l, h * HEAD_TILE + D_NOPE + l] = 1.0
            place[D_ROPE + l, h * HEAD_TILE + D_NOPE + l] = 1.0
    return (jnp.asarray(cq.reshape(s_len, -1)), jnp.asarray(sq.reshape(s_len, -1)), jnp.asarray(tk),
            jnp.asarray(place, BF16))


def _mla_weights(q_norm_g, w_uq, kv_norm_g, w_ukv):
    perm = _rope_swap_perm()
    wq = w_uq.reshape(D_Q_RANK, N_HEADS, D_NOPE + D_ROPE)
    wq1 = jnp.zeros((256, N_HEADS, HEAD_TILE), F32).at[0:D_Q_RANK, :, 0:D_NOPE + D_ROPE].set(wq)
    wq2 = jnp.zeros((256, N_HEADS, HEAD_TILE), F32).at[0:D_Q_RANK, :, D_NOPE:D_NOPE + D_ROPE].set(
        wq[:, :, D_NOPE:][:, :, perm])
    wkv = w_ukv.reshape(D_KV_RANK, N_HEADS, D_NOPE + D_V)
    wk = jnp.zeros((D_KV_RANK, N_HEADS, HEAD_TILE), F32).at[:, :, 0:D_NOPE].set(wkv[:, :, 0:D_NOPE])
    wvt = jnp.zeros((N_HEADS, VT_ROWS, D_KV_RANK), F32).at[:, 0:D_V, :].set(jnp.transpose(wkv[:, :, D_NOPE:], (1, 2, 0)))
    wvt = wvt.reshape(N_HEADS * VT_ROWS, D_KV_RANK)
    ones_rows = jnp.asarray((np.arange(N_HEADS * VT_ROWS) % VT_ROWS >= D_V).astype(np.float32)).reshape(-1, 1)
    hw = N_HEADS * HEAD_TILE
    gq = jnp.zeros((1, 256), F32).at[0, 0:D_Q_RANK].set(q_norm_g)
    return (gq, kv_norm_g.reshape(1, D_KV_RANK), wq1.reshape(256, hw).astype(BF16), wq2.reshape(256, hw).astype(BF16),
            wk.reshape(D_KV_RANK, hw).astype(BF16), wvt.astype(BF16), ones_rows)


def _mla_prep(pd, weights, tables, nblk):
    t_all = pd.shape[0]
    rb = ROW_BLOCK
    gq, gkv, wq1, wq2, wk, wvt, ones_rows = weights
    cq, sq, tk, place = tables
    hw = N_HEADS * HEAD_TILE
    hv = N_HEADS * VT_ROWS
    row = lambda w: pl.BlockSpec((rb, w), lambda i: (i, 0))
    pos = lambda w: pl.BlockSpec((rb, w), lambda i: (i % nblk, 0))
    full = lambda a: pl.BlockSpec(a.shape, lambda i: (0,) * a.ndim)
    return pl.pallas_call(
        _mla_prep_kernel,
        grid=(t_all // rb,),
        in_specs=[row(W_D), full(gq), full(gkv), full(wq1), full(wq2), full(wk), full(wvt), full(ones_rows),
                  pos(hw), pos(hw), pos(LANE), full(place)],
        out_specs=[row(hw), row(hw), pl.BlockSpec((1, hv, rb), lambda i: (i // nblk, 0, i % nblk))],
        out_shape=[jax.ShapeDtypeStruct((t_all, hw), BF16), jax.ShapeDtypeStruct((t_all, hw), BF16),
                   jax.ShapeDtypeStruct((t_all // (nblk * rb), hv, nblk * rb), BF16)],
        compiler_params=_cparams(("arbitrary",)),
        name="mla_prep",
    )(pd, gq, gkv, wq1, wq2, wk, wvt, ones_rows, cq, sq, tk, place)


MLA_HEADS_PER_STEP = 4
MLA_KEY_CHUNK = 256


def _mla_attn_kernel(q_ref, k_ref, vt_ref, o_ref, s_scr, *, j0, ctx):
    j = pl.program_id(2) + j0
    rb = q_ref.shape[0]

    def heads(n_keys):
        hs = range(MLA_HEADS_PER_STEP)
        chunks = [slice(c * MLA_KEY_CHUNK, (c + 1) * MLA_KEY_CHUNK) for c in range(n_keys // MLA_KEY_CHUNK)]
        lanes = [slice(h * HEAD_TILE, (h + 1) * HEAD_TILE) for h in hs]
        qs = [q_ref[:, lanes[h]] for h in hs]
        m = [jnp.full((1, rb), NEG_INF, F32) for _ in hs]
        for keys in chunks:
            for h in hs:
                st = lax.dot_general(k_ref[keys, lanes[h]], qs[h], (((1,), (1,)), ((), ())),
                                     preferred_element_type=F32)
                s_scr[h, keys, :] = st
                m[h] = jnp.maximum(m[h], jnp.max(st, axis=0, keepdims=True))
        acc = [jnp.zeros((VT_ROWS, rb), F32) for _ in hs]
        for keys in chunks:
            for h in hs:
                pt = jnp.exp(s_scr[h, keys, :] - m[h]).astype(BF16)
                acc[h] = acc[h] + jnp.dot(vt_ref[0, h * VT_ROWS:(h + 1) * VT_ROWS, keys], pt,
                                          preferred_element_type=F32)
        for h in hs:
            o_ref[0, h * D_V:(h + 1) * D_V, :] = (acc[h][0:D_V, :] * (1.0 / acc[h][D_V:D_V + 1, :])).astype(o_ref.dtype)

    @pl.when(j == 0)
    def _():
        heads(ctx)

    @pl.when(j > 0)
    def _():
        heads(k_ref.shape[0])


def _mla_attention(q, k, vt, batch, s_len, ctx, keep_ctx):
    rb = ROW_BLOCK
    nblk = s_len // rb
    j0 = 0 if keep_ctx else 1
    hps = MLA_HEADS_PER_STEP
    return pl.pallas_call(
        functools.partial(_mla_attn_kernel, j0=j0, ctx=ctx),
        grid=(batch, N_HEADS // hps, nblk - j0),
        in_specs=[
            pl.BlockSpec((rb, hps * HEAD_TILE), lambda b, h, jj: (b * nblk + jj + j0, h)),
            pl.BlockSpec((s_len, hps * HEAD_TILE), lambda b, h, jj: (b, h)),
            pl.BlockSpec((1, hps * VT_ROWS, s_len), lambda b, h, jj: (b, h, 0)),
        ],
        out_specs=pl.BlockSpec((1, hps * D_V, rb), lambda b, h, jj: (b * (nblk - j0) + jj, h, 0)),
        out_shape=jax.ShapeDtypeStruct((batch * (nblk - j0), N_HEADS * D_V, rb), BF16),
        scratch_shapes=[pltpu.VMEM((hps, s_len, rb), F32)],
        compiler_params=_cparams(("arbitrary", "arbitrary", "arbitrary")),
        name="mla_attention",
    )(q, k, vt)


def _outproj_kernel(oaf_ref, oab_ref, ga_ref, ob_ref, ogf_ref, ogb_ref, gc_ref, od_ref, x_ref, m_ref, gha_ref, ghc_ref,
                    g2_ref, wa_ref, wb_ref, wc_ref, wd_ref, em_ref, x1_ref, h2_ref, *, j0, nblk_out):
    i = pl.program_id(0)
    b = i // nblk_out
    row = jnp.where(i % nblk_out + j0 == 0, 4, b)
    d = D_MODEL
    em = em_ref[...]

    def readout(o, g_norm, gate, dh):
        ms = _dot_sel_r(o * o, em) / dh
        return o * lax.rsqrt(ms + EPS) * g_norm * (gate * _sigmoid(gate))

    a = readout(oaf_ref[...] + oab_ref[...], gha_ref[...], ga_ref[...], A_DH)
    c = readout(ogf_ref[...] + ogb_ref[...], ghc_ref[...], gc_ref[...], C_DV)
    mix = (_bdot(a, wa_ref[...]) + jnp.dot(ob_ref[...], wb_ref[...], preferred_element_type=F32)
           + _bdot(c, wc_ref[...])
           + lax.dot_general(od_ref[0], wd_ref[...], (((0,), (0,)), ((), ())), preferred_element_type=F32))
    m = _mod_row(m_ref, row)
    x1 = x_ref[...] + m[:, 2 * d:3 * d] * mix
    x1_ref[...] = x1
    _store_token_rows(h2_ref, _rms(x1, d) * g2_ref[...] * (1.0 + m[:, 4 * d:5 * d]) + m[:, 3 * d:4 * d])


def _outproj(oaf, oab, pa, ob, ogf, ogb, pc, od, x, m, gha, ghc, g2, w_out, batch, nblk, keep_ctx):
    rb, d = ROW_BLOCK, D_MODEL
    j0 = 0 if keep_ctx else 1
    nblk_out = nblk - j0
    t_out = batch * nblk_out * rb
    wa = w_out[0:256].astype(BF16)
    wb = w_out[256:512].astype(BF16)
    wc = w_out[512:768].astype(BF16)
    wd = w_out[768:1024].astype(BF16)
    em = jnp.asarray(_head_match(D_GROUP, D_GROUP, 64, 64), BF16)
    src = lambda i: (i // nblk_out) * nblk + i % nblk_out + j0
    row_in = lambda w, col=0: pl.BlockSpec((rb, w), lambda i: (src(i), col))
    row_out = lambda w: pl.BlockSpec((rb, w), lambda i: (i, 0))
    full = lambda a_: pl.BlockSpec(a_.shape, lambda i: (0,) * a_.ndim)
    return pl.pallas_call(
        functools.partial(_outproj_kernel, j0=j0, nblk_out=nblk_out),
        grid=(t_out // rb,),
        in_specs=[row_in(256), row_in(256), row_in(256, 4), row_out(256), row_in(256), row_in(256), row_in(256, 2),
                  pl.BlockSpec((1, N_HEADS * D_V, rb), lambda i: (i, 0, 0)), row_in(d), full(m), full(gha), full(ghc), full(g2),
                  full(wa), full(wb), full(wc), full(wd), full(em)],
        out_specs=[row_out(d), pl.BlockSpec((rb * TOK_TILES, LANE), lambda i: (i, 0))],
        out_shape=[jax.ShapeDtypeStruct((t_out, d), F32), jax.ShapeDtypeStruct((t_out * TOK_TILES, LANE), F32)],
        compiler_params=_cparams(("arbitrary",)),
        name="out_projection",
    )(oaf, oab, pa, ob, ogf, ogb, pc, od, x, m, gha, ghc, g2, wa, wb, wc, wd, em)


def _load_token_rows(ref):
    rows = ref.shape[0] // TOK_TILES
    return jnp.concatenate([ref[pl.ds(k, rows, stride=TOK_TILES), :] for k in range(TOK_TILES)], axis=1)


def _store_token_rows(ref, val):
    rows = val.shape[0]
    for k in range(TOK_TILES):
        ref[pl.ds(k, rows, stride=TOK_TILES), :] = val[:, k * LANE:(k + 1) * LANE]


def _router_logits(h, wr_ref, br_ref):
    lg = _dot_3pass(h, wr_ref[...]) + br_ref[...]
    lane = lax.broadcasted_iota(I32, lg.shape, 1).astype(F32)
    return lg, lane


def _top_group(lg, lane):
    gl = jnp.where(lane < N_GROUPS, lg, NEG_INF)
    gmax = jnp.max(gl, axis=-1, keepdims=True)
    gsel = jnp.min(jnp.where(gl == gmax, lane, float(LANE)), axis=-1, keepdims=True)
    p_group = 1.0 / jnp.sum(jnp.exp(gl - gmax), axis=-1, keepdims=True)
    return gsel, p_group


def _expert_gates(lg, lane, lo, p_group):
    big = float(LANE)
    el = jnp.where((lane >= lo) & (lane < lo + EXPERTS_PER_GROUP), lg, NEG_INF)
    m1 = jnp.max(el, axis=-1, keepdims=True)
    i1 = jnp.min(jnp.where(el == m1, lane, big), axis=-1, keepdims=True)
    el2 = jnp.where(lane == i1, NEG_INF, el)
    m2 = jnp.max(el2, axis=-1, keepdims=True)
    i2 = jnp.min(jnp.where(el2 == m2, lane, big), axis=-1, keepdims=True)
    t = jnp.exp(m2 - m1)
    w1 = p_group / (1.0 + t)
    w2 = p_group * t / (1.0 + t)
    return jnp.where(lane == i1 - lo, w1, jnp.where(lane == i2 - lo, w2, 0.0))


def _router_kernel(h_ref, wr_ref, br_ref, tril_ref, meta_ref, cnt_ref, carry):
    @pl.when(pl.program_id(0) == 0)
    def _():
        carry[...] = jnp.zeros_like(carry)

    lg, lane = _router_logits(_load_token_rows(h_ref), wr_ref, br_ref)
    gsel, _ = _top_group(lg, lane)
    onehot = jnp.where(lane == gsel, 1.0, 0.0)
    incl = jnp.dot(tril_ref[...], onehot.astype(BF16), preferred_element_type=F32)
    rank = jnp.sum(onehot * (incl - 1.0 + carry[...]), axis=-1, keepdims=True)
    carry[...] = carry[...] + jnp.sum(onehot, axis=0, keepdims=True)
    meta_ref[...] = jnp.where(lane == 0, gsel, jnp.where(lane == 1, rank, 0.0)).astype(I32)
    cnt_ref[...] = carry[...].astype(I32)


def _router_weights(w_rg, b_rg, w_re, b_re):
    d = w_rg.shape[0]
    ne = N_GROUPS * EXPERTS_PER_GROUP
    wr = jnp.zeros((d, LANE), F32).at[:, 0:N_GROUPS].set(w_rg).at[:, N_GROUPS:N_GROUPS + ne].set(w_re)
    br = jnp.zeros((1, LANE), F32).at[0, 0:N_GROUPS].set(b_rg).at[0, N_GROUPS:N_GROUPS + ne].set(b_re)
    return wr, br


def _router(h2t, wr, br):
    t = h2t.shape[0] // TOK_TILES
    rb = ROW_BLOCK
    tril = jnp.asarray(np.tril(np.ones((rb, rb), np.float32)), BF16)
    full = lambda a: pl.BlockSpec(a.shape, lambda i: (0,) * a.ndim)
    return pl.pallas_call(
        _router_kernel,
        grid=(t // rb,),
        in_specs=[pl.BlockSpec((rb * TOK_TILES, LANE), lambda i: (i, 0)), full(wr), full(br), full(tril)],
        out_specs=[pl.BlockSpec((rb, LANE), lambda i: (i, 0)), pl.BlockSpec((1, LANE), lambda i: (0, 0))],
        out_shape=[jax.ShapeDtypeStruct((t, LANE), I32), jax.ShapeDtypeStruct((1, LANE), I32)],
        scratch_shapes=[pltpu.VMEM((1, LANE), F32)],
        compiler_params=_cparams(("arbitrary",)),
        name="moe_router",
    )(h2t, wr, br, tril)


def _invert_kernel(dest_ref, inv_ref, *, n_tok):
    def spare(s, c):
        inv_ref[s] = n_tok + (s & (ROW_BLOCK - 1))
        return c

    lax.fori_loop(0, inv_ref.shape[0], spare, 0, unroll=8)

    def put(t, c):
        inv_ref[dest_ref[t]] = t
        return c

    lax.fori_loop(0, n_tok, put, 0, unroll=8)


def _invert(dest, n_slots):
    smem = pl.BlockSpec(memory_space=pltpu.SMEM)
    return pl.pallas_call(
        functools.partial(_invert_kernel, n_tok=dest.shape[0]), in_specs=[smem], out_specs=smem,
        out_shape=jax.ShapeDtypeStruct((n_slots,), I32), name="moe_invert",
    )(dest)


def _token_copy(src_ref, dst_ref, s, d, sem):
    s8 = pl.multiple_of(s * TOK_TILES, TOK_TILES)
    d8 = pl.multiple_of(d * TOK_TILES, TOK_TILES)
    return pltpu.make_async_copy(src_ref.at[pl.ds(s8, TOK_TILES), :], dst_ref.at[pl.ds(d8, TOK_TILES), :], sem)


def _experts_kernel(bg_ref, inv_ref, h_ref, wr_ref, br_ref, wgu_ref, wdn_ref, y_ref, xbuf, ybuf, gsem, ssem, *, n_tok):
    i = pl.program_id(0)
    n_steps = pl.num_programs(0)
    rb, d = ROW_BLOCK, D_MODEL
    slot = i % 2
    other = 1 - slot
    nxt = jnp.minimum(i + 1, n_steps - 1)
    prev = jnp.maximum(i - 1, 0)
    block_rows = rb * TOK_TILES

    def gather_start(blk, sl, r):
        tok = jnp.minimum(inv_ref[blk * rb + r], n_tok - 1)
        _token_copy(h_ref, xbuf.at[sl], tok, r, gsem.at[sl]).start()

    def scatter_start(blk, sl, r, to_spare):
        dst = jnp.where(to_spare, n_tok + r, inv_ref[blk * rb + r])
        _token_copy(ybuf.at[sl], y_ref, r, dst, ssem.at[sl]).start()

    def gather_wait(sl):
        pltpu.make_async_copy(h_ref.at[pl.ds(0, block_rows), :], xbuf.at[sl], gsem.at[sl]).wait()

    def scatter_wait(sl):
        pltpu.make_async_copy(ybuf.at[sl], y_ref.at[pl.ds(0, block_rows), :], ssem.at[sl]).wait()

    @pl.when(i == 0)
    def _():
        ybuf[...] = jnp.zeros_like(ybuf)

        def one(r, c):
            gather_start(0, 0, r)
            return c
        lax.fori_loop(0, rb, one, 0, unroll=8)

    gather_wait(slot)

    @pl.when(i >= 1)
    def _():
        scatter_wait(slot)

    xf = _load_token_rows(xbuf.at[slot])
    lg, lane = _router_logits(xf, wr_ref, br_ref)
    _, p_group = _top_group(lg, lane)
    lo = (N_GROUPS + bg_ref[i] * EXPERTS_PER_GROUP).astype(F32)
    gates = _expert_gates(lg, lane, lo, p_group)
    x = xf.astype(BF16)
    per_expert = rb // EXPERTS_PER_GROUP
    hidden = []
    for e in range(EXPERTS_PER_GROUP):
        gu = jnp.dot(x, wgu_ref[0, e], preferred_element_type=F32)
        g = gu[:, 0:D_EXPERT]
        hidden.append((g * _sigmoid(g) * gu[:, D_EXPERT:] * gates[:, e:e + 1]).astype(BF16))
        for r in range(e * per_expert, (e + 1) * per_expert):
            gather_start(nxt, other, r)
            scatter_start(prev, other, r, i == 0)
    y = jnp.dot(jnp.concatenate(hidden, axis=1), wdn_ref[0], preferred_element_type=F32)
    _store_token_rows(ybuf.at[slot], y)

    @pl.when(i == n_steps - 1)
    def _():
        scatter_wait(other)

        def one(r, c):
            scatter_start(i, slot, r, False)
            return c
        lax.fori_loop(0, rb, one, 0, unroll=8)
        scatter_wait(slot)
        gather_wait(other)


def _experts(block_group, inv, h2t, wr, br, w_gu, w_dn):
    n_blocks = block_group.shape[0]
    rb, d = ROW_BLOCK, D_MODEL
    n_tok = h2t.shape[0] // TOK_TILES
    wgu = w_gu.reshape(N_GROUPS, EXPERTS_PER_GROUP, d, 2 * D_EXPERT)
    wdn = w_dn.reshape(N_GROUPS, EXPERTS_PER_GROUP * D_EXPERT, d)
    any_spec = pl.BlockSpec(memory_space=pl.ANY)
    return pl.pallas_call(
        functools.partial(_experts_kernel, n_tok=n_tok),
        grid_spec=pltpu.PrefetchScalarGridSpec(
            num_scalar_prefetch=2, grid=(n_blocks,),
            in_specs=[
                any_spec,
                pl.BlockSpec(wr.shape, lambda i, bg, inv_: (0, 0)),
                pl.BlockSpec(br.shape, lambda i, bg, inv_: (0, 0)),
                pl.BlockSpec((1, EXPERTS_PER_GROUP, d, 2 * D_EXPERT), lambda i, bg, inv_: (bg[i], 0, 0, 0)),
                pl.BlockSpec((1, EXPERTS_PER_GROUP * D_EXPERT, d), lambda i, bg, inv_: (bg[i], 0, 0)),
            ],
            out_specs=any_spec,
            scratch_shapes=[pltpu.VMEM((2, rb * TOK_TILES, LANE), F32), pltpu.VMEM((2, rb * TOK_TILES, LANE), F32),
                            pltpu.SemaphoreType.DMA((2,)), pltpu.SemaphoreType.DMA((2,))]),
        out_shape=jax.ShapeDtypeStruct(((n_tok + rb) * TOK_TILES, LANE), F32),
        compiler_params=pltpu.CompilerParams(dimension_semantics=("arbitrary",), vmem_limit_bytes=VMEM_LIMIT,
                                             has_side_effects=True),
        name="moe_experts",
    )(block_group, inv, h2t, wr, br, wgu, wdn)


def _moe(h2t, w_rg, b_rg, w_re, b_re, w_gu_bf16, w_dn_bf16):
    t = h2t.shape[0] // TOK_TILES
    rb = ROW_BLOCK
    wr, br = _router_weights(w_rg, b_rg, w_re, b_re)
    meta, counts = _router(h2t, wr, br)
    group, rank = meta[:, 0], meta[:, 1]
    cnt = counts[0, 0:N_GROUPS]
    padded = (cnt + rb - 1) // rb * rb
    seg_end = jnp.cumsum(padded)
    seg_start = seg_end - padded
    dest = seg_start[group] + rank
    n_blocks = t // rb + N_GROUPS
    block_start = jnp.arange(n_blocks, dtype=I32) * rb
    block_group = jnp.minimum(jnp.sum((block_start[:, None] >= seg_end[None, :]).astype(I32), axis=1), N_GROUPS - 1)
    inv = _invert(dest, n_blocks * rb)
    return _experts(block_group, inv, h2t, wr, br, w_gu_bf16, w_dn_bf16)


def _final_kernel(x_ref, y_ref, m_ref, g_ref, o_ref, *, nblk):
    b = pl.program_id(0) // nblk
    d = D_MODEL
    m = _mod_row(m_ref, b)
    x = x_ref[...] + m[:, 5 * d:6 * d] * _load_token_rows(y_ref)
    o_ref[...] = _rms(x, d) * g_ref[...]


def _final(x1, y, m, g, nblk):
    t, d = x1.shape
    rb = ROW_BLOCK
    row = pl.BlockSpec((rb, d), lambda i: (i, 0))
    tok = pl.BlockSpec((rb * TOK_TILES, LANE), lambda i: (i, 0))
    full = lambda a: pl.BlockSpec(a.shape, lambda i: (0,) * a.ndim)
    return pl.pallas_call(
        functools.partial(_final_kernel, nblk=nblk),
        grid=(t // rb,),
        in_specs=[row, tok, full(m), full(g)],
        out_specs=row,
        out_shape=jax.ShapeDtypeStruct((t, d), F32),
        compiler_params=_cparams(("arbitrary",)),
        name="final_norm",
    )(x1, y, m, g)


def _inproj_weight(w_in):
    d = w_in.shape[0]
    z = lambda n: jnp.zeros((d, n), w_in.dtype)
    perm = _rope_swap_perm()
    kr = w_in[:, 3168:3200]
    cols = [w_in[:, 0:2048], w_in[:, 2048:2848], z(W_C - 800),
            w_in[:, 2848:3040], z(256 - D_Q_RANK), w_in[:, 3040:3168], kr, kr[:, perm], z(LANE - 2 * D_ROPE)]
    return jnp.concatenate(cols, axis=1).astype(BF16)


def _hgrn_lower_bounds(logits):
    cum = jnp.cumsum(jax.nn.softmax(logits.astype(F32), axis=0), axis=0)
    return cum - cum[0]


def kernel(x, c, ctx, c_ctx, w_mod, b_mod, norm1_g, norm2_g, w_in, w_out, hgrn_lb_logits, hgrn_norm_g, na_rpb, gla_wg_f, gla_bg_f, gla_wg_b, gla_bg_b, gla_norm_g, mla_q_norm_g, mla_w_uq, mla_kv_norm_g, mla_w_ukv, moe_w_rg, moe_b_rg, moe_w_re, moe_b_re, moe_w_gu, moe_w_dn, final_norm_g):
    batch, n, d = x.shape
    l_ctx = ctx.shape[1]
    assert d == D_MODEL and l_ctx == ROW_BLOCK and n % ROW_BLOCK == 0 and batch <= 4
    s_len = l_ctx + n
    nblk = s_len // ROW_BLOCK
    depth = w_mod.shape[0]

    c8 = jnp.zeros((8, d), F32).at[0:batch].set(c).at[4].set(c_ctx)
    mods = _mod_vectors(c8, w_mod, b_mod)
    lower_bounds = _hgrn_lower_bounds(hgrn_lb_logits)
    tables = _mla_tables(n, l_ctx)

    xa = y_prev = None
    for layer in range(depth):
        keep_ctx = layer < depth - 1
        m = mods[layer]
        stream = (xa, y_prev, mods[layer - 1]) if layer else (ctx, x)
        xa, (pa, pb, pc, pd) = _inproj(stream, m, norm1_g[layer].reshape(1, d), _inproj_weight(w_in[layer]), nblk)
        oaf, oab = _hgrn_scan(pa, lower_bounds[layer], batch, nblk)
        ob = _neighbourhood_attention(pb, na_rpb[layer], batch, s_len, l_ctx, keep_ctx)
        ogf, ogb = _gla_scan(pc, gla_wg_f[layer], gla_bg_f[layer], gla_wg_b[layer], gla_bg_b[layer], batch, nblk)
        mla_w = _mla_weights(mla_q_norm_g[layer], mla_w_uq[layer], mla_kv_norm_g[layer], mla_w_ukv[layer])
        q, k, v = _mla_prep(pd, mla_w, tables, nblk)
        od = _mla_attention(q, k, v, batch, s_len, l_ctx, keep_ctx)
        xa, h2 = _outproj(oaf, oab, pa, ob, ogf, ogb, pc, od, xa, m, hgrn_norm_g[layer].reshape(1, -1),
                          gla_norm_g[layer].reshape(1, -1), norm2_g[layer].reshape(1, d), w_out[layer],
                          batch, nblk, keep_ctx)
        y_prev = _moe(h2, moe_w_rg[layer], moe_b_rg[layer], moe_w_re[layer], moe_b_re[layer],
                      moe_w_gu[layer].astype(BF16), moe_w_dn[layer].astype(BF16))
    out = _final(xa, y_prev, mods[depth - 1], final_norm_g.reshape(1, d), n // ROW_BLOCK)
    return out.reshape(batch, n, d)
```

```python
import functools

import numpy as np
import jax
import jax.numpy as jnp
from jax import lax
from jax.experimental import pallas as pl
from jax.experimental.pallas import tpu as pltpu

F32 = jnp.float32
BF16 = jnp.bfloat16
I32 = jnp.int32

D_MODEL = 1024
DEPTH = 2
GRID_W = 64
EPS = 1e-6
D_GROUP = 256
N_HEADS = 4
A_DH = 64
B_DH = 64
WIN_H = 8
WIN_W = 16
C_DK = 32
C_DV = 64
C_GATE_RANK = 16
C_GATE_NORM = 16.0
D_NOPE = 64
D_V = 64
D_ROPE = 32
ROPE_FREQS = 8
ROPE_BASE = 10000.0
D_Q_RANK = 192
D_KV_RANK = 128
MLA_SCALE = (D_NOPE + D_ROPE) ** -0.5
N_GROUPS = 4
EXPERTS_PER_GROUP = 8
D_EXPERT = 256

ROW_BLOCK = 256
SUB = 16
MACRO = 64
DECAY_GUARD = 60.0
N_SCAN_CONSTS = 9
SCAN_BATCHES = 2
LANE = 128
HEAD_TILE = 128
VT_ROWS = 80
W_A, W_B, W_C, W_D = 1280, 768, 896, 512
TOK_TILES = D_MODEL // LANE
VMEM_LIMIT = 52 * 1024 * 1024
NEG_INF = float("-inf")


def _bdot(a, b):
    return jnp.dot(a.astype(BF16), b.astype(BF16), preferred_element_type=F32)


def _bdot_nt(a, b):
    return lax.dot_general(a.astype(BF16), b.astype(BF16), (((1,), (1,)), ((), ())), preferred_element_type=F32)


def _bdot_tn(a, b):
    return lax.dot_general(a.astype(BF16), b.astype(BF16), (((0,), (0,)), ((), ())), preferred_element_type=F32)


def _split3(a):
    hi = a.astype(BF16)
    r1 = a - hi.astype(F32)
    mid = r1.astype(BF16)
    lo = (r1 - mid.astype(F32)).astype(BF16)
    return hi, mid, lo


def _dot_f32(a, b):
    ah, am, al = _split3(a)
    bh, bm, bl = _split3(b)
    d = lambda u, v: jnp.dot(u, v, preferred_element_type=F32)
    return d(ah, bh) + (d(ah, bm) + d(am, bh)) + (d(am, bm) + d(ah, bl) + d(al, bh))


def _dot_3pass(a, b):
    ah, am, _ = _split3(a)
    bh, bm, _ = _split3(b)
    d = lambda u, v: jnp.dot(u, v, preferred_element_type=F32)
    return d(ah, bh) + (d(ah, bm) + d(am, bh))


def _dot_sel_l(sel, a):
    ah, am, al = _split3(a)
    d = lambda v: jnp.dot(sel, v, preferred_element_type=F32)
    return d(ah) + d(am) + d(al)


def _dot_sel_r(a, sel):
    ah, am, al = _split3(a)
    d = lambda u: jnp.dot(u, sel, preferred_element_type=F32)
    return d(ah) + d(am) + d(al)


def _sigmoid(x):
    return 1.0 / (1.0 + jnp.exp(-x))


def _log_sigmoid(x):
    return jnp.minimum(x, 0.0) - jnp.log1p(jnp.exp(-jnp.abs(x)))


def _logaddexp(a, b):
    amax = jnp.maximum(a, b)
    delta = a - b
    return jnp.where(jnp.isnan(delta), a + b, amax + jnp.log1p(jnp.exp(-jnp.abs(delta))))


def _rms(x, width):
    return x * lax.rsqrt(jnp.sum(x * x, axis=-1, keepdims=True) / width + EPS)


def _cparams(sem, vmem=VMEM_LIMIT):
    return pltpu.CompilerParams(dimension_semantics=sem, vmem_limit_bytes=vmem)


def _mod_kernel(c_ref, w_ref, b_ref, o_ref):
    c = c_ref[...]
    act = c * _sigmoid(c)
    o_ref[0] = _dot_f32(act, w_ref[0]) + b_ref[0]


def _mod_vectors(c8, w_mod, b_mod):
    depth, d, six_d = w_mod.shape
    nj = six_d // d
    return pl.pallas_call(
        _mod_kernel,
        grid=(depth, nj),
        in_specs=[
            pl.BlockSpec((8, d), lambda l, j: (0, 0)),
            pl.BlockSpec((1, d, d), lambda l, j: (l, 0, j)),
            pl.BlockSpec((1, 1, d), lambda l, j: (l, 0, j)),
        ],
        out_specs=pl.BlockSpec((1, 8, d), lambda l, j: (l, 0, j)),
        out_shape=jax.ShapeDtypeStruct((depth, 8, six_d), F32),
        compiler_params=_cparams(("arbitrary", "arbitrary")),
        name="mod_vectors",
    )(c8, w_mod, b_mod.reshape(depth, 1, six_d))


def _mod_row(m_ref, row):
    return m_ref[pl.ds(row, 1), :]


def _inproj_kernel(*refs, first, nblk):
    if first:
        ctx_ref, lat_ref, m_ref, g_ref, w_ref, xo_ref, pa_ref, pb_ref, pc_ref, pd_ref = refs
    else:
        x_ref, y_ref, mprev_ref, m_ref, g_ref, w_ref, xo_ref, pa_ref, pb_ref, pc_ref, pd_ref = refs
    i = pl.program_id(0)
    b = i // nblk
    is_ctx = i % nblk == 0
    row = jnp.where(is_ctx, 4, b)
    d = D_MODEL
    if first:
        x = jnp.where(is_ctx, ctx_ref[0], lat_ref[0])
    else:
        mp = _mod_row(mprev_ref, row)
        x = x_ref[...] + mp[:, 5 * d:6 * d] * _load_token_rows(y_ref)
    xo_ref[...] = x
    m = _mod_row(m_ref, row)
    h = _rms(x, d) * g_ref[...] * (1.0 + m[:, d:2 * d]) + m[:, 0:d]
    p = _bdot(h, w_ref[...])
    pa_ref[...] = p[:, 0:W_A]
    pb_ref[...] = p[:, W_A:W_A + W_B].astype(pb_ref.dtype)
    pc_ref[...] = p[:, W_A + W_B:W_A + W_B + W_C]
    pd_ref[...] = p[:, W_A + W_B + W_C:]


def _inproj(stream, m, g, w, nblk):
    first = len(stream) == 2
    rb, d = ROW_BLOCK, D_MODEL
    row_spec = lambda w_: pl.BlockSpec((rb, w_), lambda i: (i, 0))
    full = lambda a: pl.BlockSpec(a.shape, lambda i: (0,) * a.ndim)
    if first:
        ctx, lat = stream
        t = ctx.shape[0] * nblk * rb
        specs = [pl.BlockSpec((1, rb, d), lambda i: (i // nblk, 0, 0)),
                 pl.BlockSpec((1, rb, d), lambda i: (i // nblk, jnp.maximum(i % nblk - 1, 0), 0))]
    else:
        t = stream[0].shape[0]
        specs = [row_spec(d), pl.BlockSpec((rb * TOK_TILES, LANE), lambda i: (i, 0)), full(stream[2])]
    ins = list(stream) + [m, g, w]
    specs += [full(m), full(g), full(w)]
    outs, ospecs = [jax.ShapeDtypeStruct((t, d), F32)], [row_spec(d)]
    for w_ in (W_A, W_B, W_C, W_D):
        outs.append(jax.ShapeDtypeStruct((t, w_), BF16 if w_ == W_B else F32))
        ospecs.append(row_spec(w_))
    res = pl.pallas_call(
        functools.partial(_inproj_kernel, first=first, nblk=nblk),
        grid=(t // rb,),
        in_specs=specs,
        out_specs=ospecs,
        out_shape=outs,
        compiler_params=_cparams(("arbitrary",)),
        name="in_projection",
    )(*ins)
    return res[0], res[1:]


def _sub_chunk(refs, i, tri, emat, emask_t, reverse):
    q_ref, k_ref, la_ref, v_ref, o_ref, st_ref, r_ref = refs
    hv = v_ref.shape[1]
    row_id = lax.broadcasted_iota(I32, (SUB, 1), 0)
    off = pl.multiple_of(i * SUB, SUB)
    qs = q_ref[pl.ds(off, SUB), :]
    ks = k_ref[pl.ds(off, SUB), :]
    las = la_ref[pl.ds(off, SUB), :]
    vs = v_ref[pl.ds(off, SUB), :]
    cum = _dot_sel_l(tri, las)
    last = cum[0:1, :] if reverse else cum[SUB - 1:SUB, :]
    for j in range(SUB):
        valid = (row_id <= j) if reverse else (row_id >= j)
        dlt = jnp.where(valid, cum - cum[j:j + 1, :], NEG_INF)
        r_ref[j * SUB:(j + 1) * SUB, :] = (qs * ks[j:j + 1, :] * jnp.exp(dlt)).astype(BF16)
    att = jnp.dot(r_ref[...], emat, preferred_element_type=F32)
    o = jnp.zeros((SUB, hv), F32)
    for j in range(SUB):
        o = o + att[j * SUB:(j + 1) * SUB, :] * vs[j:j + 1, :]
    st = st_ref[...]
    o = o + _bdot_nt(qs * jnp.exp(cum), st)
    o_ref[pl.ds(off, SUB), :] = o
    kd = ks * jnp.exp(last - cum)
    st_ref[...] = st * jnp.exp(last) + _bdot_tn(vs, kd) * emask_t


def _macro_step(refs, m, tri, emask_t, hmk, hmv, reverse):
    q_ref, k_ref, la_ref, v_ref, o_ref, st_ref, _ = refs
    hv = v_ref.shape[1]
    n_sub = MACRO // SUB
    off = pl.multiple_of(m * MACRO, MACRO)
    q = q_ref[pl.ds(off, MACRO), :]
    k = k_ref[pl.ds(off, MACRO), :]
    v = v_ref[pl.ds(off, MACRO), :]
    cum = _dot_sel_l(tri, la_ref[pl.ds(off, MACRO), :])
    last = cum[0:1, :] if reverse else cum[MACRO - 1:MACRO, :]
    st = st_ref[...]
    o_inter = _bdot_nt(q * jnp.exp(cum), st)
    st_ref[...] = st * jnp.exp(last) + _bdot_tn(v, k * jnp.exp(last - cum)) * emask_t
    vb = v.astype(BF16)
    row_i = lax.broadcasted_iota(I32, (N_HEADS * SUB, 1), 0) % SUB
    for s in range(n_sub):
        lo = s * SUB
        if reverse:
            k_lo, k_hi = lo, MACRO
            ref = cum[lo + SUB:lo + SUB + 1, :] if s < n_sub - 1 else jnp.zeros_like(last)
        else:
            k_lo, k_hi = 0, lo + SUB
            ref = cum[lo - 1:lo, :] if s > 0 else jnp.zeros_like(last)
        qt = q[lo:lo + SUB, :] * jnp.exp(cum[lo:lo + SUB, :] - ref)
        qs = jnp.concatenate([qt * hmk[h:h + 1, :] for h in range(N_HEADS)], axis=0).astype(BF16)
        kt = (k[k_lo:k_hi, :] * jnp.exp(ref - cum[k_lo:k_hi, :])).astype(BF16)
        att = lax.dot_general(qs, kt, (((1,), (1,)), ((), ())), preferred_element_type=F32)
        col = lax.broadcasted_iota(I32, (1, k_hi - k_lo), 1) + k_lo
        valid = (col >= lo + row_i) if reverse else (col <= lo + row_i)
        att = jnp.where(valid, att, 0.0)
        o_heads = jnp.dot(att.astype(BF16), vb[k_lo:k_hi, :], preferred_element_type=F32)
        o = o_inter[lo:lo + SUB, :]
        for h in range(N_HEADS):
            o = o + o_heads[h * SUB:(h + 1) * SUB, :] * hmv[h:h + 1, :]
        o_ref[pl.ds(off + lo, SUB), :] = o


def _scan_chains(chains, consts):
    tri_f, tri_b, trim_f, trim_b, emat, emask_t, hmk, hmv, sub_sum = consts
    rows = chains[0][0][0].shape[0]
    n_sub, n_macro = rows // SUB, rows // MACRO
    tot = None
    for refs, _ in chains:
        block_tot = jnp.dot(sub_sum, refs[2][...].astype(BF16), preferred_element_type=F32)
        tot = block_tot if tot is None else jnp.minimum(tot, block_tot)
    factorisable = jnp.min(tot) > -DECAY_GUARD

    @pl.when(factorisable)
    def _():
        def body(step, carry):
            for refs, rev in chains:
                _macro_step(refs, n_macro - 1 - step if rev else step, trim_b if rev else trim_f, emask_t, hmk, hmv, rev)
            return carry

        lax.fori_loop(0, n_macro, body, 0, unroll=True)

    @pl.when(jnp.logical_not(factorisable))
    def _():
        def body(step, carry):
            for refs, rev in chains:
                _sub_chunk(refs, n_sub - 1 - step if rev else step, tri_b if rev else tri_f, emat, emask_t, rev)
            return carry

        lax.fori_loop(0, n_sub, body, 0)


def _zero_at_first_block(*state_refs):
    @pl.when(pl.program_id(1) == 0)
    def _():
        for st in state_refs:
            st[...] = jnp.zeros_like(st)


def _scan_batches(batch):
    return SCAN_BATCHES if batch % SCAN_BATCHES == 0 else 1


def _scan_consts(hk, hv, dk, dv):
    tri = lambda n, low: jnp.asarray(np.tril(np.ones((n, n), np.float32)) if low else np.triu(np.ones((n, n), np.float32)), BF16)
    em = _head_match(hk, hv, dk, dv)
    heads = np.arange(N_HEADS)[:, None]
    hmk = (np.arange(hk)[None, :] // dk == heads).astype(np.float32)
    hmv = (np.arange(hv)[None, :] // dv == heads).astype(np.float32)
    sub_sum = (np.arange(ROW_BLOCK)[None, :] // SUB == np.arange(ROW_BLOCK // SUB)[:, None]).astype(np.float32)
    return [tri(SUB, True), tri(SUB, False), tri(MACRO, True), tri(MACRO, False), jnp.asarray(em, BF16),
            jnp.asarray(em.T, F32), jnp.asarray(hmk), jnp.asarray(hmv), jnp.asarray(sub_sum, BF16)]


def _hgrn_prep(q_ref, v_ref, f_ref, lb, qo, ko, lo, vo):
    qr = q_ref[...]
    qo[...] = qr * _sigmoid(qr) * (A_DH ** -0.5)
    vo[...] = v_ref[...]
    z = f_ref[...]
    lo[...] = _logaddexp(jnp.log(lb), jnp.log1p(-lb) + _log_sigmoid(z))
    ko[...] = (1.0 - lb) * _sigmoid(-z)


def _hgrn_kernel(qf_ref, vf_ref, ff_ref, qb_ref, vb_ref, fb_ref, lb_ref, *rest):
    consts, (of_ref, ob_ref), scratch = rest[:N_SCAN_CONSTS], rest[N_SCAN_CONSTS:N_SCAN_CONSTS + 2], rest[N_SCAN_CONSTS + 2:]
    chains = []
    for s in range(qf_ref.shape[0]):
        stf, stb, rf, rb_, q1, k1, l1, v1, q2, k2, l2, v2 = scratch[12 * s:12 * s + 12]
        _zero_at_first_block(stf, stb)
        _hgrn_prep(qf_ref.at[s], vf_ref.at[s], ff_ref.at[s], lb_ref[0:1, :], q1, k1, l1, v1)
        _hgrn_prep(qb_ref.at[s], vb_ref.at[s], fb_ref.at[s], lb_ref[1:2, :], q2, k2, l2, v2)
        chains += [((q1, k1, l1, v1, of_ref.at[s], stf, rf), False), ((q2, k2, l2, v2, ob_ref.at[s], stb, rb_), True)]
    _scan_chains(chains, [c[...] for c in consts])


def _head_match(hk, hv, dk, dv):
    m = (np.arange(hk)[:, None] // dk == np.arange(hv)[None, :] // dv).astype(np.float32)
    return m


def _bwd_block(t, nblk):
    return jnp.where(t == 0, 0, nblk - t)


def _hgrn_scan(pa, lb, batch, nblk):
    rb, w = ROW_BLOCK, D_GROUP
    consts = _scan_consts(w, w, A_DH, A_DH)
    nb = _scan_batches(batch)
    pa3 = pa.reshape(batch, nblk * rb, pa.shape[1])
    fwd = lambda col: pl.BlockSpec((nb, rb, w), lambda b, t: (b, t, col))
    bwd = lambda col: pl.BlockSpec((nb, rb, w), lambda b, t: (b, _bwd_block(t, nblk), col))
    full = lambda a: pl.BlockSpec(a.shape, lambda b, t: (0,) * a.ndim)
    vm = lambda shape, dt=F32: pltpu.VMEM(shape, dt)
    of, ob = pl.pallas_call(
        _hgrn_kernel,
        grid=(batch // nb, nblk),
        in_specs=[fwd(0), fwd(1), fwd(2), bwd(0), bwd(1), bwd(3), full(lb)] + [full(c) for c in consts],
        out_specs=[fwd(0), bwd(0)],
        out_shape=[jax.ShapeDtypeStruct((batch, nblk * rb, w), F32)] * 2,
        scratch_shapes=([vm((w, w)), vm((w, w)), vm((SUB * SUB, w), BF16), vm((SUB * SUB, w), BF16)]
                        + [vm((rb, w))] * 8) * nb,
        compiler_params=_cparams(("arbitrary", "arbitrary")),
        name="hgrn_scan",
    )(pa3, pa3, pa3, pa3, pa3, pa3, lb, *consts)
    return of.reshape(-1, w), ob.reshape(-1, w)


def _gla_prep(q_ref, k_ref, v_ref, z_ref, wg_ref, bg_ref, qo, ko, lo, vo):
    qo[...] = q_ref[...] * (C_DK ** -0.5)
    ko[...] = k_ref[...]
    vo[...] = v_ref[...]
    zl = _dot_f32(z_ref[...], wg_ref[...]) + bg_ref[...]
    lo[...] = _log_sigmoid(zl) / C_GATE_NORM


def _gla_kernel(qf_ref, kf_ref, vf_ref, zf_ref, qb_ref, kb_ref, vb_ref, zb_ref, wgf_ref, bgf_ref, wgb_ref, bgb_ref, *rest):
    consts, (of_ref, ob_ref), scratch = rest[:N_SCAN_CONSTS], rest[N_SCAN_CONSTS:N_SCAN_CONSTS + 2], rest[N_SCAN_CONSTS + 2:]
    chains = []
    for s in range(qf_ref.shape[0]):
        stf, stb, rf, rb_, q1, k1, l1, v1, q2, k2, l2, v2 = scratch[12 * s:12 * s + 12]
        _zero_at_first_block(stf, stb)
        _gla_prep(qf_ref.at[s], kf_ref.at[s], vf_ref.at[s], zf_ref.at[s], wgf_ref, bgf_ref, q1, k1, l1, v1)
        _gla_prep(qb_ref.at[s], kb_ref.at[s], vb_ref.at[s], zb_ref.at[s], wgb_ref, bgb_ref, q2, k2, l2, v2)
        chains += [((q1, k1, l1, v1, of_ref.at[s], stf, rf), False), ((q2, k2, l2, v2, ob_ref.at[s], stb, rb_), True)]
    _scan_chains(chains, [c[...] for c in consts])


def _gla_scan(pc, wg_f, bg_f, wg_b, bg_b, batch, nblk):
    rb = ROW_BLOCK
    hk, hv = N_HEADS * C_DK, N_HEADS * C_DV
    consts = _scan_consts(hk, hv, C_DK, C_DV)
    wgf = jnp.zeros((LANE, hk), F32).at[0:C_GATE_RANK].set(wg_f)
    wgb = jnp.zeros((LANE, hk), F32).at[C_GATE_RANK:2 * C_GATE_RANK].set(wg_b)
    bgf, bgb = bg_f.reshape(1, hk), bg_b.reshape(1, hk)
    nb = _scan_batches(batch)
    pc3 = pc.reshape(batch, nblk * rb, pc.shape[1])
    fwd = lambda w, col: pl.BlockSpec((nb, rb, w), lambda b, t: (b, t, col))
    bwd = lambda w, col: pl.BlockSpec((nb, rb, w), lambda b, t: (b, _bwd_block(t, nblk), col))
    full = lambda a: pl.BlockSpec(a.shape, lambda b, t: (0,) * a.ndim)
    vm = lambda shape, dt=F32: pltpu.VMEM(shape, dt)
    of, ob = pl.pallas_call(
        _gla_kernel,
        grid=(batch // nb, nblk),
        in_specs=[fwd(hk, 0), fwd(hk, 1), fwd(hv, 1), fwd(LANE, 6), bwd(hk, 0), bwd(hk, 1), bwd(hv, 1), bwd(LANE, 6),
                  full(wgf), full(bgf), full(wgb), full(bgb)] + [full(c) for c in consts],
        out_specs=[fwd(hv, 0), bwd(hv, 0)],
        out_shape=[jax.ShapeDtypeStruct((batch, nblk * rb, hv), F32)] * 2,
        scratch_shapes=([vm((hv, hk)), vm((hv, hk)), vm((SUB * SUB, hk), BF16), vm((SUB * SUB, hk), BF16)]
                        + [vm((rb, hk)), vm((rb, hk)), vm((rb, hk)), vm((rb, hv))] * 2) * nb,
        compiler_params=_cparams(("arbitrary", "arbitrary")),
        name="gla_scan",
    )(pc3, pc3, pc3, pc3, pc3, pc3, pc3, pc3, wgf, bgf, wgb, bgb, *consts)
    return of.reshape(-1, hv), ob.reshape(-1, hv)


NA_QROWS = ROW_BLOCK // GRID_W
NA_WROWS = WIN_H + NA_QROWS


def _na_window_rows(rows):
    kh = WIN_H
    patterns = []
    for r0 in (0, NA_QROWS, rows - NA_QROWS):
        ws = int(np.clip(r0 - kh // 2, 0, rows - NA_WROWS))
        seen = {}
        for a in range(NA_QROWS):
            r = r0 + a
            s = int(np.clip(r - kh // 2, 0, rows - kh))
            for jj in range(NA_WROWS):
                if s <= ws + jj < s + kh:
                    seen[(a, jj)] = ws + jj - r + kh - 1
        patterns.append(seen)
    return patterns


def _na_kernel(q_ref, k_ref, v_ref, by_col_ref, hm_ref, o_ref, bias_scr, *, j0, rows, ctx):
    j = pl.program_id(1) + j0
    q = q_ref[...] * (B_DH ** -0.5)
    hm = hm_ref[...]
    kc = k_ref[0:ctx, :]
    vc = v_ref[0:ctx, :]
    nt = lambda a, b: lax.dot_general(a, b, (((1,), (1,)), ((), ())), preferred_element_type=F32)

    @pl.when(j == 0)
    def _():
        acc = jnp.zeros(q.shape, F32)
        for h in range(N_HEADS):
            mh = hm[h:h + 1, :]
            s = nt(q * mh.astype(BF16), kc)
            p = jnp.exp(s - jnp.max(s, axis=-1, keepdims=True))
            inv = 1.0 / jnp.sum(p, axis=-1, keepdims=True)
            acc = acc + jnp.dot(p.astype(BF16), vc, preferred_element_type=F32) * (mh * inv)
        o_ref[...] = acc.astype(o_ref.dtype)

    @pl.when(j > 0)
    def _():
        for position, (at_step, seen) in enumerate(zip((1, 2, rows // NA_QROWS), _na_window_rows(rows))):
            @pl.when(j == at_step)
            def _(seen=seen):
                blocked = jnp.full((GRID_W, GRID_W), NEG_INF, F32)
                for h in range(N_HEADS):
                    for a in range(NA_QROWS):
                        for jj in range(NA_WROWS):
                            slab = by_col_ref[h, seen[(a, jj)]] if (a, jj) in seen else blocked
                            bias_scr[h, a * GRID_W:(a + 1) * GRID_W, jj * GRID_W:(jj + 1) * GRID_W] = slab

        r0 = (j - 1) * NA_QROWS
        start = jnp.clip(r0 - WIN_H // 2, 0, rows - NA_WROWS)
        off = pl.multiple_of(ctx + start * GRID_W, GRID_W)
        kw = k_ref[pl.ds(off, NA_WROWS * GRID_W), :]
        vw = v_ref[pl.ds(off, NA_WROWS * GRID_W), :]
        acc = jnp.zeros(q.shape, F32)
        for h in range(N_HEADS):
            mh = hm[h:h + 1, :]
            qh = q * mh.astype(BF16)
            sw = nt(qh, kw) + bias_scr[h]
            sc = nt(qh, kc)
            m = jnp.maximum(jnp.max(sw, axis=-1, keepdims=True), jnp.max(sc, axis=-1, keepdims=True))
            pw = jnp.exp(sw - m)
            pc_ = jnp.exp(sc - m)
            inv = 1.0 / (jnp.sum(pw, axis=-1, keepdims=True) + jnp.sum(pc_, axis=-1, keepdims=True))
            o = (jnp.dot(pw.astype(BF16), vw, preferred_element_type=F32)
                 + jnp.dot(pc_.astype(BF16), vc, preferred_element_type=F32))
            acc = acc + o * (mh * inv)
        o_ref[...] = acc.astype(o_ref.dtype)


def _na_bias_by_column(rpb):
    cidx = np.arange(GRID_W)
    c_start = np.clip(cidx - WIN_W // 2, 0, GRID_W - WIN_W)
    col_in = (cidx[None] >= c_start[:, None]) & (cidx[None] < c_start[:, None] + WIN_W)
    dc = np.clip(cidx[None] - cidx[:, None], -(WIN_W - 1), WIN_W - 1) + (WIN_W - 1)
    sel = (dc[None] == np.arange(2 * WIN_W - 1)[:, None, None]).astype(np.float32)
    by_col = jnp.einsum("hrc,cqw->hrqw", rpb.astype(F32), jnp.asarray(sel), precision=lax.Precision.HIGHEST)
    return jnp.where(jnp.asarray(col_in)[None, None], by_col, NEG_INF)


def _head_masks(width, dh):
    return jnp.asarray((np.arange(width)[None, :] // dh == np.arange(N_HEADS)[:, None]).astype(np.float32))


def _neighbourhood_attention(pb, rpb, batch, s_len, ctx, keep_ctx):
    rows = (s_len - ctx) // GRID_W
    assert rows >= NA_WROWS and rows % NA_QROWS == 0 and ctx == ROW_BLOCK
    rb = ROW_BLOCK
    j0 = 0 if keep_ctx else 1
    per_b = s_len // rb
    by_col = _na_bias_by_column(rpb)
    hm = _head_masks(D_GROUP, B_DH)

    return pl.pallas_call(
        functools.partial(_na_kernel, j0=j0, rows=rows, ctx=ctx),
        grid=(batch, per_b - j0),
        in_specs=[
            pl.BlockSpec((rb, D_GROUP), lambda b, jj: (b * per_b + jj + j0, 0)),
            pl.BlockSpec((s_len, D_GROUP), lambda b, jj: (b, 1)),
            pl.BlockSpec((s_len, D_GROUP), lambda b, jj: (b, 2)),
            pl.BlockSpec(by_col.shape, lambda b, jj: (0, 0, 0, 0)),
            pl.BlockSpec(hm.shape, lambda b, jj: (0, 0)),
        ],
        out_specs=pl.BlockSpec((rb, D_GROUP), lambda b, jj: (b * (per_b - j0) + jj, 0)),
        out_shape=jax.ShapeDtypeStruct((batch * (per_b - j0) * rb, D_GROUP), BF16),
        scratch_shapes=[pltpu.VMEM((N_HEADS, rb, NA_WROWS * GRID_W), F32)],
        compiler_params=_cparams(("arbitrary", "arbitrary")),
        name="neighbourhood_attention",
    )(pb, pb, pb, by_col, hm)


def _mla_prep_kernel(pd_ref, gq_ref, gkv_ref, wq1_ref, wq2_ref, wk_ref, wvt_ref, ones_ref, cq_ref, sq_ref, tk_ref,
                     place_ref, q_ref, k_ref, vt_ref):
    pd = pd_ref[...]
    cq = pd[:, 0:256]
    ckv = pd[:, 256:384]
    kr = pd[:, 384:512]
    qn = _rms(cq, D_Q_RANK) * gq_ref[...]
    q = _bdot(qn, wq1_ref[...]) * cq_ref[...] + _bdot(qn, wq2_ref[...]) * sq_ref[...]
    q_ref[...] = q.astype(BF16)
    kvn = (_rms(ckv, D_KV_RANK) * gkv_ref[...]).astype(BF16)
    k = jnp.dot(kvn, wk_ref[...], preferred_element_type=F32) + _dot_sel_r(kr * tk_ref[...], place_ref[...])
    k_ref[...] = k.astype(BF16)
    vt = lax.dot_general(wvt_ref[...], kvn, (((1,), (1,)), ((), ())), preferred_element_type=F32) + ones_ref[...]
    vt_ref[0] = vt.astype(BF16)


def _rope_swap_perm():
    f = ROPE_FREQS
    return np.concatenate([np.arange(f, 2 * f), np.arange(0, f), np.arange(3 * f, 4 * f), np.arange(2 * f, 3 * f)])


def _mla_tables(n, ctx):
    t = np.arange(n)
    inv_freq = ROPE_BASE ** (-np.arange(ROPE_FREQS, dtype=np.float32) / ROPE_FREQS)
    ang_r = (t // GRID_W).astype(np.float32)[:, None] * inv_freq
    ang_c = (t % GRID_W).astype(np.float32)[:, None] * inv_freq
    cos32 = np.concatenate([np.cos(ang_r), np.cos(ang_r), np.cos(ang_c), np.cos(ang_c)], axis=1)
    sin32 = np.concatenate([-np.sin(ang_r), np.sin(ang_r), -np.sin(ang_c), np.sin(ang_c)], axis=1)
    cos32 = np.concatenate([np.ones((ctx, D_ROPE), np.float32), cos32.astype(np.float32)], axis=0)
    sin32 = np.concatenate([np.zeros((ctx, D_ROPE), np.float32), sin32.astype(np.float32)], axis=0)
    s_len = n + ctx
    cq = np.zeros((s_len, N_HEADS, HEAD_TILE), np.float32)
    sq = np.zeros((s_len, N_HEADS, HEAD_TILE), np.float32)
    cq[:, :, 0:D_NOPE] = MLA_SCALE
    cq[:, :, D_NOPE:D_NOPE + D_ROPE] = cos32[:, None, :] * MLA_SCALE
    sq[:, :, D_NOPE:D_NOPE + D_ROPE] = sin32[:, None, :] * MLA_SCALE
    tk = np.zeros((s_len, LANE), np.float32)
    tk[:, 0:D_ROPE] = cos32
    tk[:, D_ROPE:2 * D_ROPE] = sin32
    place = np.zeros((LANE, N_HEADS * HEAD_TILE), np.float32)
    for h in range(N_HEADS):
        for l in range(D_ROPE):
            place[l, h * HEAD_TILE + D_NOPE + l] = 1.0
            place[D_ROPE + l, h * HEAD_TILE + D_NOPE + l] = 1.0
    return (jnp.asarray(cq.reshape(s_len, -1)), jnp.asarray(sq.reshape(s_len, -1)), jnp.asarray(tk),
            jnp.asarray(place, BF16))


def _mla_weights(q_norm_g, w_uq, kv_norm_g, w_ukv):
    perm = _rope_swap_perm()
    wq = w_uq.reshape(D_Q_RANK, N_HEADS, D_NOPE + D_ROPE)
    wq1 = jnp.zeros((256, N_HEADS, HEAD_TILE), F32).at[0:D_Q_RANK, :, 0:D_NOPE + D_ROPE].set(wq)
    wq2 = jnp.zeros((256, N_HEADS, HEAD_TILE), F32).at[0:D_Q_RANK, :, D_NOPE:D_NOPE + D_ROPE].set(
        wq[:, :, D_NOPE:][:, :, perm])
    wkv = w_ukv.reshape(D_KV_RANK, N_HEADS, D_NOPE + D_V)
    wk = jnp.zeros((D_KV_RANK, N_HEADS, HEAD_TILE), F32).at[:, :, 0:D_NOPE].set(wkv[:, :, 0:D_NOPE])
    wvt = jnp.zeros((N_HEADS, VT_ROWS, D_KV_RANK), F32).at[:, 0:D_V, :].set(jnp.transpose(wkv[:, :, D_NOPE:], (1, 2, 0)))
    wvt = wvt.reshape(N_HEADS * VT_ROWS, D_KV_RANK)
    ones_rows = jnp.asarray((np.arange(N_HEADS * VT_ROWS) % VT_ROWS >= D_V).astype(np.float32)).reshape(-1, 1)
    hw = N_HEADS * HEAD_TILE
    gq = jnp.zeros((1, 256), F32).at[0, 0:D_Q_RANK].set(q_norm_g)
    return (gq, kv_norm_g.reshape(1, D_KV_RANK), wq1.reshape(256, hw).astype(BF16), wq2.reshape(256, hw).astype(BF16),
            wk.reshape(D_KV_RANK, hw).astype(BF16), wvt.astype(BF16), ones_rows)


def _mla_prep(pd, weights, tables, nblk):
    t_all = pd.shape[0]
    rb = ROW_BLOCK
    gq, gkv, wq1, wq2, wk, wvt, ones_rows = weights
    cq, sq, tk, place = tables
    hw = N_HEADS * HEAD_TILE
    hv = N_HEADS * VT_ROWS
    row = lambda w: pl.BlockSpec((rb, w), lambda i: (i, 0))
    pos = lambda w: pl.BlockSpec((rb, w), lambda i: (i % nblk, 0))
    full = lambda a: pl.BlockSpec(a.shape, lambda i: (0,) * a.ndim)
    return pl.pallas_call(
        _mla_prep_kernel,
        grid=(t_all // rb,),
        in_specs=[row(W_D), full(gq), full(gkv), full(wq1), full(wq2), full(wk), full(wvt), full(ones_rows),
                  pos(hw), pos(hw), pos(LANE), full(place)],
        out_specs=[row(hw), row(hw), pl.BlockSpec((1, hv, rb), lambda i: (i // nblk, 0, i % nblk))],
        out_shape=[jax.ShapeDtypeStruct((t_all, hw), BF16), jax.ShapeDtypeStruct((t_all, hw), BF16),
                   jax.ShapeDtypeStruct((t_all // (nblk * rb), hv, nblk * rb), BF16)],
        compiler_params=_cparams(("arbitrary",)),
        name="mla_prep",
    )(pd, gq, gkv, wq1, wq2, wk, wvt, ones_rows, cq, sq, tk, place)


MLA_HEADS_PER_STEP = 4
MLA_KEY_CHUNK = 256


def _mla_attn_kernel(q_ref, k_ref, vt_ref, o_ref, s_scr, *, j0, ctx):
    j = pl.program_id(2) + j0
    rb = q_ref.shape[0]

    def heads(n_keys):
        hs = range(MLA_HEADS_PER_STEP)
        chunks = [slice(c * MLA_KEY_CHUNK, (c + 1) * MLA_KEY_CHUNK) for c in range(n_keys // MLA_KEY_CHUNK)]
        lanes = [slice(h * HEAD_TILE, (h + 1) * HEAD_TILE) for h in hs]
        qs = [q_ref[:, lanes[h]] for h in hs]
        m = [jnp.full((1, rb), NEG_INF, F32) for _ in hs]
        for keys in chunks:
            for h in hs:
                st = lax.dot_general(k_ref[keys, lanes[h]], qs[h], (((1,), (1,)), ((), ())),
                                     preferred_element_type=F32)
                s_scr[h, keys, :] = st
                m[h] = jnp.maximum(m[h], jnp.max(st, axis=0, keepdims=True))
        acc = [jnp.zeros((VT_ROWS, rb), F32) for _ in hs]
        for keys in chunks:
            for h in hs:
                pt = jnp.exp(s_scr[h, keys, :] - m[h]).astype(BF16)
                acc[h] = acc[h] + jnp.dot(vt_ref[0, h * VT_ROWS:(h + 1) * VT_ROWS, keys], pt,
                                          preferred_element_type=F32)
        for h in hs:
            o_ref[0, h * D_V:(h + 1) * D_V, :] = (acc[h][0:D_V, :] * (1.0 / acc[h][D_V:D_V + 1, :])).astype(o_ref.dtype)

    @pl.when(j == 0)
    def _():
        heads(ctx)

    @pl.when(j > 0)
    def _():
        heads(k_ref.shape[0])


def _mla_attention(q, k, vt, batch, s_len, ctx, keep_ctx):
    rb = ROW_BLOCK
    nblk = s_len // rb
    j0 = 0 if keep_ctx else 1
    hps = MLA_HEADS_PER_STEP
    return pl.pallas_call(
        functools.partial(_mla_attn_kernel, j0=j0, ctx=ctx),
        grid=(batch, N_HEADS // hps, nblk - j0),
        in_specs=[
            pl.BlockSpec((rb, hps * HEAD_TILE), lambda b, h, jj: (b * nblk + jj + j0, h)),
            pl.BlockSpec((s_len, hps * HEAD_TILE), lambda b, h, jj: (b, h)),
            pl.BlockSpec((1, hps * VT_ROWS, s_len), lambda b, h, jj: (b, h, 0)),
        ],
        out_specs=pl.BlockSpec((1, hps * D_V, rb), lambda b, h, jj: (b * (nblk - j0) + jj, h, 0)),
        out_shape=jax.ShapeDtypeStruct((batch * (nblk - j0), N_HEADS * D_V, rb), BF16),
        scratch_shapes=[pltpu.VMEM((hps, s_len, rb), F32)],
        compiler_params=_cparams(("arbitrary", "arbitrary", "arbitrary")),
        name="mla_attention",
    )(q, k, vt)


def _outproj_kernel(oaf_ref, oab_ref, ga_ref, ob_ref, ogf_ref, ogb_ref, gc_ref, od_ref, x_ref, m_ref, gha_ref, ghc_ref,
                    g2_ref, wa_ref, wb_ref, wc_ref, wd_ref, em_ref, x1_ref, h2_ref, *, j0, nblk_out):
    i = pl.program_id(0)
    b = i // nblk_out
    row = jnp.where(i % nblk_out + j0 == 0, 4, b)
    d = D_MODEL
    em = em_ref[...]

    def readout(o, g_norm, gate, dh):
        ms = _dot_sel_r(o * o, em) / dh
        return o * lax.rsqrt(ms + EPS) * g_norm * (gate * _sigmoid(gate))

    a = readout(oaf_ref[...] + oab_ref[...], gha_ref[...], ga_ref[...], A_DH)
    c = readout(ogf_ref[...] + ogb_ref[...], ghc_ref[...], gc_ref[...], C_DV)
    mix = (_bdot(a, wa_ref[...]) + jnp.dot(ob_ref[...], wb_ref[...], preferred_element_type=F32)
           + _bdot(c, wc_ref[...])
           + lax.dot_general(od_ref[0], wd_ref[...], (((0,), (0,)), ((), ())), preferred_element_type=F32))
    m = _mod_row(m_ref, row)
    x1 = x_ref[...] + m[:, 2 * d:3 * d] * mix
    x1_ref[...] = x1
    _store_token_rows(h2_ref, _rms(x1, d) * g2_ref[...] * (1.0 + m[:, 4 * d:5 * d]) + m[:, 3 * d:4 * d])


def _outproj(oaf, oab, pa, ob, ogf, ogb, pc, od, x, m, gha, ghc, g2, w_out, batch, nblk, keep_ctx):
    rb, d = ROW_BLOCK, D_MODEL
    j0 = 0 if keep_ctx else 1
    nblk_out = nblk - j0
    t_out = batch * nblk_out * rb
    wa = w_out[0:256].astype(BF16)
    wb = w_out[256:512].astype(BF16)
    wc = w_out[512:768].astype(BF16)
    wd = w_out[768:1024].astype(BF16)
    em = jnp.asarray(_head_match(D_GROUP, D_GROUP, 64, 64), BF16)
    src = lambda i: (i // nblk_out) * nblk + i % nblk_out + j0
    row_in = lambda w, col=0: pl.BlockSpec((rb, w), lambda i: (src(i), col))
    row_out = lambda w: pl.BlockSpec((rb, w), lambda i: (i, 0))
    full = lambda a_: pl.BlockSpec(a_.shape, lambda i: (0,) * a_.ndim)
    return pl.pallas_call(
        functools.partial(_outproj_kernel, j0=j0, nblk_out=nblk_out),
        grid=(t_out // rb,),
        in_specs=[row_in(256), row_in(256), row_in(256, 4), row_out(256), row_in(256), row_in(256), row_in(256, 2),
                  pl.BlockSpec((1, N_HEADS * D_V, rb), lambda i: (i, 0, 0)), row_in(d), full(m), full(gha), full(ghc), full(g2),
                  full(wa), full(wb), full(wc), full(wd), full(em)],
        out_specs=[row_out(d), pl.BlockSpec((rb * TOK_TILES, LANE), lambda i: (i, 0))],
        out_shape=[jax.ShapeDtypeStruct((t_out, d), F32), jax.ShapeDtypeStruct((t_out * TOK_TILES, LANE), F32)],
        compiler_params=_cparams(("arbitrary",)),
        name="out_projection",
    )(oaf, oab, pa, ob, ogf, ogb, pc, od, x, m, gha, ghc, g2, wa, wb, wc, wd, em)


def _load_token_rows(ref):
    rows = ref.shape[0] // TOK_TILES
    return jnp.concatenate([ref[pl.ds(k, rows, stride=TOK_TILES), :] for k in range(TOK_TILES)], axis=1)


def _store_token_rows(ref, val):
    rows = val.shape[0]
    for k in range(TOK_TILES):
        ref[pl.ds(k, rows, stride=TOK_TILES), :] = val[:, k * LANE:(k + 1) * LANE]


def _router_logits(h, wr_ref, br_ref):
    lg = _dot_3pass(h, wr_ref[...]) + br_ref[...]
    lane = lax.broadcasted_iota(I32, lg.shape, 1).astype(F32)
    return lg, lane


def _top_group(lg, lane):
    gl = jnp.where(lane < N_GROUPS, lg, NEG_INF)
    gmax = jnp.max(gl, axis=-1, keepdims=True)
    gsel = jnp.min(jnp.where(gl == gmax, lane, float(LANE)), axis=-1, keepdims=True)
    p_group = 1.0 / jnp.sum(jnp.exp(gl - gmax), axis=-1, keepdims=True)
    return gsel, p_group


def _expert_gates(lg, lane, lo, p_group):
    big = float(LANE)
    el = jnp.where((lane >= lo) & (lane < lo + EXPERTS_PER_GROUP), lg, NEG_INF)
    m1 = jnp.max(el, axis=-1, keepdims=True)
    i1 = jnp.min(jnp.where(el == m1, lane, big), axis=-1, keepdims=True)
    el2 = jnp.where(lane == i1, NEG_INF, el)
    m2 = jnp.max(el2, axis=-1, keepdims=True)
    i2 = jnp.min(jnp.where(el2 == m2, lane, big), axis=-1, keepdims=True)
    t = jnp.exp(m2 - m1)
    w1 = p_group / (1.0 + t)
    w2 = p_group * t / (1.0 + t)
    return jnp.where(lane == i1 - lo, w1, jnp.where(lane == i2 - lo, w2, 0.0))


def _router_kernel(h_ref, wr_ref, br_ref, tril_ref, meta_ref, cnt_ref, carry):
    @pl.when(pl.program_id(0) == 0)
    def _():
        carry[...] = jnp.zeros_like(carry)

    lg, lane = _router_logits(_load_token_rows(h_ref), wr_ref, br_ref)
    gsel, _ = _top_group(lg, lane)
    onehot = jnp.where(lane == gsel, 1.0, 0.0)
    incl = jnp.dot(tril_ref[...], onehot.astype(BF16), preferred_element_type=F32)
    rank = jnp.sum(onehot * (incl - 1.0 + carry[...]), axis=-1, keepdims=True)
    carry[...] = carry[...] + jnp.sum(onehot, axis=0, keepdims=True)
    meta_ref[...] = jnp.where(lane == 0, gsel, jnp.where(lane == 1, rank, 0.0)).astype(I32)
    cnt_ref[...] = carry[...].astype(I32)


def _router_weights(w_rg, b_rg, w_re, b_re):
    d = w_rg.shape[0]
    ne = N_GROUPS * EXPERTS_PER_GROUP
    wr = jnp.zeros((d, LANE), F32).at[:, 0:N_GROUPS].set(w_rg).at[:, N_GROUPS:N_GROUPS + ne].set(w_re)
    br = jnp.zeros((1, LANE), F32).at[0, 0:N_GROUPS].set(b_rg).at[0, N_GROUPS:N_GROUPS + ne].set(b_re)
    return wr, br


def _router(h2t, wr, br):
    t = h2t.shape[0] // TOK_TILES
    rb = ROW_BLOCK
    tril = jnp.asarray(np.tril(np.ones((rb, rb), np.float32)), BF16)
    full = lambda a: pl.BlockSpec(a.shape, lambda i: (0,) * a.ndim)
    return pl.pallas_call(
        _router_kernel,
        grid=(t // rb,),
        in_specs=[pl.BlockSpec((rb * TOK_TILES, LANE), lambda i: (i, 0)), full(wr), full(br), full(tril)],
        out_specs=[pl.BlockSpec((rb, LANE), lambda i: (i, 0)), pl.BlockSpec((1, LANE), lambda i: (0, 0))],
        out_shape=[jax.ShapeDtypeStruct((t, LANE), I32), jax.ShapeDtypeStruct((1, LANE), I32)],
        scratch_shapes=[pltpu.VMEM((1, LANE), F32)],
        compiler_params=_cparams(("arbitrary",)),
        name="moe_router",
    )(h2t, wr, br, tril)


def _invert_kernel(dest_ref, inv_ref, *, n_tok):
    def spare(s, c):
        inv_ref[s] = n_tok + (s & (ROW_BLOCK - 1))
        return c

    lax.fori_loop(0, inv_ref.shape[0], spare, 0, unroll=8)

    def put(t, c):
        inv_ref[dest_ref[t]] = t
        return c

    lax.fori_loop(0, n_tok, put, 0, unroll=8)


def _invert(dest, n_slots):
    smem = pl.BlockSpec(memory_space=pltpu.SMEM)
    return pl.pallas_call(
        functools.partial(_invert_kernel, n_tok=dest.shape[0]), in_specs=[smem], out_specs=smem,
        out_shape=jax.ShapeDtypeStruct((n_slots,), I32), name="moe_invert",
    )(dest)


def _token_copy(src_ref, dst_ref, s, d, sem):
    s8 = pl.multiple_of(s * TOK_TILES, TOK_TILES)
    d8 = pl.multiple_of(d * TOK_TILES, TOK_TILES)
    return pltpu.make_async_copy(src_ref.at[pl.ds(s8, TOK_TILES), :], dst_ref.at[pl.ds(d8, TOK_TILES), :], sem)


def _experts_kernel(bg_ref, inv_ref, h_ref, wr_ref, br_ref, wgu_ref, wdn_ref, y_ref, xbuf, ybuf, gsem, ssem, *, n_tok):
    i = pl.program_id(0)
    n_steps = pl.num_programs(0)
    rb, d = ROW_BLOCK, D_MODEL
    slot = i % 2
    other = 1 - slot
    nxt = jnp.minimum(i + 1, n_steps - 1)
    prev = jnp.maximum(i - 1, 0)
    block_rows = rb * TOK_TILES

    def gather_start(blk, sl, r):
        tok = jnp.minimum(inv_ref[blk * rb + r], n_tok - 1)
        _token_copy(h_ref, xbuf.at[sl], tok, r, gsem.at[sl]).start()

    def scatter_start(blk, sl, r, to_spare):
        dst = jnp.where(to_spare, n_tok + r, inv_ref[blk * rb + r])
        _token_copy(ybuf.at[sl], y_ref, r, dst, ssem.at[sl]).start()

    def gather_wait(sl):
        pltpu.make_async_copy(h_ref.at[pl.ds(0, block_rows), :], xbuf.at[sl], gsem.at[sl]).wait()

    def scatter_wait(sl):
        pltpu.make_async_copy(ybuf.at[sl], y_ref.at[pl.ds(0, block_rows), :], ssem.at[sl]).wait()

    @pl.when(i == 0)
    def _():
        ybuf[...] = jnp.zeros_like(ybuf)

        def one(r, c):
            gather_start(0, 0, r)
            return c
        lax.fori_loop(0, rb, one, 0, unroll=8)

    gather_wait(slot)

    @pl.when(i >= 1)
    def _():
        scatter_wait(slot)

    xf = _load_token_rows(xbuf.at[slot])
    lg, lane = _router_logits(xf, wr_ref, br_ref)
    _, p_group = _top_group(lg, lane)
    lo = (N_GROUPS + bg_ref[i] * EXPERTS_PER_GROUP).astype(F32)
    gates = _expert_gates(lg, lane, lo, p_group)
    x = xf.astype(BF16)
    per_expert = rb // EXPERTS_PER_GROUP
    hidden = []
    for e in range(EXPERTS_PER_GROUP):
        gu = jnp.dot(x, wgu_ref[0, e], preferred_element_type=F32)
        g = gu[:, 0:D_EXPERT]
        hidden.append((g * _sigmoid(g) * gu[:, D_EXPERT:] * gates[:, e:e + 1]).astype(BF16))
        for r in range(e * per_expert, (e + 1) * per_expert):
            gather_start(nxt, other, r)
            scatter_start(prev, other, r, i == 0)
    y = jnp.dot(jnp.concatenate(hidden, axis=1), wdn_ref[0], preferred_element_type=F32)
    _store_token_rows(ybuf.at[slot], y)

    @pl.when(i == n_steps - 1)
    def _():
        scatter_wait(other)

        def one(r, c):
            scatter_start(i, slot, r, False)
            return c
        lax.fori_loop(0, rb, one, 0, unroll=8)
        scatter_wait(slot)
        gather_wait(other)


def _experts(block_group, inv, h2t, wr, br, w_gu, w_dn):
    n_blocks = block_group.shape[0]
    rb, d = ROW_BLOCK, D_MODEL
    n_tok = h2t.shape[0] // TOK_TILES
    wgu = w_gu.reshape(N_GROUPS, EXPERTS_PER_GROUP, d, 2 * D_EXPERT)
    wdn = w_dn.reshape(N_GROUPS, EXPERTS_PER_GROUP * D_EXPERT, d)
    any_spec = pl.BlockSpec(memory_space=pl.ANY)
    return pl.pallas_call(
        functools.partial(_experts_kernel, n_tok=n_tok),
        grid_spec=pltpu.PrefetchScalarGridSpec(
            num_scalar_prefetch=2, grid=(n_blocks,),
            in_specs=[
                any_spec,
                pl.BlockSpec(wr.shape, lambda i, bg, inv_: (0, 0)),
                pl.BlockSpec(br.shape, lambda i, bg, inv_: (0, 0)),
                pl.BlockSpec((1, EXPERTS_PER_GROUP, d, 2 * D_EXPERT), lambda i, bg, inv_: (bg[i], 0, 0, 0)),
                pl.BlockSpec((1, EXPERTS_PER_GROUP * D_EXPERT, d), lambda i, bg, inv_: (bg[i], 0, 0)),
            ],
            out_specs=any_spec,
            scratch_shapes=[pltpu.VMEM((2, rb * TOK_TILES, LANE), F32), pltpu.VMEM((2, rb * TOK_TILES, LANE), F32),
                            pltpu.SemaphoreType.DMA((2,)), pltpu.SemaphoreType.DMA((2,))]),
        out_shape=jax.ShapeDtypeStruct(((n_tok + rb) * TOK_TILES, LANE), F32),
        compiler_params=pltpu.CompilerParams(dimension_semantics=("arbitrary",), vmem_limit_bytes=VMEM_LIMIT,
                                             has_side_effects=True),
        name="moe_experts",
    )(block_group, inv, h2t, wr, br, wgu, wdn)


def _moe(h2t, w_rg, b_rg, w_re, b_re, w_gu_bf16, w_dn_bf16):
    t = h2t.shape[0] // TOK_TILES
    rb = ROW_BLOCK
    wr, br = _router_weights(w_rg, b_rg, w_re, b_re)
    meta, counts = _router(h2t, wr, br)
    group, rank = meta[:, 0], meta[:, 1]
    cnt = counts[0, 0:N_GROUPS]
    padded = (cnt + rb - 1) // rb * rb
    seg_end = jnp.cumsum(padded)
    seg_start = seg_end - padded
    dest = seg_start[group] + rank
    n_blocks = t // rb + N_GROUPS
    block_start = jnp.arange(n_blocks, dtype=I32) * rb
    block_group = jnp.minimum(jnp.sum((block_start[:, None] >= seg_end[None, :]).astype(I32), axis=1), N_GROUPS - 1)
    inv = _invert(dest, n_blocks * rb)
    return _experts(block_group, inv, h2t, wr, br, w_gu_bf16, w_dn_bf16)


def _final_kernel(x_ref, y_ref, m_ref, g_ref, o_ref, *, nblk):
    b = pl.program_id(0) // nblk
    d = D_MODEL
    m = _mod_row(m_ref, b)
    x = x_ref[...] + m[:, 5 * d:6 * d] * _load_token_rows(y_ref)
    o_ref[...] = _rms(x, d) * g_ref[...]


def _final(x1, y, m, g, nblk):
    t, d = x1.shape
    rb = ROW_BLOCK
    row = pl.BlockSpec((rb, d), lambda i: (i, 0))
    tok = pl.BlockSpec((rb * TOK_TILES, LANE), lambda i: (i, 0))
    full = lambda a: pl.BlockSpec(a.shape, lambda i: (0,) * a.ndim)
    return pl.pallas_call(
        functools.partial(_final_kernel, nblk=nblk),
        grid=(t // rb,),
        in_specs=[row, tok, full(m), full(g)],
        out_specs=row,
        out_shape=jax.ShapeDtypeStruct((t, d), F32),
        compiler_params=_cparams(("arbitrary",)),
        name="final_norm",
    )(x1, y, m, g)


def _inproj_weight(w_in):
    d = w_in.shape[0]
    z = lambda n: jnp.zeros((d, n), w_in.dtype)
    perm = _rope_swap_perm()
    kr = w_in[:, 3168:3200]
    cols = [w_in[:, 0:2048], w_in[:, 2048:2848], z(W_C - 800),
            w_in[:, 2848:3040], z(256 - D_Q_RANK), w_in[:, 3040:3168], kr, kr[:, perm], z(LANE - 2 * D_ROPE)]
    return jnp.concatenate(cols, axis=1).astype(BF16)


def _hgrn_lower_bounds(logits):
    cum = jnp.cumsum(jax.nn.softmax(logits.astype(F32), axis=0), axis=0)
    return cum - cum[0]


def kernel(x, c, ctx, c_ctx, w_mod, b_mod, norm1_g, norm2_g, w_in, w_out, hgrn_lb_logits, hgrn_norm_g, na_rpb, gla_wg_f, gla_bg_f, gla_wg_b, gla_bg_b, gla_norm_g, mla_q_norm_g, mla_w_uq, mla_kv_norm_g, mla_w_ukv, moe_w_rg, moe_b_rg, moe_w_re, moe_b_re, moe_w_gu, moe_w_dn, final_norm_g):
    batch, n, d = x.shape
    l_ctx = ctx.shape[1]
    assert d == D_MODEL and l_ctx == ROW_BLOCK and n % ROW_BLOCK == 0 and batch <= 4
    s_len = l_ctx + n
    nblk = s_len // ROW_BLOCK
    depth = w_mod.shape[0]

    c8 = jnp.zeros((8, d), F32).at[0:batch].set(c).at[4].set(c_ctx)
    mods = _mod_vectors(c8, w_mod, b_mod)
    lower_bounds = _hgrn_lower_bounds(hgrn_lb_logits)
    tables = _mla_tables(n, l_ctx)

    xa = y_prev = None
    for layer in range(depth):
        keep_ctx = layer < depth - 1
        m = mods[layer]
        stream = (xa, y_prev, mods[layer - 1]) if layer else (ctx, x)
        xa, (pa, pb, pc, pd) = _inproj(stream, m, norm1_g[layer].reshape(1, d), _inproj_weight(w_in[layer]), nblk)
        oaf, oab = _hgrn_scan(pa, lower_bounds[layer], batch, nblk)
        ob = _neighbourhood_attention(pb, na_rpb[layer], batch, s_len, l_ctx, keep_ctx)
        ogf, ogb = _gla_scan(pc, gla_wg_f[layer], gla_bg_f[layer], gla_wg_b[layer], gla_bg_b[layer], batch, nblk)
        mla_w = _mla_weights(mla_q_norm_g[layer], mla_w_uq[layer], mla_kv_norm_g[layer], mla_w_ukv[layer])
        q, k, v = _mla_prep(pd, mla_w, tables, nblk)
        od = _mla_attention(q, k, v, batch, s_len, l_ctx, keep_ctx)
        xa, h2 = _outproj(oaf, oab, pa, ob, ogf, ogb, pc, od, xa, m, hgrn_norm_g[layer].reshape(1, -1),
                          gla_norm_g[layer].reshape(1, -1), norm2_g[layer].reshape(1, d), w_out[layer],
                          batch, nblk, keep_ctx)
        y_prev = _moe(h2, moe_w_rg[layer], moe_b_rg[layer], moe_w_re[layer], moe_b_re[layer],
                      moe_w_gu[layer].astype(BF16), moe_w_dn[layer].astype(BF16))
    out = _final(xa, y_prev, mods[depth - 1], final_norm_g.reshape(1, d), n // ROW_BLOCK)
    return out.reshape(batch, n, d)
```

```python
import functools

import numpy as np
import jax
import jax.numpy as jnp
from jax import lax
from jax.experimental import pallas as pl
from jax.experimental.pallas import tpu as pltpu

F32 = jnp.float32
BF16 = jnp.bfloat16
I32 = jnp.int32

D_MODEL = 1024
DEPTH = 2
GRID_W = 64
EPS = 1e-6
D_GROUP = 256
N_HEADS = 4
A_DH = 64
B_DH = 64
WIN_H = 8
WIN_W = 16
C_DK = 32
C_DV = 64
C_GATE_RANK = 16
C_GATE_NORM = 16.0
D_NOPE = 64
D_V = 64
D_ROPE = 32
ROPE_FREQS = 8
ROPE_BASE = 10000.0
D_Q_RANK = 192
D_KV_RANK = 128
MLA_SCALE = (D_NOPE + D_ROPE) ** -0.5
N_GROUPS = 4
EXPERTS_PER_GROUP = 8
D_EXPERT = 256

ROW_BLOCK = 256
SUB = 16
MACRO = 64
DECAY_GUARD = 60.0
N_SCAN_CONSTS = 9
SCAN_BATCHES = 2
LANE = 128
HEAD_TILE = 128
VT_ROWS = 80
W_A, W_B, W_C, W_D = 1280, 768, 896, 512
TOK_TILES = D_MODEL // LANE
VMEM_LIMIT = 52 * 1024 * 1024
NEG_INF = float("-inf")


def _bdot(a, b):
    return jnp.dot(a.astype(BF16), b.astype(BF16), preferred_element_type=F32)


def _bdot_nt(a, b):
    return lax.dot_general(a.astype(BF16), b.astype(BF16), (((1,), (1,)), ((), ())), preferred_element_type=F32)


def _bdot_tn(a, b):
    return lax.dot_general(a.astype(BF16), b.astype(BF16), (((0,), (0,)), ((), ())), preferred_element_type=F32)


def _split3(a):
    hi = a.astype(BF16)
    r1 = a - hi.astype(F32)
    mid = r1.astype(BF16)
    lo = (r1 - mid.astype(F32)).astype(BF16)
    return hi, mid, lo


def _dot_f32(a, b):
    ah, am, al = _split3(a)
    bh, bm, bl = _split3(b)
    d = lambda u, v: jnp.dot(u, v, preferred_element_type=F32)
    return d(ah, bh) + (d(ah, bm) + d(am, bh)) + (d(am, bm) + d(ah, bl) + d(al, bh))


def _dot_3pass(a, b):
    ah, am, _ = _split3(a)
    bh, bm, _ = _split3(b)
    d = lambda u, v: jnp.dot(u, v, preferred_element_type=F32)
    return d(ah, bh) + (d(ah, bm) + d(am, bh))


def _dot_sel_l(sel, a):
    ah, am, al = _split3(a)
    d = lambda v: jnp.dot(sel, v, preferred_element_type=F32)
    return d(ah) + d(am) + d(al)


def _dot_sel_r(a, sel):
    ah, am, al = _split3(a)
    d = lambda u: jnp.dot(u, sel, preferred_element_type=F32)
    return d(ah) + d(am) + d(al)


def _sigmoid(x):
    return 1.0 / (1.0 + jnp.exp(-x))


def _log_sigmoid(x):
    return jnp.minimum(x, 0.0) - jnp.log1p(jnp.exp(-jnp.abs(x)))


def _logaddexp(a, b):
    amax = jnp.maximum(a, b)
    delta = a - b
    return jnp.where(jnp.isnan(delta), a + b, amax + jnp.log1p(jnp.exp(-jnp.abs(delta))))


def _rms(x, width):
    return x * lax.rsqrt(jnp.sum(x * x, axis=-1, keepdims=True) / width + EPS)


def _cparams(sem, vmem=VMEM_LIMIT):
    return pltpu.CompilerParams(dimension_semantics=sem, vmem_limit_bytes=vmem)


def _mod_kernel(c_ref, w_ref, b_ref, o_ref):
    c = c_ref[...]
    act = c * _sigmoid(c)
    o_ref[0] = _dot_f32(act, w_ref[0]) + b_ref[0]


def _mod_vectors(c8, w_mod, b_mod):
    depth, d, six_d = w_mod.shape
    nj = six_d // d
    return pl.pallas_call(
        _mod_kernel,
        grid=(depth, nj),
        in_specs=[
            pl.BlockSpec((8, d), lambda l, j: (0, 0)),
            pl.BlockSpec((1, d, d), lambda l, j: (l, 0, j)),
            pl.BlockSpec((1, 1, d), lambda l, j: (l, 0, j)),
        ],
        out_specs=pl.BlockSpec((1, 8, d), lambda l, j: (l, 0, j)),
        out_shape=jax.ShapeDtypeStruct((depth, 8, six_d), F32),
        compiler_params=_cparams(("arbitrary", "arbitrary")),
        name="mod_vectors",
    )(c8, w_mod, b_mod.reshape(depth, 1, six_d))


def _mod_row(m_ref, row):
    return m_ref[pl.ds(row, 1), :]


def _inproj_kernel(*refs, first, nblk):
    if first:
        ctx_ref, lat_ref, m_ref, g_ref, w_ref, xo_ref, pa_ref, pb_ref, pc_ref, pd_ref = refs
    else:
        x_ref, y_ref, mprev_ref, m_ref, g_ref, w_ref, xo_ref, pa_ref, pb_ref, pc_ref, pd_ref = refs
    i = pl.program_id(0)
    b = i // nblk
    is_ctx = i % nblk == 0
    row = jnp.where(is_ctx, 4, b)
    d = D_MODEL
    if first:
        x = jnp.where(is_ctx, ctx_ref[0], lat_ref[0])
    else:
        mp = _mod_row(mprev_ref, row)
        x = x_ref[...] + mp[:, 5 * d:6 * d] * _load_token_rows(y_ref)
    xo_ref[...] = x
    m = _mod_row(m_ref, row)
    h = _rms(x, d) * g_ref[...] * (1.0 + m[:, d:2 * d]) + m[:, 0:d]
    p = _bdot(h, w_ref[...])
    pa_ref[...] = p[:, 0:W_A]
    pb_ref[...] = p[:, W_A:W_A + W_B].astype(pb_ref.dtype)
    pc_ref[...] = p[:, W_A + W_B:W_A + W_B + W_C]
    pd_ref[...] = p[:, W_A + W_B + W_C:]


def _inproj(stream, m, g, w, nblk):
    first = len(stream) == 2
    rb, d = ROW_BLOCK, D_MODEL
    row_spec = lambda w_: pl.BlockSpec((rb, w_), lambda i: (i, 0))
    full = lambda a: pl.BlockSpec(a.shape, lambda i: (0,) * a.ndim)
    if first:
        ctx, lat = stream
        t = ctx.shape[0] * nblk * rb
        specs = [pl.BlockSpec((1, rb, d), lambda i: (i // nblk, 0, 0)),
                 pl.BlockSpec((1, rb, d), lambda i: (i // nblk, jnp.maximum(i % nblk - 1, 0), 0))]
    else:
        t = stream[0].shape[0]
        specs = [row_spec(d), pl.BlockSpec((rb * TOK_TILES, LANE), lambda i: (i, 0)), full(stream[2])]
    ins = list(stream) + [m, g, w]
    specs += [full(m), full(g), full(w)]
    outs, ospecs = [jax.ShapeDtypeStruct((t, d), F32)], [row_spec(d)]
    for w_ in (W_A, W_B, W_C, W_D):
        outs.append(jax.ShapeDtypeStruct((t, w_), BF16 if w_ == W_B else F32))
        ospecs.append(row_spec(w_))
    res = pl.pallas_call(
        functools.partial(_inproj_kernel, first=first, nblk=nblk),
        grid=(t // rb,),
        in_specs=specs,
        out_specs=ospecs,
        out_shape=outs,
        compiler_params=_cparams(("arbitrary",)),
        name="in_projection",
    )(*ins)
    return res[0], res[1:]


def _sub_chunk(refs, i, tri, emat, emask_t, reverse):
    q_ref, k_ref, la_ref, v_ref, o_ref, st_ref, r_ref = refs
    hv = v_ref.shape[1]
    row_id = lax.broadcasted_iota(I32, (SUB, 1), 0)
    off = pl.multiple_of(i * SUB, SUB)
    qs = q_ref[pl.ds(off, SUB), :]
    ks = k_ref[pl.ds(off, SUB), :]
    las = la_ref[pl.ds(off, SUB), :]
    vs = v_ref[pl.ds(off, SUB), :]
    cum = _dot_sel_l(tri, las)
    last = cum[0:1, :] if reverse else cum[SUB - 1:SUB, :]
    for j in range(SUB):
        valid = (row_id <= j) if reverse else (row_id >= j)
        dlt = jnp.where(valid, cum - cum[j:j + 1, :], NEG_INF)
        r_ref[j * SUB:(j + 1) * SUB, :] = (qs * ks[j:j + 1, :] * jnp.exp(dlt)).astype(BF16)
    att = jnp.dot(r_ref[...], emat, preferred_element_type=F32)
    o = jnp.zeros((SUB, hv), F32)
    for j in range(SUB):
        o = o + att[j * SUB:(j + 1) * SUB, :] * vs[j:j + 1, :]
    st = st_ref[...]
    o = o + _bdot_nt(qs * jnp.exp(cum), st)
    o_ref[pl.ds(off, SUB), :] = o
    kd = ks * jnp.exp(last - cum)
    st_ref[...] = st * jnp.exp(last) + _bdot_tn(vs, kd) * emask_t


def _macro_step(refs, m, tri, emask_t, hmk, hmv, reverse):
    q_ref, k_ref, la_ref, v_ref, o_ref, st_ref, _ = refs
    hv = v_ref.shape[1]
    n_sub = MACRO // SUB
    off = pl.multiple_of(m * MACRO, MACRO)
    q = q_ref[pl.ds(off, MACRO), :]
    k = k_ref[pl.ds(off, MACRO), :]
    v = v_ref[pl.ds(off, MACRO), :]
    cum = _dot_sel_l(tri, la_ref[pl.ds(off, MACRO), :])
    last = cum[0:1, :] if reverse else cum[MACRO - 1:MACRO, :]
    st = st_ref[...]
    o_inter = _bdot_nt(q * jnp.exp(cum), st)
    st_ref[...] = st * jnp.exp(last) + _bdot_tn(v, k * jnp.exp(last - cum)) * emask_t
    vb = v.astype(BF16)
    row_i = lax.broadcasted_iota(I32, (N_HEADS * SUB, 1), 0) % SUB
    for s in range(n_sub):
        lo = s * SUB
        if reverse:
            k_lo, k_hi = lo, MACRO
            ref = cum[lo + SUB:lo + SUB + 1, :] if s < n_sub - 1 else jnp.zeros_like(last)
        else:
            k_lo, k_hi = 0, lo + SUB
            ref = cum[lo - 1:lo, :] if s > 0 else jnp.zeros_like(last)
        qt = q[lo:lo + SUB, :] * jnp.exp(cum[lo:lo + SUB, :] - ref)
        qs = jnp.concatenate([qt * hmk[h:h + 1, :] for h in range(N_HEADS)], axis=0).astype(BF16)
        kt = (k[k_lo:k_hi, :] * jnp.exp(ref - cum[k_lo:k_hi, :])).astype(BF16)
        att = lax.dot_general(qs, kt, (((1,), (1,)), ((), ())), preferred_element_type=F32)
        col = lax.broadcasted_iota(I32, (1, k_hi - k_lo), 1) + k_lo
        valid = (col >= lo + row_i) if reverse else (col <= lo + row_i)
        att = jnp.where(valid, att, 0.0)
        o_heads = jnp.dot(att.astype(BF16), vb[k_lo:k_hi, :], preferred_element_type=F32)
        o = o_inter[lo:lo + SUB, :]
        for h in range(N_HEADS):
            o = o + o_heads[h * SUB:(h + 1) * SUB, :] * hmv[h:h + 1, :]
        o_ref[pl.ds(off + lo, SUB), :] = o


def _scan_chains(chains, consts):
    tri_f, tri_b, trim_f, trim_b, emat, emask_t, hmk, hmv, sub_sum = consts
    rows = chains[0][0][0].shape[0]
    n_sub, n_macro = rows // SUB, rows // MACRO
    tot = None
    for refs, _ in chains:
        block_tot = jnp.dot(sub_sum, refs[2][...].astype(BF16), preferred_element_type=F32)
        tot = block_tot if tot is None else jnp.minimum(tot, block_tot)
    factorisable = jnp.min(tot) > -DECAY_GUARD

    @pl.when(factorisable)
    def _():
        def body(step, carry):
            for refs, rev in chains:
                _macro_step(refs, n_macro - 1 - step if rev else step, trim_b if rev else trim_f, emask_t, hmk, hmv, rev)
            return carry

        lax.fori_loop(0, n_macro, body, 0, unroll=True)

    @pl.when(jnp.logical_not(factorisable))
    def _():
        def body(step, carry):
            for refs, rev in chains:
                _sub_chunk(refs, n_sub - 1 - step if rev else step, tri_b if rev else tri_f, emat, emask_t, rev)
            return carry

        lax.fori_loop(0, n_sub, body, 0)


def _zero_at_first_block(*state_refs):
    @pl.when(pl.program_id(1) == 0)
    def _():
        for st in state_refs:
            st[...] = jnp.zeros_like(st)


def _scan_batches(batch):
    return SCAN_BATCHES if batch % SCAN_BATCHES == 0 else 1


def _scan_consts(hk, hv, dk, dv):
    tri = lambda n, low: jnp.asarray(np.tril(np.ones((n, n), np.float32)) if low else np.triu(np.ones((n, n), np.float32)), BF16)
    em = _head_match(hk, hv, dk, dv)
    heads = np.arange(N_HEADS)[:, None]
    hmk = (np.arange(hk)[None, :] // dk == heads).astype(np.float32)
    hmv = (np.arange(hv)[None, :] // dv == heads).astype(np.float32)
    sub_sum = (np.arange(ROW_BLOCK)[None, :] // SUB == np.arange(ROW_BLOCK // SUB)[:, None]).astype(np.float32)
    return [tri(SUB, True), tri(SUB, False), tri(MACRO, True), tri(MACRO, False), jnp.asarray(em, BF16),
            jnp.asarray(em.T, F32), jnp.asarray(hmk), jnp.asarray(hmv), jnp.asarray(sub_sum, BF16)]


def _hgrn_prep(q_ref, v_ref, f_ref, lb, qo, ko, lo, vo):
    qr = q_ref[...]
    qo[...] = qr * _sigmoid(qr) * (A_DH ** -0.5)
    vo[...] = v_ref[...]
    z = f_ref[...]
    lo[...] = _logaddexp(jnp.log(lb), jnp.log1p(-lb) + _log_sigmoid(z))
    ko[...] = (1.0 - lb) * _sigmoid(-z)


def _hgrn_kernel(qf_ref, vf_ref, ff_ref, qb_ref, vb_ref, fb_ref, lb_ref, *rest):
    consts, (of_ref, ob_ref), scratch = rest[:N_SCAN_CONSTS], rest[N_SCAN_CONSTS:N_SCAN_CONSTS + 2], rest[N_SCAN_CONSTS + 2:]
    chains = []
    for s in range(qf_ref.shape[0]):
        stf, stb, rf, rb_, q1, k1, l1, v1, q2, k2, l2, v2 = scratch[12 * s:12 * s + 12]
        _zero_at_first_block(stf, stb)
        _hgrn_prep(qf_ref.at[s], vf_ref.at[s], ff_ref.at[s], lb_ref[0:1, :], q1, k1, l1, v1)
        _hgrn_prep(qb_ref.at[s], vb_ref.at[s], fb_ref.at[s], lb_ref[1:2, :], q2, k2, l2, v2)
        chains += [((q1, k1, l1, v1, of_ref.at[s], stf, rf), False), ((q2, k2, l2, v2, ob_ref.at[s], stb, rb_), True)]
    _scan_chains(chains, [c[...] for c in consts])


def _head_match(hk, hv, dk, dv):
    m = (np.arange(hk)[:, None] // dk == np.arange(hv)[None, :] // dv).astype(np.float32)
    return m


def _bwd_block(t, nblk):
    return jnp.where(t == 0, 0, nblk - t)


def _hgrn_scan(pa, lb, batch, nblk):
    rb, w = ROW_BLOCK, D_GROUP
    consts = _scan_consts(w, w, A_DH, A_DH)
    nb = _scan_batches(batch)
    pa3 = pa.reshape(batch, nblk * rb, pa.shape[1])
    fwd = lambda col: pl.BlockSpec((nb, rb, w), lambda b, t: (b, t, col))
    bwd = lambda col: pl.BlockSpec((nb, rb, w), lambda b, t: (b, _bwd_block(t, nblk), col))
    full = lambda a: pl.BlockSpec(a.shape, lambda b, t: (0,) * a.ndim)
    vm = lambda shape, dt=F32: pltpu.VMEM(shape, dt)
    of, ob = pl.pallas_call(
        _hgrn_kernel,
        grid=(batch // nb, nblk),
        in_specs=[fwd(0), fwd(1), fwd(2), bwd(0), bwd(1), bwd(3), full(lb)] + [full(c) for c in consts],
        out_specs=[fwd(0), bwd(0)],
        out_shape=[jax.ShapeDtypeStruct((batch, nblk * rb, w), F32)] * 2,
        scratch_shapes=([vm((w, w)), vm((w, w)), vm((SUB * SUB, w), BF16), vm((SUB * SUB, w), BF16)]
                        + [vm((rb, w))] * 8) * nb,
        compiler_params=_cparams(("arbitrary", "arbitrary")),
        name="hgrn_scan",
    )(pa3, pa3, pa3, pa3, pa3, pa3, lb, *consts)
    return of.reshape(-1, w), ob.reshape(-1, w)


def _gla_prep(q_ref, k_ref, v_ref, z_ref, wg_ref, bg_ref, qo, ko, lo, vo):
    qo[...] = q_ref[...] * (C_DK ** -0.5)
    ko[...] = k_ref[...]
    vo[...] = v_ref[...]
    zl = _dot_f32(z_ref[...], wg_ref[...]) + bg_ref[...]
    lo[...] = _log_sigmoid(zl) / C_GATE_NORM


def _gla_kernel(qf_ref, kf_ref, vf_ref, zf_ref, qb_ref, kb_ref, vb_ref, zb_ref, wgf_ref, bgf_ref, wgb_ref, bgb_ref, *rest):
    consts, (of_ref, ob_ref), scratch = rest[:N_SCAN_CONSTS], rest[N_SCAN_CONSTS:N_SCAN_CONSTS + 2], rest[N_SCAN_CONSTS + 2:]
    chains = []
    for s in range(qf_ref.shape[0]):
        stf, stb, rf, rb_, q1, k1, l1, v1, q2, k2, l2, v2 = scratch[12 * s:12 * s + 12]
        _zero_at_first_block(stf, stb)
        _gla_prep(qf_ref.at[s], kf_ref.at[s], vf_ref.at[s], zf_ref.at[s], wgf_ref, bgf_ref, q1, k1, l1, v1)
        _gla_prep(qb_ref.at[s], kb_ref.at[s], vb_ref.at[s], zb_ref.at[s], wgb_ref, bgb_ref, q2, k2, l2, v2)
        chains += [((q1, k1, l1, v1, of_ref.at[s], stf, rf), False), ((q2, k2, l2, v2, ob_ref.at[s], stb, rb_), True)]
    _scan_chains(chains, [c[...] for c in consts])


def _gla_scan(pc, wg_f, bg_f, wg_b, bg_b, batch, nblk):
    rb = ROW_BLOCK
    hk, hv = N_HEADS * C_DK, N_HEADS * C_DV
    consts = _scan_consts(hk, hv, C_DK, C_DV)
    wgf = jnp.zeros((LANE, hk), F32).at[0:C_GATE_RANK].set(wg_f)
    wgb = jnp.zeros((LANE, hk), F32).at[C_GATE_RANK:2 * C_GATE_RANK].set(wg_b)
    bgf, bgb = bg_f.reshape(1, hk), bg_b.reshape(1, hk)
    nb = _scan_batches(batch)
    pc3 = pc.reshape(batch, nblk * rb, pc.shape[1])
    fwd = lambda w, col: pl.BlockSpec((nb, rb, w), lambda b, t: (b, t, col))
    bwd = lambda w, col: pl.BlockSpec((nb, rb, w), lambda b, t: (b, _bwd_block(t, nblk), col))
    full = lambda a: pl.BlockSpec(a.shape, lambda b, t: (0,) * a.ndim)
    vm = lambda shape, dt=F32: pltpu.VMEM(shape, dt)
    of, ob = pl.pallas_call(
        _gla_kernel,
        grid=(batch // nb, nblk),
        in_specs=[fwd(hk, 0), fwd(hk, 1), fwd(hv, 1), fwd(LANE, 6), bwd(hk, 0), bwd(hk, 1), bwd(hv, 1), bwd(LANE, 6),
                  full(wgf), full(bgf), full(wgb), full(bgb)] + [full(c) for c in consts],
        out_specs=[fwd(hv, 0), bwd(hv, 0)],
        out_shape=[jax.ShapeDtypeStruct((batch, nblk * rb, hv), F32)] * 2,
        scratch_shapes=([vm((hv, hk)), vm((hv, hk)), vm((SUB * SUB, hk), BF16), vm((SUB * SUB, hk), BF16)]
                        + [vm((rb, hk)), vm((rb, hk)), vm((rb, hk)), vm((rb, hv))] * 2) * nb,
        compiler_params=_cparams(("arbitrary", "arbitrary")),
        name="gla_scan",
    )(pc3, pc3, pc3, pc3, pc3, pc3, pc3, pc3, wgf, bgf, wgb, bgb, *consts)
    return of.reshape(-1, hv), ob.reshape(-1, hv)


NA_QROWS = ROW_BLOCK // GRID_W
NA_WROWS = WIN_H + NA_QROWS


def _na_window_rows(rows):
    kh = WIN_H
    patterns = []
    for r0 in (0, NA_QROWS, rows - NA_QROWS):
        ws = int(np.clip(r0 - kh // 2, 0, rows - NA_WROWS))
        seen = {}
        for a in range(NA_QROWS):
            r = r0 + a
            s = int(np.clip(r - kh // 2, 0, rows - kh))
            for jj in range(NA_WROWS):
                if s <= ws + jj < s + kh:
                    seen[(a, jj)] = ws + jj - r + kh - 1
        patterns.append(seen)
    return patterns


def _na_kernel(q_ref, k_ref, v_ref, by_col_ref, hm_ref, o_ref, bias_scr, *, j0, rows, ctx):
    j = pl.program_id(1) + j0
    q = q_ref[...] * (B_DH ** -0.5)
    hm = hm_ref[...]
    kc = k_ref[0:ctx, :]
    vc = v_ref[0:ctx, :]
    nt = lambda a, b: lax.dot_general(a, b, (((1,), (1,)), ((), ())), preferred_element_type=F32)

    @pl.when(j == 0)
    def _():
        acc = jnp.zeros(q.shape, F32)
        for h in range(N_HEADS):
            mh = hm[h:h + 1, :]
            s = nt(q * mh.astype(BF16), kc)
            p = jnp.exp(s - jnp.max(s, axis=-1, keepdims=True))
            inv = 1.0 / jnp.sum(p, axis=-1, keepdims=True)
            acc = acc + jnp.dot(p.astype(BF16), vc, preferred_element_type=F32) * (mh * inv)
        o_ref[...] = acc.astype(o_ref.dtype)

    @pl.when(j > 0)
    def _():
        for position, (at_step, seen) in enumerate(zip((1, 2, rows // NA_QROWS), _na_window_rows(rows))):
            @pl.when(j == at_step)
            def _(seen=seen):
                blocked = jnp.full((GRID_W, GRID_W), NEG_INF, F32)
                for h in range(N_HEADS):
                    for a in range(NA_QROWS):
                        for jj in range(NA_WROWS):
                            slab = by_col_ref[h, seen[(a, jj)]] if (a, jj) in seen else blocked
                            bias_scr[h, a * GRID_W:(a + 1) * GRID_W, jj * GRID_W:(jj + 1) * GRID_W] = slab

        r0 = (j - 1) * NA_QROWS
        start = jnp.clip(r0 - WIN_H // 2, 0, rows - NA_WROWS)
        off = pl.multiple_of(ctx + start * GRID_W, GRID_W)
        kw = k_ref[pl.ds(off, NA_WROWS * GRID_W), :]
        vw = v_ref[pl.ds(off, NA_WROWS * GRID_W), :]
        acc = jnp.zeros(q.shape, F32)
        for h in range(N_HEADS):
            mh = hm[h:h + 1, :]
            qh = q * mh.astype(BF16)
            sw = nt(qh, kw) + bias_scr[h]
            sc = nt(qh, kc)
            m = jnp.maximum(jnp.max(sw, axis=-1, keepdims=True), jnp.max(sc, axis=-1, keepdims=True))
            pw = jnp.exp(sw - m)
            pc_ = jnp.exp(sc - m)
            inv = 1.0 / (jnp.sum(pw, axis=-1, keepdims=True) + jnp.sum(pc_, axis=-1, keepdims=True))
            o = (jnp.dot(pw.astype(BF16), vw, preferred_element_type=F32)
                 + jnp.dot(pc_.astype(BF16), vc, preferred_element_type=F32))
            acc = acc + o * (mh * inv)
        o_ref[...] = acc.astype(o_ref.dtype)


def _na_bias_by_column(rpb):
    cidx = np.arange(GRID_W)
    c_start = np.clip(cidx - WIN_W // 2, 0, GRID_W - WIN_W)
    col_in = (cidx[None] >= c_start[:, None]) & (cidx[None] < c_start[:, None] + WIN_W)
    dc = np.clip(cidx[None] - cidx[:, None], -(WIN_W - 1), WIN_W - 1) + (WIN_W - 1)
    sel = (dc[None] == np.arange(2 * WIN_W - 1)[:, None, None]).astype(np.float32)
    by_col = jnp.einsum("hrc,cqw->hrqw", rpb.astype(F32), jnp.asarray(sel), precision=lax.Precision.HIGHEST)
    return jnp.where(jnp.asarray(col_in)[None, None], by_col, NEG_INF)


def _head_masks(width, dh):
    return jnp.asarray((np.arange(width)[None, :] // dh == np.arange(N_HEADS)[:, None]).astype(np.float32))


def _neighbourhood_attention(pb, rpb, batch, s_len, ctx, keep_ctx):
    rows = (s_len - ctx) // GRID_W
    assert rows >= NA_WROWS and rows % NA_QROWS == 0 and ctx == ROW_BLOCK
    rb = ROW_BLOCK
    j0 = 0 if keep_ctx else 1
    per_b = s_len // rb
    by_col = _na_bias_by_column(rpb)
    hm = _head_masks(D_GROUP, B_DH)

    return pl.pallas_call(
        functools.partial(_na_kernel, j0=j0, rows=rows, ctx=ctx),
        grid=(batch, per_b - j0),
        in_specs=[
            pl.BlockSpec((rb, D_GROUP), lambda b, jj: (b * per_b + jj + j0, 0)),
            pl.BlockSpec((s_len, D_GROUP), lambda b, jj: (b, 1)),
            pl.BlockSpec((s_len, D_GROUP), lambda b, jj: (b, 2)),
            pl.BlockSpec(by_col.shape, lambda b, jj: (0, 0, 0, 0)),
            pl.BlockSpec(hm.shape, lambda b, jj: (0, 0)),
        ],
        out_specs=pl.BlockSpec((rb, D_GROUP), lambda b, jj: (b * (per_b - j0) + jj, 0)),
        out_shape=jax.ShapeDtypeStruct((batch * (per_b - j0) * rb, D_GROUP), BF16),
        scratch_shapes=[pltpu.VMEM((N_HEADS, rb, NA_WROWS * GRID_W), F32)],
        compiler_params=_cparams(("arbitrary", "arbitrary")),
        name="neighbourhood_attention",
    )(pb, pb, pb, by_col, hm)


def _mla_prep_kernel(pd_ref, gq_ref, gkv_ref, wq1_ref, wq2_ref, wk_ref, wvt_ref, ones_ref, cq_ref, sq_ref, tk_ref,
                     place_ref, q_ref, k_ref, vt_ref):
    pd = pd_ref[...]
    cq = pd[:, 0:256]
    ckv = pd[:, 256:384]
    kr = pd[:, 384:512]
    qn = _rms(cq, D_Q_RANK) * gq_ref[...]
    q = _bdot(qn, wq1_ref[...]) * cq_ref[...] + _bdot(qn, wq2_ref[...]) * sq_ref[...]
    q_ref[...] = q.astype(BF16)
    kvn = (_rms(ckv, D_KV_RANK) * gkv_ref[...]).astype(BF16)
    k = jnp.dot(kvn, wk_ref[...], preferred_element_type=F32) + _dot_sel_r(kr * tk_ref[...], place_ref[...])
    k_ref[...] = k.astype(BF16)
    vt = lax.dot_general(wvt_ref[...], kvn, (((1,), (1,)), ((), ())), preferred_element_type=F32) + ones_ref[...]
    vt_ref[0] = vt.astype(BF16)


def _rope_swap_perm():
    f = ROPE_FREQS
    return np.concatenate([np.arange(f, 2 * f), np.arange(0, f), np.arange(3 * f, 4 * f), np.arange(2 * f, 3 * f)])


def _mla_tables(n, ctx):
    t = np.arange(n)
    inv_freq = ROPE_BASE ** (-np.arange(ROPE_FREQS, dtype=np.float32) / ROPE_FREQS)
    ang_r = (t // GRID_W).astype(np.float32)[:, None] * inv_freq
    ang_c = (t % GRID_W).astype(np.float32)[:, None] * inv_freq
    cos32 = np.concatenate([np.cos(ang_r), np.cos(ang_r), np.cos(ang_c), np.cos(ang_c)], axis=1)
    sin32 = np.concatenate([-np.sin(ang_r), np.sin(ang_r), -np.sin(ang_c), np.sin(ang_c)], axis=1)
    cos32 = np.concatenate([np.ones((ctx, D_ROPE), np.float32), cos32.astype(np.float32)], axis=0)
    sin32 = np.concatenate([np.zeros((ctx, D_ROPE), np.float32), sin32.astype(np.float32)], axis=0)
    s_len = n + ctx
    cq = np.zeros((s_len, N_HEADS, HEAD_TILE), np.float32)
    sq = np.zeros((s_len, N_HEADS, HEAD_TILE), np.float32)
    cq[:, :, 0:D_NOPE] = MLA_SCALE
    cq[:, :, D_NOPE:D_NOPE + D_ROPE] = cos32[:, None, :] * MLA_SCALE
    sq[:, :, D_NOPE:D_NOPE + D_ROPE] = sin32[:, None, :] * MLA_SCALE
    tk = np.zeros((s_len, LANE), np.float32)
    tk[:, 0:D_ROPE] = cos32
    tk[:, D_ROPE:2 * D_ROPE] = sin32
    place = np.zeros((LANE, N_HEADS * HEAD_TILE), np.float32)
    for h in range(N_HEADS):
        for l in range(D_ROPE):
            place[l, h * HEAD_TILE + D_NOPE + l] = 1.0
            place[D_ROPE + l, h * HEAD_TILE + D_NOPE + l] = 1.0
    return (jnp.asarray(cq.reshape(s_len, -1)), jnp.asarray(sq.reshape(s_len, -1)), jnp.asarray(tk),
            jnp.asarray(place, BF16))


def _mla_weights(q_norm_g, w_uq, kv_norm_g, w_ukv):
    perm = _rope_swap_perm()
    wq = w_uq.reshape(D_Q_RANK, N_HEADS, D_NOPE + D_ROPE)
    row_pad = (0, 256 - D_Q_RANK)
    wq1 = jnp.pad(wq, (row_pad, (0, 0), (0, HEAD_TILE - D_NOPE - D_ROPE)))
    wq2 = jnp.pad(wq[:, :, D_NOPE:][:, :, perm], (row_pad, (0, 0), (D_NOPE, HEAD_TILE - D_NOPE - D_ROPE)))
    wkv = w_ukv.reshape(D_KV_RANK, N_HEADS, D_NOPE + D_V)
    wk = jnp.pad(wkv[:, :, 0:D_NOPE], ((0, 0), (0, 0), (0, HEAD_TILE - D_NOPE)))
    wvt = jnp.pad(jnp.transpose(wkv[:, :, D_NOPE:], (1, 2, 0)), ((0, 0), (0, VT_ROWS - D_V), (0, 0)))
    wvt = wvt.reshape(N_HEADS * VT_ROWS, D_KV_RANK)
    ones_rows = jnp.asarray((np.arange(N_HEADS * VT_ROWS) % VT_ROWS >= D_V).astype(np.float32)).reshape(-1, 1)
    hw = N_HEADS * HEAD_TILE
    gq = jnp.pad(q_norm_g.reshape(1, D_Q_RANK), ((0, 0), row_pad))
    return (gq, kv_norm_g.reshape(1, D_KV_RANK), wq1.reshape(256, hw).astype(BF16), wq2.reshape(256, hw).astype(BF16),
            wk.reshape(D_KV_RANK, hw).astype(BF16), wvt.astype(BF16), ones_rows)


def _mla_prep(pd, weights, tables, nblk):
    t_all = pd.shape[0]
    rb = ROW_BLOCK
    gq, gkv, wq1, wq2, wk, wvt, ones_rows = weights
    cq, sq, tk, place = tables
    hw = N_HEADS * HEAD_TILE
    hv = N_HEADS * VT_ROWS
    row = lambda w: pl.BlockSpec((rb, w), lambda i: (i, 0))
    pos = lambda w: pl.BlockSpec((rb, w), lambda i: (i % nblk, 0))
    full = lambda a: pl.BlockSpec(a.shape, lambda i: (0,) * a.ndim)
    return pl.pallas_call(
        _mla_prep_kernel,
        grid=(t_all // rb,),
        in_specs=[row(W_D), full(gq), full(gkv), full(wq1), full(wq2), full(wk), full(wvt), full(ones_rows),
                  pos(hw), pos(hw), pos(LANE), full(place)],
        out_specs=[row(hw), row(hw), pl.BlockSpec((1, hv, rb), lambda i: (i // nblk, 0, i % nblk))],
        out_shape=[jax.ShapeDtypeStruct((t_all, hw), BF16), jax.ShapeDtypeStruct((t_all, hw), BF16),
                   jax.ShapeDtypeStruct((t_all // (nblk * rb), hv, nblk * rb), BF16)],
        compiler_params=_cparams(("arbitrary",)),
        name="mla_prep",
    )(pd, gq, gkv, wq1, wq2, wk, wvt, ones_rows, cq, sq, tk, place)


MLA_HEADS_PER_STEP = 4
MLA_KEY_CHUNK = 256


def _mla_attn_kernel(q_ref, k_ref, vt_ref, o_ref, s_scr, *, j0, ctx):
    j = pl.program_id(2) + j0
    rb = q_ref.shape[0]

    def heads(n_keys):
        hs = range(MLA_HEADS_PER_STEP)
        chunks = [slice(c * MLA_KEY_CHUNK, (c + 1) * MLA_KEY_CHUNK) for c in range(n_keys // MLA_KEY_CHUNK)]
        lanes = [slice(h * HEAD_TILE, (h + 1) * HEAD_TILE) for h in hs]
        qs = [q_ref[:, lanes[h]] for h in hs]
        m = [jnp.full((1, rb), NEG_INF, F32) for _ in hs]
        for keys in chunks:
            for h in hs:
                st = lax.dot_general(k_ref[keys, lanes[h]], qs[h], (((1,), (1,)), ((), ())),
                                     preferred_element_type=F32)
                s_scr[h, keys, :] = st
                m[h] = jnp.maximum(m[h], jnp.max(st, axis=0, keepdims=True))
        acc = [jnp.zeros((VT_ROWS, rb), F32) for _ in hs]
        for keys in chunks:
            for h in hs:
                pt = jnp.exp(s_scr[h, keys, :] - m[h]).astype(BF16)
                acc[h] = acc[h] + jnp.dot(vt_ref[0, h * VT_ROWS:(h + 1) * VT_ROWS, keys], pt,
                                          preferred_element_type=F32)
        for h in hs:
            o_ref[0, h * D_V:(h + 1) * D_V, :] = (acc[h][0:D_V, :] * (1.0 / acc[h][D_V:D_V + 1, :])).astype(o_ref.dtype)

    @pl.when(j == 0)
    def _():
        heads(ctx)

    @pl.when(j > 0)
    def _():
        heads(k_ref.shape[0])


def _mla_attention(q, k, vt, batch, s_len, ctx, keep_ctx):
    rb = ROW_BLOCK
    nblk = s_len // rb
    j0 = 0 if keep_ctx else 1
    hps = MLA_HEADS_PER_STEP
    return pl.pallas_call(
        functools.partial(_mla_attn_kernel, j0=j0, ctx=ctx),
        grid=(batch, N_HEADS // hps, nblk - j0),
        in_specs=[
            pl.BlockSpec((rb, hps * HEAD_TILE), lambda b, h, jj: (b * nblk + jj + j0, h)),
            pl.BlockSpec((s_len, hps * HEAD_TILE), lambda b, h, jj: (b, h)),
            pl.BlockSpec((1, hps * VT_ROWS, s_len), lambda b, h, jj: (b, h, 0)),
        ],
        out_specs=pl.BlockSpec((1, hps * D_V, rb), lambda b, h, jj: (b * (nblk - j0) + jj, h, 0)),
        out_shape=jax.ShapeDtypeStruct((batch * (nblk - j0), N_HEADS * D_V, rb), BF16),
        scratch_shapes=[pltpu.VMEM((hps, s_len, rb), F32)],
        compiler_params=_cparams(("arbitrary", "arbitrary", "arbitrary")),
        name="mla_attention",
    )(q, k, vt)


def _outproj_kernel(oaf_ref, oab_ref, ga_ref, ob_ref, ogf_ref, ogb_ref, gc_ref, od_ref, x_ref, m_ref, gha_ref, ghc_ref,
                    g2_ref, wa_ref, wb_ref, wc_ref, wd_ref, em_ref, x1_ref, h2_ref, *, j0, nblk_out):
    i = pl.program_id(0)
    b = i // nblk_out
    row = jnp.where(i % nblk_out + j0 == 0, 4, b)
    d = D_MODEL
    em = em_ref[...]

    def readout(o, g_norm, gate, dh):
        ms = _dot_sel_r(o * o, em) / dh
        return o * lax.rsqrt(ms + EPS) * g_norm * (gate * _sigmoid(gate))

    a = readout(oaf_ref[...] + oab_ref[...], gha_ref[...], ga_ref[...], A_DH)
    c = readout(ogf_ref[...] + ogb_ref[...], ghc_ref[...], gc_ref[...], C_DV)
    mix = (_bdot(a, wa_ref[...]) + jnp.dot(ob_ref[...], wb_ref[...], preferred_element_type=F32)
           + _bdot(c, wc_ref[...])
           + lax.dot_general(od_ref[0], wd_ref[...], (((0,), (0,)), ((), ())), preferred_element_type=F32))
    m = _mod_row(m_ref, row)
    x1 = x_ref[...] + m[:, 2 * d:3 * d] * mix
    x1_ref[...] = x1
    _store_token_rows(h2_ref, _rms(x1, d) * g2_ref[...] * (1.0 + m[:, 4 * d:5 * d]) + m[:, 3 * d:4 * d])


def _outproj(oaf, oab, pa, ob, ogf, ogb, pc, od, x, m, gha, ghc, g2, w_out, batch, nblk, keep_ctx):
    rb, d = ROW_BLOCK, D_MODEL
    j0 = 0 if keep_ctx else 1
    nblk_out = nblk - j0
    t_out = batch * nblk_out * rb
    wa = w_out[0:256].astype(BF16)
    wb = w_out[256:512].astype(BF16)
    wc = w_out[512:768].astype(BF16)
    wd = w_out[768:1024].astype(BF16)
    em = jnp.asarray(_head_match(D_GROUP, D_GROUP, 64, 64), BF16)
    src = lambda i: (i // nblk_out) * nblk + i % nblk_out + j0
    row_in = lambda w, col=0: pl.BlockSpec((rb, w), lambda i: (src(i), col))
    row_out = lambda w: pl.BlockSpec((rb, w), lambda i: (i, 0))
    full = lambda a_: pl.BlockSpec(a_.shape, lambda i: (0,) * a_.ndim)
    return pl.pallas_call(
        functools.partial(_outproj_kernel, j0=j0, nblk_out=nblk_out),
        grid=(t_out // rb,),
        in_specs=[row_in(256), row_in(256), row_in(256, 4), row_out(256), row_in(256), row_in(256), row_in(256, 2),
                  pl.BlockSpec((1, N_HEADS * D_V, rb), lambda i: (i, 0, 0)), row_in(d), full(m), full(gha), full(ghc), full(g2),
                  full(wa), full(wb), full(wc), full(wd), full(em)],
        out_specs=[row_out(d), pl.BlockSpec((rb * TOK_TILES, LANE), lambda i: (i, 0))],
        out_shape=[jax.ShapeDtypeStruct((t_out, d), F32), jax.ShapeDtypeStruct((t_out * TOK_TILES, LANE), F32)],
        compiler_params=_cparams(("arbitrary",)),
        name="out_projection",
    )(oaf, oab, pa, ob, ogf, ogb, pc, od, x, m, gha, ghc, g2, wa, wb, wc, wd, em)


def _load_token_rows(ref):
    rows = ref.shape[0] // TOK_TILES
    return jnp.concatenate([ref[pl.ds(k, rows, stride=TOK_TILES), :] for k in range(TOK_TILES)], axis=1)


def _store_token_rows(ref, val):
    rows = val.shape[0]
    for k in range(TOK_TILES):
        ref[pl.ds(k, rows, stride=TOK_TILES), :] = val[:, k * LANE:(k + 1) * LANE]


def _router_logits(h, wr_ref, br_ref):
    lg = _dot_3pass(h, wr_ref[...]) + br_ref[...]
    lane = lax.broadcasted_iota(I32, lg.shape, 1).astype(F32)
    return lg, lane


def _top_group(lg, lane):
    gl = jnp.where(lane < N_GROUPS, lg, NEG_INF)
    gmax = jnp.max(gl, axis=-1, keepdims=True)
    gsel = jnp.min(jnp.where(gl == gmax, lane, float(LANE)), axis=-1, keepdims=True)
    p_group = 1.0 / jnp.sum(jnp.exp(gl - gmax), axis=-1, keepdims=True)
    return gsel, p_group


def _expert_gates(lg, lane, lo, p_group):
    big = float(LANE)
    el = jnp.where((lane >= lo) & (lane < lo + EXPERTS_PER_GROUP), lg, NEG_INF)
    m1 = jnp.max(el, axis=-1, keepdims=True)
    i1 = jnp.min(jnp.where(el == m1, lane, big), axis=-1, keepdims=True)
    el2 = jnp.where(lane == i1, NEG_INF, el)
    m2 = jnp.max(el2, axis=-1, keepdims=True)
    i2 = jnp.min(jnp.where(el2 == m2, lane, big), axis=-1, keepdims=True)
    t = jnp.exp(m2 - m1)
    w1 = p_group / (1.0 + t)
    w2 = p_group * t / (1.0 + t)
    return jnp.where(lane == i1 - lo, w1, jnp.where(lane == i2 - lo, w2, 0.0))


def _router_kernel(h_ref, wr_ref, br_ref, tril_ref, meta_ref, cnt_ref, carry):
    @pl.when(pl.program_id(0) == 0)
    def _():
        carry[...] = jnp.zeros_like(carry)

    lg, lane = _router_logits(_load_token_rows(h_ref), wr_ref, br_ref)
    gsel, _ = _top_group(lg, lane)
    onehot = jnp.where(lane == gsel, 1.0, 0.0)
    incl = jnp.dot(tril_ref[...], onehot.astype(BF16), preferred_element_type=F32)
    rank = jnp.sum(onehot * (incl - 1.0 + carry[...]), axis=-1, keepdims=True)
    carry[...] = carry[...] + jnp.sum(onehot, axis=0, keepdims=True)
    meta_ref[...] = jnp.where(lane == 0, gsel, jnp.where(lane == 1, rank, 0.0)).astype(I32)
    cnt_ref[...] = carry[...].astype(I32)


def _router_weights(w_rg, b_rg, w_re, b_re):
    ne = N_GROUPS * EXPERTS_PER_GROUP
    pad = ((0, 0), (0, LANE - N_GROUPS - ne))
    wr = jnp.pad(jnp.concatenate([w_rg, w_re], axis=1).astype(F32), pad)
    br = jnp.pad(jnp.concatenate([b_rg, b_re]).astype(F32).reshape(1, -1), pad)
    return wr, br


def _router(h2t, wr, br):
    t = h2t.shape[0] // TOK_TILES
    rb = ROW_BLOCK
    tril = jnp.asarray(np.tril(np.ones((rb, rb), np.float32)), BF16)
    full = lambda a: pl.BlockSpec(a.shape, lambda i: (0,) * a.ndim)
    return pl.pallas_call(
        _router_kernel,
        grid=(t // rb,),
        in_specs=[pl.BlockSpec((rb * TOK_TILES, LANE), lambda i: (i, 0)), full(wr), full(br), full(tril)],
        out_specs=[pl.BlockSpec((rb, LANE), lambda i: (i, 0)), pl.BlockSpec((1, LANE), lambda i: (0, 0))],
        out_shape=[jax.ShapeDtypeStruct((t, LANE), I32), jax.ShapeDtypeStruct((1, LANE), I32)],
        scratch_shapes=[pltpu.VMEM((1, LANE), F32)],
        compiler_params=_cparams(("arbitrary",)),
        name="moe_router",
    )(h2t, wr, br, tril)


def _invert_kernel(dest_ref, inv_ref, *, n_tok):
    def spare(s, c):
        inv_ref[s] = n_tok + (s & (ROW_BLOCK - 1))
        return c

    lax.fori_loop(0, inv_ref.shape[0], spare, 0, unroll=8)

    def put(t, c):
        inv_ref[dest_ref[t]] = t
        return c

    lax.fori_loop(0, n_tok, put, 0, unroll=8)


def _invert(dest, n_slots):
    smem = pl.BlockSpec(memory_space=pltpu.SMEM)
    return pl.pallas_call(
        functools.partial(_invert_kernel, n_tok=dest.shape[0]), in_specs=[smem], out_specs=smem,
        out_shape=jax.ShapeDtypeStruct((n_slots,), I32), name="moe_invert",
    )(dest)


def _token_copy(src_ref, dst_ref, s, d, sem):
    s8 = pl.multiple_of(s * TOK_TILES, TOK_TILES)
    d8 = pl.multiple_of(d * TOK_TILES, TOK_TILES)
    return pltpu.make_async_copy(src_ref.at[pl.ds(s8, TOK_TILES), :], dst_ref.at[pl.ds(d8, TOK_TILES), :], sem)


def _experts_kernel(bg_ref, inv_ref, h_ref, wr_ref, br_ref, wgu_ref, wdn_ref, y_ref, xbuf, ybuf, gsem, ssem, *, n_tok):
    i = pl.program_id(0)
    n_steps = pl.num_programs(0)
    rb, d = ROW_BLOCK, D_MODEL
    slot = i % 2
    other = 1 - slot
    nxt = jnp.minimum(i + 1, n_steps - 1)
    prev = jnp.maximum(i - 1, 0)
    block_rows = rb * TOK_TILES

    def gather_start(blk, sl, r):
        tok = jnp.minimum(inv_ref[blk * rb + r], n_tok - 1)
        _token_copy(h_ref, xbuf.at[sl], tok, r, gsem.at[sl]).start()

    def scatter_start(blk, sl, r, to_spare):
        dst = jnp.where(to_spare, n_tok + r, inv_ref[blk * rb + r])
        _token_copy(ybuf.at[sl], y_ref, r, dst, ssem.at[sl]).start()

    def gather_wait(sl):
        pltpu.make_async_copy(h_ref.at[pl.ds(0, block_rows), :], xbuf.at[sl], gsem.at[sl]).wait()

    def scatter_wait(sl):
        pltpu.make_async_copy(ybuf.at[sl], y_ref.at[pl.ds(0, block_rows), :], ssem.at[sl]).wait()

    @pl.when(i == 0)
    def _():
        ybuf[...] = jnp.zeros_like(ybuf)

        def one(r, c):
            gather_start(0, 0, r)
            return c
        lax.fori_loop(0, rb, one, 0, unroll=8)

    gather_wait(slot)

    @pl.when(i >= 1)
    def _():
        scatter_wait(slot)

    xf = _load_token_rows(xbuf.at[slot])
    lg, lane = _router_logits(xf, wr_ref, br_ref)
    _, p_group = _top_group(lg, lane)
    lo = (N_GROUPS + bg_ref[i] * EXPERTS_PER_GROUP).astype(F32)
    gates = _expert_gates(lg, lane, lo, p_group)
    x = xf.astype(BF16)
    per_expert = rb // EXPERTS_PER_GROUP
    hidden = []
    for e in range(EXPERTS_PER_GROUP):
        gu = jnp.dot(x, wgu_ref[0, e], preferred_element_type=F32)
        g = gu[:, 0:D_EXPERT]
        hidden.append((g * _sigmoid(g) * gu[:, D_EXPERT:] * gates[:, e:e + 1]).astype(BF16))
        for r in range(e * per_expert, (e + 1) * per_expert):
            gather_start(nxt, other, r)
            scatter_start(prev, other, r, i == 0)
    y = jnp.dot(jnp.concatenate(hidden, axis=1), wdn_ref[0], preferred_element_type=F32)
    _store_token_rows(ybuf.at[slot], y)

    @pl.when(i == n_steps - 1)
    def _():
        scatter_wait(other)

        def one(r, c):
            scatter_start(i, slot, r, False)
            return c
        lax.fori_loop(0, rb, one, 0, unroll=8)
        scatter_wait(slot)
        gather_wait(other)


def _experts(block_group, inv, h2t, wr, br, w_gu, w_dn):
    n_blocks = block_group.shape[0]
    rb, d = ROW_BLOCK, D_MODEL
    n_tok = h2t.shape[0] // TOK_TILES
    wgu = w_gu.reshape(N_GROUPS, EXPERTS_PER_GROUP, d, 2 * D_EXPERT)
    wdn = w_dn.reshape(N_GROUPS, EXPERTS_PER_GROUP * D_EXPERT, d)
    any_spec = pl.BlockSpec(memory_space=pl.ANY)
    return pl.pallas_call(
        functools.partial(_experts_kernel, n_tok=n_tok),
        grid_spec=pltpu.PrefetchScalarGridSpec(
            num_scalar_prefetch=2, grid=(n_blocks,),
            in_specs=[
                any_spec,
                pl.BlockSpec(wr.shape, lambda i, bg, inv_: (0, 0)),
                pl.BlockSpec(br.shape, lambda i, bg, inv_: (0, 0)),
                pl.BlockSpec((1, EXPERTS_PER_GROUP, d, 2 * D_EXPERT), lambda i, bg, inv_: (bg[i], 0, 0, 0)),
                pl.BlockSpec((1, EXPERTS_PER_GROUP * D_EXPERT, d), lambda i, bg, inv_: (bg[i], 0, 0)),
            ],
            out_specs=any_spec,
            scratch_shapes=[pltpu.VMEM((2, rb * TOK_TILES, LANE), F32), pltpu.VMEM((2, rb * TOK_TILES, LANE), F32),
                            pltpu.SemaphoreType.DMA((2,)), pltpu.SemaphoreType.DMA((2,))]),
        out_shape=jax.ShapeDtypeStruct(((n_tok + rb) * TOK_TILES, LANE), F32),
        compiler_params=pltpu.CompilerParams(dimension_semantics=("arbitrary",), vmem_limit_bytes=VMEM_LIMIT,
                                             has_side_effects=True),
        name="moe_experts",
    )(block_group, inv, h2t, wr, br, wgu, wdn)


def _moe(h2t, w_rg, b_rg, w_re, b_re, w_gu_bf16, w_dn_bf16):
    t = h2t.shape[0] // TOK_TILES
    rb = ROW_BLOCK
    wr, br = _router_weights(w_rg, b_rg, w_re, b_re)
    meta, counts = _router(h2t, wr, br)
    group, rank = meta[:, 0], meta[:, 1]
    cnt = counts[0, 0:N_GROUPS]
    padded = (cnt + rb - 1) // rb * rb
    seg_end = jnp.cumsum(padded)
    seg_start = seg_end - padded
    dest = seg_start[group] + rank
    n_blocks = t // rb + N_GROUPS
    block_start = jnp.arange(n_blocks, dtype=I32) * rb
    block_group = jnp.minimum(jnp.sum((block_start[:, None] >= seg_end[None, :]).astype(I32), axis=1), N_GROUPS - 1)
    inv = _invert(dest, n_blocks * rb)
    return _experts(block_group, inv, h2t, wr, br, w_gu_bf16, w_dn_bf16)


def _final_kernel(x_ref, y_ref, m_ref, g_ref, o_ref, *, nblk):
    b = pl.program_id(0) // nblk
    d = D_MODEL
    m = _mod_row(m_ref, b)
    x = x_ref[...] + m[:, 5 * d:6 * d] * _load_token_rows(y_ref)
    o_ref[...] = _rms(x, d) * g_ref[...]


def _final(x1, y, m, g, nblk):
    t, d = x1.shape
    rb = ROW_BLOCK
    row = pl.BlockSpec((rb, d), lambda i: (i, 0))
    tok = pl.BlockSpec((rb * TOK_TILES, LANE), lambda i: (i, 0))
    full = lambda a: pl.BlockSpec(a.shape, lambda i: (0,) * a.ndim)
    return pl.pallas_call(
        functools.partial(_final_kernel, nblk=nblk),
        grid=(t // rb,),
        in_specs=[row, tok, full(m), full(g)],
        out_specs=row,
        out_shape=jax.ShapeDtypeStruct((t, d), F32),
        compiler_params=_cparams(("arbitrary",)),
        name="final_norm",
    )(x1, y, m, g)


def _inproj_weight(w_in):
    d = w_in.shape[0]
    z = lambda n: jnp.zeros((d, n), w_in.dtype)
    perm = _rope_swap_perm()
    kr = w_in[:, 3168:3200]
    cols = [w_in[:, 0:2048], w_in[:, 2048:2848], z(W_C - 800),
            w_in[:, 2848:3040], z(256 - D_Q_RANK), w_in[:, 3040:3168], kr, kr[:, perm], z(LANE - 2 * D_ROPE)]
    return jnp.concatenate(cols, axis=1).astype(BF16)


def _hgrn_lower_bounds(logits):
    cum = jnp.cumsum(jax.nn.softmax(logits.astype(F32), axis=0), axis=0)
    return cum - cum[0]


def kernel(x, c, ctx, c_ctx, w_mod, b_mod, norm1_g, norm2_g, w_in, w_out, hgrn_lb_logits, hgrn_norm_g, na_rpb, gla_wg_f, gla_bg_f, gla_wg_b, gla_bg_b, gla_norm_g, mla_q_norm_g, mla_w_uq, mla_kv_norm_g, mla_w_ukv, moe_w_rg, moe_b_rg, moe_w_re, moe_b_re, moe_w_gu, moe_w_dn, final_norm_g):
    batch, n, d = x.shape
    l_ctx = ctx.shape[1]
    assert d == D_MODEL and l_ctx == ROW_BLOCK and n % ROW_BLOCK == 0 and batch <= 4
    s_len = l_ctx + n
    nblk = s_len // ROW_BLOCK
    depth = w_mod.shape[0]

    c8 = jnp.zeros((8, d), F32).at[0:batch].set(c).at[4].set(c_ctx)
    mods = _mod_vectors(c8, w_mod, b_mod)
    lower_bounds = _hgrn_lower_bounds(hgrn_lb_logits)
    tables = _mla_tables(n, l_ctx)

    xa = y_prev = None
    for layer in range(depth):
        keep_ctx = layer < depth - 1
        m = mods[layer]
        stream = (xa, y_prev, mods[layer - 1]) if layer else (ctx, x)
        xa, (pa, pb, pc, pd) = _inproj(stream, m, norm1_g[layer].reshape(1, d), _inproj_weight(w_in[layer]), nblk)
        oaf, oab = _hgrn_scan(pa, lower_bounds[layer], batch, nblk)
        ob = _neighbourhood_attention(pb, na_rpb[layer], batch, s_len, l_ctx, keep_ctx)
        ogf, ogb = _gla_scan(pc, gla_wg_f[layer], gla_bg_f[layer], gla_wg_b[layer], gla_bg_b[layer], batch, nblk)
        mla_w = _mla_weights(mla_q_norm_g[layer], mla_w_uq[layer], mla_kv_norm_g[layer], mla_w_ukv[layer])
        q, k, v = _mla_prep(pd, mla_w, tables, nblk)
        od = _mla_attention(q, k, v, batch, s_len, l_ctx, keep_ctx)
        xa, h2 = _outproj(oaf, oab, pa, ob, ogf, ogb, pc, od, xa, m, hgrn_norm_g[layer].reshape(1, -1),
                          gla_norm_g[layer].reshape(1, -1), norm2_g[layer].reshape(1, d), w_out[layer],
                          batch, nblk, keep_ctx)
        y_prev = _moe(h2, moe_w_rg[layer], moe_b_rg[layer], moe_w_re[layer], moe_b_re[layer],
                      moe_w_gu[layer].astype(BF16), moe_w_dn[layer].astype(BF16))
    out = _final(xa, y_prev, mods[depth - 1], final_norm_g.reshape(1, d), n // ROW_BLOCK)
    return out.reshape(batch, n, d)
```

```python
import functools

import numpy as np
import jax
import jax.numpy as jnp
from jax import lax
from jax.experimental import pallas as pl
from jax.experimental.pallas import tpu as pltpu

F32 = jnp.float32
BF16 = jnp.bfloat16
I32 = jnp.int32

D_MODEL = 1024
DEPTH = 2
GRID_W = 64
EPS = 1e-6
D_GROUP = 256
N_HEADS = 4
A_DH = 64
B_DH = 64
WIN_H = 8
WIN_W = 16
C_DK = 32
C_DV = 64
C_GATE_RANK = 16
C_GATE_NORM = 16.0
D_NOPE = 64
D_V = 64
D_ROPE = 32
ROPE_FREQS = 8
ROPE_BASE = 10000.0
D_Q_RANK = 192
D_KV_RANK = 128
MLA_SCALE = (D_NOPE + D_ROPE) ** -0.5
N_GROUPS = 4
EXPERTS_PER_GROUP = 8
D_EXPERT = 256

ROW_BLOCK = 256
SUB = 16
MACRO = 64
DECAY_GUARD = 60.0
N_SCAN_CONSTS = 9
SCAN_BATCHES = 2
LANE = 128
HEAD_TILE = 128
VT_ROWS = 80
W_A, W_B, W_C, W_D = 1280, 768, 896, 512
TOK_TILES = D_MODEL // LANE
VMEM_LIMIT = 52 * 1024 * 1024
NEG_INF = float("-inf")


def _bdot(a, b):
    return jnp.dot(a.astype(BF16), b.astype(BF16), preferred_element_type=F32)


def _bdot_nt(a, b):
    return lax.dot_general(a.astype(BF16), b.astype(BF16), (((1,), (1,)), ((), ())), preferred_element_type=F32)


def _bdot_tn(a, b):
    return lax.dot_general(a.astype(BF16), b.astype(BF16), (((0,), (0,)), ((), ())), preferred_element_type=F32)


def _split3(a):
    hi = a.astype(BF16)
    r1 = a - hi.astype(F32)
    mid = r1.astype(BF16)
    lo = (r1 - mid.astype(F32)).astype(BF16)
    return hi, mid, lo


def _dot_f32(a, b):
    ah, am, al = _split3(a)
    bh, bm, bl = _split3(b)
    d = lambda u, v: jnp.dot(u, v, preferred_element_type=F32)
    return d(ah, bh) + (d(ah, bm) + d(am, bh)) + (d(am, bm) + d(ah, bl) + d(al, bh))


def _dot_3pass(a, b):
    ah, am, _ = _split3(a)
    bh, bm, _ = _split3(b)
    d = lambda u, v: jnp.dot(u, v, preferred_element_type=F32)
    return d(ah, bh) + (d(ah, bm) + d(am, bh))


def _dot_sel_l(sel, a):
    ah, am, al = _split3(a)
    d = lambda v: jnp.dot(sel, v, preferred_element_type=F32)
    return d(ah) + d(am) + d(al)


def _dot_sel_r(a, sel):
    ah, am, al = _split3(a)
    d = lambda u: jnp.dot(u, sel, preferred_element_type=F32)
    return d(ah) + d(am) + d(al)


def _sigmoid(x):
    return 1.0 / (1.0 + jnp.exp(-x))


def _log_sigmoid(x):
    return jnp.minimum(x, 0.0) - jnp.log1p(jnp.exp(-jnp.abs(x)))


def _logaddexp(a, b):
    amax = jnp.maximum(a, b)
    delta = a - b
    return jnp.where(jnp.isnan(delta), a + b, amax + jnp.log1p(jnp.exp(-jnp.abs(delta))))


def _rms(x, width):
    return x * lax.rsqrt(jnp.sum(x * x, axis=-1, keepdims=True) / width + EPS)


def _cparams(sem, vmem=VMEM_LIMIT):
    return pltpu.CompilerParams(dimension_semantics=sem, vmem_limit_bytes=vmem)


def _mod_kernel(c_ref, w_ref, b_ref, o_ref):
    c = c_ref[...]
    act = c * _sigmoid(c)
    o_ref[0] = _dot_f32(act, w_ref[0]) + b_ref[0]


def _mod_vectors(c8, w_mod, b_mod):
    depth, d, six_d = w_mod.shape
    nj = six_d // d
    return pl.pallas_call(
        _mod_kernel,
        grid=(depth, nj),
        in_specs=[
            pl.BlockSpec((8, d), lambda l, j: (0, 0)),
            pl.BlockSpec((1, d, d), lambda l, j: (l, 0, j)),
            pl.BlockSpec((1, 1, d), lambda l, j: (l, 0, j)),
        ],
        out_specs=pl.BlockSpec((1, 8, d), lambda l, j: (l, 0, j)),
        out_shape=jax.ShapeDtypeStruct((depth, 8, six_d), F32),
        compiler_params=_cparams(("arbitrary", "arbitrary")),
        name="mod_vectors",
    )(c8, w_mod, b_mod.reshape(depth, 1, six_d))


def _mod_row(m_ref, row):
    return m_ref[pl.ds(row, 1), :]


def _inproj_kernel(*refs, first, nblk):
    if first:
        ctx_ref, lat_ref, m_ref, g_ref, w_ref, xo_ref, pa_ref, pb_ref, pc_ref, pd_ref = refs
    else:
        x_ref, y_ref, mprev_ref, m_ref, g_ref, w_ref, xo_ref, pa_ref, pb_ref, pc_ref, pd_ref = refs
    i = pl.program_id(0)
    b = i // nblk
    is_ctx = i % nblk == 0
    row = jnp.where(is_ctx, 4, b)
    d = D_MODEL
    if first:
        x = jnp.where(is_ctx, ctx_ref[0], lat_ref[0])
    else:
        mp = _mod_row(mprev_ref, row)
        x = x_ref[...] + mp[:, 5 * d:6 * d] * _load_token_rows(y_ref)
    xo_ref[...] = x
    m = _mod_row(m_ref, row)
    h = _rms(x, d) * g_ref[...] * (1.0 + m[:, d:2 * d]) + m[:, 0:d]
    p = _bdot(h, w_ref[...])
    pa_ref[...] = p[:, 0:W_A]
    pb_ref[...] = p[:, W_A:W_A + W_B].astype(pb_ref.dtype)
    pc_ref[...] = p[:, W_A + W_B:W_A + W_B + W_C]
    pd_ref[...] = p[:, W_A + W_B + W_C:]


def _inproj(stream, m, g, w, nblk):
    first = len(stream) == 2
    rb, d = ROW_BLOCK, D_MODEL
    row_spec = lambda w_: pl.BlockSpec((rb, w_), lambda i: (i, 0))
    full = lambda a: pl.BlockSpec(a.shape, lambda i: (0,) * a.ndim)
    if first:
        ctx, lat = stream
        t = ctx.shape[0] * nblk * rb
        specs = [pl.BlockSpec((1, rb, d), lambda i: (i // nblk, 0, 0)),
                 pl.BlockSpec((1, rb, d), lambda i: (i // nblk, jnp.maximum(i % nblk - 1, 0), 0))]
    else:
        t = stream[0].shape[0]
        specs = [row_spec(d), pl.BlockSpec((rb * TOK_TILES, LANE), lambda i: (i, 0)), full(stream[2])]
    ins = list(stream) + [m, g, w]
    specs += [full(m), full(g), full(w)]
    outs, ospecs = [jax.ShapeDtypeStruct((t, d), F32)], [row_spec(d)]
    for w_ in (W_A, W_B, W_C, W_D):
        outs.append(jax.ShapeDtypeStruct((t, w_), BF16 if w_ == W_B else F32))
        ospecs.append(row_spec(w_))
    res = pl.pallas_call(
        functools.partial(_inproj_kernel, first=first, nblk=nblk),
        grid=(t // rb,),
        in_specs=specs,
        out_specs=ospecs,
        out_shape=outs,
        compiler_params=_cparams(("arbitrary",)),
        name="in_projection",
    )(*ins)
    return res[0], res[1:]


def _sub_chunk(refs, i, tri, emat, emask_t, reverse):
    q_ref, k_ref, la_ref, v_ref, o_ref, st_ref, r_ref = refs
    hv = v_ref.shape[1]
    row_id = lax.broadcasted_iota(I32, (SUB, 1), 0)
    off = pl.multiple_of(i * SUB, SUB)
    qs = q_ref[pl.ds(off, SUB), :]
    ks = k_ref[pl.ds(off, SUB), :]
    las = la_ref[pl.ds(off, SUB), :]
    vs = v_ref[pl.ds(off, SUB), :]
    cum = _dot_sel_l(tri, las)
    last = cum[0:1, :] if reverse else cum[SUB - 1:SUB, :]
    for j in range(SUB):
        valid = (row_id <= j) if reverse else (row_id >= j)
        dlt = jnp.where(valid, cum - cum[j:j + 1, :], NEG_INF)
        r_ref[j * SUB:(j + 1) * SUB, :] = (qs * ks[j:j + 1, :] * jnp.exp(dlt)).astype(BF16)
    att = jnp.dot(r_ref[...], emat, preferred_element_type=F32)
    o = jnp.zeros((SUB, hv), F32)
    for j in range(SUB):
        o = o + att[j * SUB:(j + 1) * SUB, :] * vs[j:j + 1, :]
    st = st_ref[...]
    o = o + _bdot_nt(qs * jnp.exp(cum), st)
    o_ref[pl.ds(off, SUB), :] = o
    kd = ks * jnp.exp(last - cum)
    st_ref[...] = st * jnp.exp(last) + _bdot_tn(vs, kd) * emask_t


def _macro_step(refs, m, tri, emask_t, hmk, hmv, reverse):
    q_ref, k_ref, la_ref, v_ref, o_ref, st_ref, _ = refs
    hv = v_ref.shape[1]
    n_sub = MACRO // SUB
    off = pl.multiple_of(m * MACRO, MACRO)
    q = q_ref[pl.ds(off, MACRO), :]
    k = k_ref[pl.ds(off, MACRO), :]
    v = v_ref[pl.ds(off, MACRO), :]
    cum = _dot_sel_l(tri, la_ref[pl.ds(off, MACRO), :])
    last = cum[0:1, :] if reverse else cum[MACRO - 1:MACRO, :]
    st = st_ref[...]
    o_inter = _bdot_nt(q * jnp.exp(cum), st)
    st_ref[...] = st * jnp.exp(last) + _bdot_tn(v, k * jnp.exp(last - cum)) * emask_t
    vb = v.astype(BF16)
    row_i = lax.broadcasted_iota(I32, (N_HEADS * SUB, 1), 0) % SUB
    for s in range(n_sub):
        lo = s * SUB
        if reverse:
            k_lo, k_hi = lo, MACRO
            ref = cum[lo + SUB:lo + SUB + 1, :] if s < n_sub - 1 else jnp.zeros_like(last)
        else:
            k_lo, k_hi = 0, lo + SUB
            ref = cum[lo - 1:lo, :] if s > 0 else jnp.zeros_like(last)
        qt = q[lo:lo + SUB, :] * jnp.exp(cum[lo:lo + SUB, :] - ref)
        qs = jnp.concatenate([qt * hmk[h:h + 1, :] for h in range(N_HEADS)], axis=0).astype(BF16)
        kt = (k[k_lo:k_hi, :] * jnp.exp(ref - cum[k_lo:k_hi, :])).astype(BF16)
        att = lax.dot_general(qs, kt, (((1,), (1,)), ((), ())), preferred_element_type=F32)
        col = lax.broadcasted_iota(I32, (1, k_hi - k_lo), 1) + k_lo
        valid = (col >= lo + row_i) if reverse else (col <= lo + row_i)
        att = jnp.where(valid, att, 0.0)
        o_heads = jnp.dot(att.astype(BF16), vb[k_lo:k_hi, :], preferred_element_type=F32)
        o = o_inter[lo:lo + SUB, :]
        for h in range(N_HEADS):
            o = o + o_heads[h * SUB:(h + 1) * SUB, :] * hmv[h:h + 1, :]
        o_ref[pl.ds(off + lo, SUB), :] = o


def _scan_chains(chains, consts):
    tri_f, tri_b, trim_f, trim_b, emat, emask_t, hmk, hmv, sub_sum = consts
    rows = chains[0][0][0].shape[0]
    n_sub, n_macro = rows // SUB, rows // MACRO
    tot = None
    for refs, _ in chains:
        block_tot = jnp.dot(sub_sum, refs[2][...].astype(BF16), preferred_element_type=F32)
        tot = block_tot if tot is None else jnp.minimum(tot, block_tot)
    factorisable = jnp.min(tot) > -DECAY_GUARD

    @pl.when(factorisable)
    def _():
        def body(step, carry):
            for refs, rev in chains:
                _macro_step(refs, n_macro - 1 - step if rev else step, trim_b if rev else trim_f, emask_t, hmk, hmv, rev)
            return carry

        lax.fori_loop(0, n_macro, body, 0, unroll=True)

    @pl.when(jnp.logical_not(factorisable))
    def _():
        def body(step, carry):
            for refs, rev in chains:
                _sub_chunk(refs, n_sub - 1 - step if rev else step, tri_b if rev else tri_f, emat, emask_t, rev)
            return carry

        lax.fori_loop(0, n_sub, body, 0)


def _zero_at_first_block(*state_refs):
    @pl.when(pl.program_id(1) == 0)
    def _():
        for st in state_refs:
            st[...] = jnp.zeros_like(st)


def _scan_batches(batch):
    return SCAN_BATCHES if batch % SCAN_BATCHES == 0 else 1


def _scan_consts(hk, hv, dk, dv):
    tri = lambda n, low: jnp.asarray(np.tril(np.ones((n, n), np.float32)) if low else np.triu(np.ones((n, n), np.float32)), BF16)
    em = _head_match(hk, hv, dk, dv)
    heads = np.arange(N_HEADS)[:, None]
    hmk = (np.arange(hk)[None, :] // dk == heads).astype(np.float32)
    hmv = (np.arange(hv)[None, :] // dv == heads).astype(np.float32)
    sub_sum = (np.arange(ROW_BLOCK)[None, :] // SUB == np.arange(ROW_BLOCK // SUB)[:, None]).astype(np.float32)
    return [tri(SUB, True), tri(SUB, False), tri(MACRO, True), tri(MACRO, False), jnp.asarray(em, BF16),
            jnp.asarray(em.T, F32), jnp.asarray(hmk), jnp.asarray(hmv), jnp.asarray(sub_sum, BF16)]


def _hgrn_prep(q_ref, v_ref, f_ref, lb, qo, ko, lo, vo):
    qr = q_ref[...]
    qo[...] = qr * _sigmoid(qr) * (A_DH ** -0.5)
    vo[...] = v_ref[...]
    z = f_ref[...]
    lo[...] = _logaddexp(jnp.log(lb), jnp.log1p(-lb) + _log_sigmoid(z))
    ko[...] = (1.0 - lb) * _sigmoid(-z)


def _hgrn_kernel(qf_ref, vf_ref, ff_ref, qb_ref, vb_ref, fb_ref, lb_ref, *rest):
    consts, (of_ref, ob_ref), scratch = rest[:N_SCAN_CONSTS], rest[N_SCAN_CONSTS:N_SCAN_CONSTS + 2], rest[N_SCAN_CONSTS + 2:]
    chains = []
    for s in range(qf_ref.shape[0]):
        stf, stb, rf, rb_, q1, k1, l1, v1, q2, k2, l2, v2 = scratch[12 * s:12 * s + 12]
        _zero_at_first_block(stf, stb)
        _hgrn_prep(qf_ref.at[s], vf_ref.at[s], ff_ref.at[s], lb_ref[0:1, :], q1, k1, l1, v1)
        _hgrn_prep(qb_ref.at[s], vb_ref.at[s], fb_ref.at[s], lb_ref[1:2, :], q2, k2, l2, v2)
        chains += [((q1, k1, l1, v1, of_ref.at[s], stf, rf), False), ((q2, k2, l2, v2, ob_ref.at[s], stb, rb_), True)]
    _scan_chains(chains, [c[...] for c in consts])


def _head_match(hk, hv, dk, dv):
    m = (np.arange(hk)[:, None] // dk == np.arange(hv)[None, :] // dv).astype(np.float32)
    return m


def _bwd_block(t, nblk):
    return jnp.where(t == 0, 0, nblk - t)


def _hgrn_scan(pa, lb, batch, nblk):
    rb, w = ROW_BLOCK, D_GROUP
    consts = _scan_consts(w, w, A_DH, A_DH)
    nb = _scan_batches(batch)
    pa3 = pa.reshape(batch, nblk * rb, pa.shape[1])
    fwd = lambda col: pl.BlockSpec((nb, rb, w), lambda b, t: (b, t, col))
    bwd = lambda col: pl.BlockSpec((nb, rb, w), lambda b, t: (b, _bwd_block(t, nblk), col))
    full = lambda a: pl.BlockSpec(a.shape, lambda b, t: (0,) * a.ndim)
    vm = lambda shape, dt=F32: pltpu.VMEM(shape, dt)
    of, ob = pl.pallas_call(
        _hgrn_kernel,
        grid=(batch // nb, nblk),
        in_specs=[fwd(0), fwd(1), fwd(2), bwd(0), bwd(1), bwd(3), full(lb)] + [full(c) for c in consts],
        out_specs=[fwd(0), bwd(0)],
        out_shape=[jax.ShapeDtypeStruct((batch, nblk * rb, w), F32)] * 2,
        scratch_shapes=([vm((w, w)), vm((w, w)), vm((SUB * SUB, w), BF16), vm((SUB * SUB, w), BF16)]
                        + [vm((rb, w))] * 8) * nb,
        compiler_params=_cparams(("arbitrary", "arbitrary")),
        name="hgrn_scan",
    )(pa3, pa3, pa3, pa3, pa3, pa3, lb, *consts)
    return of.reshape(-1, w), ob.reshape(-1, w)


def _gla_prep(q_ref, k_ref, v_ref, z_ref, wg_ref, bg_ref, qo, ko, lo, vo):
    qo[...] = q_ref[...] * (C_DK ** -0.5)
    ko[...] = k_ref[...]
    vo[...] = v_ref[...]
    zl = _dot_f32(z_ref[...], wg_ref[...]) + bg_ref[...]
    lo[...] = _log_sigmoid(zl) / C_GATE_NORM


def _gla_kernel(qf_ref, kf_ref, vf_ref, zf_ref, qb_ref, kb_ref, vb_ref, zb_ref, wgf_ref, bgf_ref, wgb_ref, bgb_ref, *rest):
    consts, (of_ref, ob_ref), scratch = rest[:N_SCAN_CONSTS], rest[N_SCAN_CONSTS:N_SCAN_CONSTS + 2], rest[N_SCAN_CONSTS + 2:]
    chains = []
    for s in range(qf_ref.shape[0]):
        stf, stb, rf, rb_, q1, k1, l1, v1, q2, k2, l2, v2 = scratch[12 * s:12 * s + 12]
        _zero_at_first_block(stf, stb)
        _gla_prep(qf_ref.at[s], kf_ref.at[s], vf_ref.at[s], zf_ref.at[s], wgf_ref, bgf_ref, q1, k1, l1, v1)
        _gla_prep(qb_ref.at[s], kb_ref.at[s], vb_ref.at[s], zb_ref.at[s], wgb_ref, bgb_ref, q2, k2, l2, v2)
        chains += [((q1, k1, l1, v1, of_ref.at[s], stf, rf), False), ((q2, k2, l2, v2, ob_ref.at[s], stb, rb_), True)]
    _scan_chains(chains, [c[...] for c in consts])


def _gla_scan(pc, wg_f, bg_f, wg_b, bg_b, batch, nblk):
    rb = ROW_BLOCK
    hk, hv = N_HEADS * C_DK, N_HEADS * C_DV
    consts = _scan_consts(hk, hv, C_DK, C_DV)
    wgf = jnp.zeros((LANE, hk), F32).at[0:C_GATE_RANK].set(wg_f)
    wgb = jnp.zeros((LANE, hk), F32).at[C_GATE_RANK:2 * C_GATE_RANK].set(wg_b)
    bgf, bgb = bg_f.reshape(1, hk), bg_b.reshape(1, hk)
    nb = _scan_batches(batch)
    pc3 = pc.reshape(batch, nblk * rb, pc.shape[1])
    fwd = lambda w, col: pl.BlockSpec((nb, rb, w), lambda b, t: (b, t, col))
    bwd = lambda w, col: pl.BlockSpec((nb, rb, w), lambda b, t: (b, _bwd_block(t, nblk), col))
    full = lambda a: pl.BlockSpec(a.shape, lambda b, t: (0,) * a.ndim)
    vm = lambda shape, dt=F32: pltpu.VMEM(shape, dt)
    of, ob = pl.pallas_call(
        _gla_kernel,
        grid=(batch // nb, nblk),
        in_specs=[fwd(hk, 0), fwd(hk, 1), fwd(hv, 1), fwd(LANE, 6), bwd(hk, 0), bwd(hk, 1), bwd(hv, 1), bwd(LANE, 6),
                  full(wgf), full(bgf), full(wgb), full(bgb)] + [full(c) for c in consts],
        out_specs=[fwd(hv, 0), bwd(hv, 0)],
        out_shape=[jax.ShapeDtypeStruct((batch, nblk * rb, hv), F32)] * 2,
        scratch_shapes=([vm((hv, hk)), vm((hv, hk)), vm((SUB * SUB, hk), BF16), vm((SUB * SUB, hk), BF16)]
                        + [vm((rb, hk)), vm((rb, hk)), vm((rb, hk)), vm((rb, hv))] * 2) * nb,
        compiler_params=_cparams(("arbitrary", "arbitrary")),
        name="gla_scan",
    )(pc3, pc3, pc3, pc3, pc3, pc3, pc3, pc3, wgf, bgf, wgb, bgb, *consts)
    return of.reshape(-1, hv), ob.reshape(-1, hv)


NA_QROWS = ROW_BLOCK // GRID_W
NA_WROWS = WIN_H + NA_QROWS


def _na_window_rows(rows):
    kh = WIN_H
    patterns = []
    for r0 in (0, NA_QROWS, rows - NA_QROWS):
        ws = int(np.clip(r0 - kh // 2, 0, rows - NA_WROWS))
        seen = {}
        for a in range(NA_QROWS):
            r = r0 + a
            s = int(np.clip(r - kh // 2, 0, rows - kh))
            for jj in range(NA_WROWS):
                if s <= ws + jj < s + kh:
                    seen[(a, jj)] = ws + jj - r + kh - 1
        patterns.append(seen)
    return patterns


def _na_kernel(q_ref, k_ref, v_ref, by_col_ref, hm_ref, o_ref, bias_scr, *, j0, rows, ctx):
    j = pl.program_id(1) + j0
    q = q_ref[...] * (B_DH ** -0.5)
    hm = hm_ref[...]
    kc = k_ref[0:ctx, :]
    vc = v_ref[0:ctx, :]
    nt = lambda a, b: lax.dot_general(a, b, (((1,), (1,)), ((), ())), preferred_element_type=F32)

    @pl.when(j == 0)
    def _():
        acc = jnp.zeros(q.shape, F32)
        for h in range(N_HEADS):
            mh = hm[h:h + 1, :]
            s = nt(q * mh.astype(BF16), kc)
            p = jnp.exp(s - jnp.max(s, axis=-1, keepdims=True))
            inv = 1.0 / jnp.sum(p, axis=-1, keepdims=True)
            acc = acc + jnp.dot(p.astype(BF16), vc, preferred_element_type=F32) * (mh * inv)
        o_ref[...] = acc.astype(o_ref.dtype)

    @pl.when(j > 0)
    def _():
        for position, (at_step, seen) in enumerate(zip((1, 2, rows // NA_QROWS), _na_window_rows(rows))):
            @pl.when(j == at_step)
            def _(seen=seen):
                blocked = jnp.full((GRID_W, GRID_W), NEG_INF, F32)
                for h in range(N_HEADS):
                    for a in range(NA_QROWS):
                        for jj in range(NA_WROWS):
                            slab = by_col_ref[h, seen[(a, jj)]] if (a, jj) in seen else blocked
                            bias_scr[h, a * GRID_W:(a + 1) * GRID_W, jj * GRID_W:(jj + 1) * GRID_W] = slab

        r0 = (j - 1) * NA_QROWS
        start = jnp.clip(r0 - WIN_H // 2, 0, rows - NA_WROWS)
        off = pl.multiple_of(ctx + start * GRID_W, GRID_W)
        kw = k_ref[pl.ds(off, NA_WROWS * GRID_W), :]
        vw = v_ref[pl.ds(off, NA_WROWS * GRID_W), :]
        acc = jnp.zeros(q.shape, F32)
        for h in range(N_HEADS):
            mh = hm[h:h + 1, :]
            qh = q * mh.astype(BF16)
            sw = nt(qh, kw) + bias_scr[h]
            sc = nt(qh, kc)
            m = jnp.maximum(jnp.max(sw, axis=-1, keepdims=True), jnp.max(sc, axis=-1, keepdims=True))
            pw = jnp.exp(sw - m)
            pc_ = jnp.exp(sc - m)
            inv = 1.0 / (jnp.sum(pw, axis=-1, keepdims=True) + jnp.sum(pc_, axis=-1, keepdims=True))
            o = (jnp.dot(pw.astype(BF16), vw, preferred_element_type=F32)
                 + jnp.dot(pc_.astype(BF16), vc, preferred_element_type=F32))
            acc = acc + o * (mh * inv)
        o_ref[...] = acc.astype(o_ref.dtype)


def _na_bias_by_column(rpb):
    cidx = np.arange(GRID_W)
    c_start = np.clip(cidx - WIN_W // 2, 0, GRID_W - WIN_W)
    col_in = (cidx[None] >= c_start[:, None]) & (cidx[None] < c_start[:, None] + WIN_W)
    dc = np.clip(cidx[None] - cidx[:, None], -(WIN_W - 1), WIN_W - 1) + (WIN_W - 1)
    sel = (dc[None] == np.arange(2 * WIN_W - 1)[:, None, None]).astype(np.float32)
    by_col = jnp.einsum("hrc,cqw->hrqw", rpb.astype(F32), jnp.asarray(sel), precision=lax.Precision.HIGHEST)
    return jnp.where(jnp.asarray(col_in)[None, None], by_col, NEG_INF)


def _head_masks(width, dh):
    return jnp.asarray((np.arange(width)[None, :] // dh == np.arange(N_HEADS)[:, None]).astype(np.float32))


def _neighbourhood_attention(pb, rpb, batch, s_len, ctx, keep_ctx):
    rows = (s_len - ctx) // GRID_W
    assert rows >= NA_WROWS and rows % NA_QROWS == 0 and ctx == ROW_BLOCK
    rb = ROW_BLOCK
    j0 = 0 if keep_ctx else 1
    per_b = s_len // rb
    by_col = _na_bias_by_column(rpb)
    hm = _head_masks(D_GROUP, B_DH)

    return pl.pallas_call(
        functools.partial(_na_kernel, j0=j0, rows=rows, ctx=ctx),
        grid=(batch, per_b - j0),
        in_specs=[
            pl.BlockSpec((rb, D_GROUP), lambda b, jj: (b * per_b + jj + j0, 0)),
            pl.BlockSpec((s_len, D_GROUP), lambda b, jj: (b, 1)),
            pl.BlockSpec((s_len, D_GROUP), lambda b, jj: (b, 2)),
            pl.BlockSpec(by_col.shape, lambda b, jj: (0, 0, 0, 0)),
            pl.BlockSpec(hm.shape, lambda b, jj: (0, 0)),
        ],
        out_specs=pl.BlockSpec((rb, D_GROUP), lambda b, jj: (b * (per_b - j0) + jj, 0)),
        out_shape=jax.ShapeDtypeStruct((batch * (per_b - j0) * rb, D_GROUP), BF16),
        scratch_shapes=[pltpu.VMEM((N_HEADS, rb, NA_WROWS * GRID_W), F32)],
        compiler_params=_cparams(("arbitrary", "arbitrary")),
        name="neighbourhood_attention",
    )(pb, pb, pb, by_col, hm)


def _mla_prep_kernel(pd_ref, gq_ref, gkv_ref, wq1_ref, wq2_ref, wk_ref, wvt_ref, ones_ref, cq_ref, sq_ref, tk_ref,
                     place_ref, q_ref, k_ref, vt_ref):
    pd = pd_ref[...]
    cq = pd[:, 0:256]
    ckv = pd[:, 256:384]
    kr = pd[:, 384:512]
    qn = _rms(cq, D_Q_RANK) * gq_ref[...]
    q = _bdot(qn, wq1_ref[...]) * cq_ref[...] + _bdot(qn, wq2_ref[...]) * sq_ref[...]
    q_ref[...] = q.astype(BF16)
    kvn = (_rms(ckv, D_KV_RANK) * gkv_ref[...]).astype(BF16)
    k = jnp.dot(kvn, wk_ref[...], preferred_element_type=F32) + _dot_sel_r(kr * tk_ref[...], place_ref[...])
    k_ref[...] = k.astype(BF16)
    vt = lax.dot_general(wvt_ref[...], kvn, (((1,), (1,)), ((), ())), preferred_element_type=F32) + ones_ref[...]
    vt_ref[0] = vt.astype(BF16)


def _rope_swap_perm():
    f = ROPE_FREQS
    return np.concatenate([np.arange(f, 2 * f), np.arange(0, f), np.arange(3 * f, 4 * f), np.arange(2 * f, 3 * f)])


def _mla_tables(n, ctx):
    t = np.arange(n)
    inv_freq = ROPE_BASE ** (-np.arange(ROPE_FREQS, dtype=np.float32) / ROPE_FREQS)
    ang_r = (t // GRID_W).astype(np.float32)[:, None] * inv_freq
    ang_c = (t % GRID_W).astype(np.float32)[:, None] * inv_freq
    cos32 = np.concatenate([np.cos(ang_r), np.cos(ang_r), np.cos(ang_c), np.cos(ang_c)], axis=1)
    sin32 = np.concatenate([-np.sin(ang_r), np.sin(ang_r), -np.sin(ang_c), np.sin(ang_c)], axis=1)
    cos32 = np.concatenate([np.ones((ctx, D_ROPE), np.float32), cos32.astype(np.float32)], axis=0)
    sin32 = np.concatenate([np.zeros((ctx, D_ROPE), np.float32), sin32.astype(np.float32)], axis=0)
    s_len = n + ctx
    cq = np.zeros((s_len, N_HEADS, HEAD_TILE), np.float32)
    sq = np.zeros((s_len, N_HEADS, HEAD_TILE), np.float32)
    cq[:, :, 0:D_NOPE] = MLA_SCALE
    cq[:, :, D_NOPE:D_NOPE + D_ROPE] = cos32[:, None, :] * MLA_SCALE
    sq[:, :, D_NOPE:D_NOPE + D_ROPE] = sin32[:, None, :] * MLA_SCALE
    tk = np.zeros((s_len, LANE), np.float32)
    tk[:, 0:D_ROPE] = cos32
    tk[:, D_ROPE:2 * D_ROPE] = sin32
    place = np.zeros((LANE, N_HEADS * HEAD_TILE), np.float32)
    for h in range(N_HEADS):
        for l in range(D_ROPE):
            place[l, h * HEAD_TILE + D_NOPE + l] = 1.0
            place[D_ROPE + l, h * HEAD_TILE + D_NOPE + l] = 1.0
    return (jnp.asarray(cq.reshape(s_len, -1)), jnp.asarray(sq.reshape(s_len, -1)), jnp.asarray(tk),
            jnp.asarray(place, BF16))


def _mla_weights(q_norm_g, w_uq, kv_norm_g, w_ukv):
    perm = _rope_swap_perm()
    wq = w_uq.reshape(D_Q_RANK, N_HEADS, D_NOPE + D_ROPE)
    row_pad = (0, 256 - D_Q_RANK)
    wq1 = jnp.pad(wq, (row_pad, (0, 0), (0, HEAD_TILE - D_NOPE - D_ROPE)))
    wq2 = jnp.pad(wq[:, :, D_NOPE:][:, :, perm], (row_pad, (0, 0), (D_NOPE, HEAD_TILE - D_NOPE - D_ROPE)))
    wkv = w_ukv.reshape(D_KV_RANK, N_HEADS, D_NOPE + D_V)
    wk = jnp.pad(wkv[:, :, 0:D_NOPE], ((0, 0), (0, 0), (0, HEAD_TILE - D_NOPE)))
    wvt = jnp.pad(jnp.transpose(wkv[:, :, D_NOPE:], (1, 2, 0)), ((0, 0), (0, VT_ROWS - D_V), (0, 0)))
    wvt = wvt.reshape(N_HEADS * VT_ROWS, D_KV_RANK)
    ones_rows = jnp.asarray((np.arange(N_HEADS * VT_ROWS) % VT_ROWS >= D_V).astype(np.float32)).reshape(-1, 1)
    hw = N_HEADS * HEAD_TILE
    gq = jnp.pad(q_norm_g.reshape(1, D_Q_RANK), ((0, 0), row_pad))
    return (gq, kv_norm_g.reshape(1, D_KV_RANK), wq1.reshape(256, hw).astype(BF16), wq2.reshape(256, hw).astype(BF16),
            wk.reshape(D_KV_RANK, hw).astype(BF16), wvt.astype(BF16), ones_rows)


def _mla_prep(pd, weights, tables, nblk):
    t_all = pd.shape[0]
    rb = ROW_BLOCK
    gq, gkv, wq1, wq2, wk, wvt, ones_rows = weights
    cq, sq, tk, place = tables
    hw = N_HEADS * HEAD_TILE
    hv = N_HEADS * VT_ROWS
    row = lambda w: pl.BlockSpec((rb, w), lambda i: (i, 0))
    pos = lambda w: pl.BlockSpec((rb, w), lambda i: (i % nblk, 0))
    full = lambda a: pl.BlockSpec(a.shape, lambda i: (0,) * a.ndim)
    return pl.pallas_call(
        _mla_prep_kernel,
        grid=(t_all // rb,),
        in_specs=[row(W_D), full(gq), full(gkv), full(wq1), full(wq2), full(wk), full(wvt), full(ones_rows),
                  pos(hw), pos(hw), pos(LANE), full(place)],
        out_specs=[row(hw), row(hw), pl.BlockSpec((1, hv, rb), lambda i: (i // nblk, 0, i % nblk))],
        out_shape=[jax.ShapeDtypeStruct((t_all, hw), BF16), jax.ShapeDtypeStruct((t_all, hw), BF16),
                   jax.ShapeDtypeStruct((t_all // (nblk * rb), hv, nblk * rb), BF16)],
        compiler_params=_cparams(("arbitrary",)),
        name="mla_prep",
    )(pd, gq, gkv, wq1, wq2, wk, wvt, ones_rows, cq, sq, tk, place)


MLA_HEADS_PER_STEP = 4
MLA_KEY_CHUNK = 256


def _mla_attn_kernel(q_ref, k_ref, vt_ref, o_ref, s_scr, *, j0, ctx):
    j = pl.program_id(2) + j0
    rb = q_ref.shape[0]

    def heads(n_keys):
        hs = range(MLA_HEADS_PER_STEP)
        chunks = [slice(c * MLA_KEY_CHUNK, (c + 1) * MLA_KEY_CHUNK) for c in range(n_keys // MLA_KEY_CHUNK)]
        lanes = [slice(h * HEAD_TILE, (h + 1) * HEAD_TILE) for h in hs]
        qs = [q_ref[:, lanes[h]] for h in hs]
        m = [jnp.full((1, rb), NEG_INF, F32) for _ in hs]
        for keys in chunks:
            for h in hs:
                st = lax.dot_general(k_ref[keys, lanes[h]], qs[h], (((1,), (1,)), ((), ())),
                                     preferred_element_type=F32)
                s_scr[h, keys, :] = st
                m[h] = jnp.maximum(m[h], jnp.max(st, axis=0, keepdims=True))
        acc = [jnp.zeros((VT_ROWS, rb), F32) for _ in hs]
        for keys in chunks:
            for h in hs:
                pt = jnp.exp(s_scr[h, keys, :] - m[h]).astype(BF16)
                acc[h] = acc[h] + jnp.dot(vt_ref[0, h * VT_ROWS:(h + 1) * VT_ROWS, keys], pt,
                                          preferred_element_type=F32)
        for h in hs:
            o_ref[0, h * D_V:(h + 1) * D_V, :] = (acc[h][0:D_V, :] * (1.0 / acc[h][D_V:D_V + 1, :])).astype(o_ref.dtype)

    @pl.when(j == 0)
    def _():
        heads(ctx)

    @pl.when(j > 0)
    def _():
        heads(k_ref.shape[0])


def _mla_attention(q, k, vt, batch, s_len, ctx, keep_ctx):
    rb = ROW_BLOCK
    nblk = s_len // rb
    j0 = 0 if keep_ctx else 1
    hps = MLA_HEADS_PER_STEP
    return pl.pallas_call(
        functools.partial(_mla_attn_kernel, j0=j0, ctx=ctx),
        grid=(batch, N_HEADS // hps, nblk - j0),
        in_specs=[
            pl.BlockSpec((rb, hps * HEAD_TILE), lambda b, h, jj: (b * nblk + jj + j0, h)),
            pl.BlockSpec((s_len, hps * HEAD_TILE), lambda b, h, jj: (b, h)),
            pl.BlockSpec((1, hps * VT_ROWS, s_len), lambda b, h, jj: (b, h, 0)),
        ],
        out_specs=pl.BlockSpec((1, hps * D_V, rb), lambda b, h, jj: (b * (nblk - j0) + jj, h, 0)),
        out_shape=jax.ShapeDtypeStruct((batch * (nblk - j0), N_HEADS * D_V, rb), BF16),
        scratch_shapes=[pltpu.VMEM((hps, s_len, rb), F32)],
        compiler_params=_cparams(("arbitrary", "arbitrary", "arbitrary")),
        name="mla_attention",
    )(q, k, vt)


def _outproj_kernel(oaf_ref, oab_ref, ga_ref, ob_ref, ogf_ref, ogb_ref, gc_ref, od_ref, x_ref, m_ref, gha_ref, ghc_ref,
                    g2_ref, wa_ref, wb_ref, wc_ref, wd_ref, em_ref, x1_ref, h2_ref, *, j0, nblk_out):
    i = pl.program_id(0)
    b = i // nblk_out
    row = jnp.where(i % nblk_out + j0 == 0, 4, b)
    d = D_MODEL
    em = em_ref[...]

    def readout(o, g_norm, gate, dh):
        ms = _dot_sel_r(o * o, em) / dh
        return o * lax.rsqrt(ms + EPS) * g_norm * (gate * _sigmoid(gate))

    a = readout(oaf_ref[...] + oab_ref[...], gha_ref[...], ga_ref[...], A_DH)
    c = readout(ogf_ref[...] + ogb_ref[...], ghc_ref[...], gc_ref[...], C_DV)
    mix = (_bdot(a, wa_ref[...]) + jnp.dot(ob_ref[...], wb_ref[...], preferred_element_type=F32)
           + _bdot(c, wc_ref[...])
           + lax.dot_general(od_ref[0], wd_ref[...], (((0,), (0,)), ((), ())), preferred_element_type=F32))
    m = _mod_row(m_ref, row)
    x1 = x_ref[...] + m[:, 2 * d:3 * d] * mix
    x1_ref[...] = x1
    _store_token_rows(h2_ref, _rms(x1, d) * g2_ref[...] * (1.0 + m[:, 4 * d:5 * d]) + m[:, 3 * d:4 * d])


def _outproj(oaf, oab, pa, ob, ogf, ogb, pc, od, x, m, gha, ghc, g2, w_out, batch, nblk, keep_ctx):
    rb, d = ROW_BLOCK, D_MODEL
    j0 = 0 if keep_ctx else 1
    nblk_out = nblk - j0
    t_out = batch * nblk_out * rb
    wa = w_out[0:256].astype(BF16)
    wb = w_out[256:512].astype(BF16)
    wc = w_out[512:768].astype(BF16)
    wd = w_out[768:1024].astype(BF16)
    em = jnp.asarray(_head_match(D_GROUP, D_GROUP, 64, 64), BF16)
    src = lambda i: (i // nblk_out) * nblk + i % nblk_out + j0
    row_in = lambda w, col=0: pl.BlockSpec((rb, w), lambda i: (src(i), col))
    row_out = lambda w: pl.BlockSpec((rb, w), lambda i: (i, 0))
    full = lambda a_: pl.BlockSpec(a_.shape, lambda i: (0,) * a_.ndim)
    return pl.pallas_call(
        functools.partial(_outproj_kernel, j0=j0, nblk_out=nblk_out),
        grid=(t_out // rb,),
        in_specs=[row_in(256), row_in(256), row_in(256, 4), row_out(256), row_in(256), row_in(256), row_in(256, 2),
                  pl.BlockSpec((1, N_HEADS * D_V, rb), lambda i: (i, 0, 0)), row_in(d), full(m), full(gha), full(ghc), full(g2),
                  full(wa), full(wb), full(wc), full(wd), full(em)],
        out_specs=[row_out(d), pl.BlockSpec((rb * TOK_TILES, LANE), lambda i: (i, 0))],
        out_shape=[jax.ShapeDtypeStruct((t_out, d), F32), jax.ShapeDtypeStruct((t_out * TOK_TILES, LANE), F32)],
        compiler_params=_cparams(("arbitrary",)),
        name="out_projection",
    )(oaf, oab, pa, ob, ogf, ogb, pc, od, x, m, gha, ghc, g2, wa, wb, wc, wd, em)


def _load_token_rows(ref):
    rows = ref.shape[0] // TOK_TILES
    return jnp.concatenate([ref[pl.ds(k, rows, stride=TOK_TILES), :] for k in range(TOK_TILES)], axis=1)


def _store_token_rows(ref, val):
    rows = val.shape[0]
    for k in range(TOK_TILES):
        ref[pl.ds(k, rows, stride=TOK_TILES), :] = val[:, k * LANE:(k + 1) * LANE]


def _router_logits(h, wr_ref, br_ref):
    lg = _dot_3pass(h, wr_ref[...]) + br_ref[...]
    lane = lax.broadcasted_iota(I32, lg.shape, 1).astype(F32)
    return lg, lane


def _top_group(lg, lane):
    gl = jnp.where(lane < N_GROUPS, lg, NEG_INF)
    gmax = jnp.max(gl, axis=-1, keepdims=True)
    gsel = jnp.min(jnp.where(gl == gmax, lane, float(LANE)), axis=-1, keepdims=True)
    p_group = 1.0 / jnp.sum(jnp.exp(gl - gmax), axis=-1, keepdims=True)
    return gsel, p_group


def _expert_gates(lg, lane, lo, p_group):
    big = float(LANE)
    el = jnp.where((lane >= lo) & (lane < lo + EXPERTS_PER_GROUP), lg, NEG_INF)
    m1 = jnp.max(el, axis=-1, keepdims=True)
    i1 = jnp.min(jnp.where(el == m1, lane, big), axis=-1, keepdims=True)
    el2 = jnp.where(lane == i1, NEG_INF, el)
    m2 = jnp.max(el2, axis=-1, keepdims=True)
    i2 = jnp.min(jnp.where(el2 == m2, lane, big), axis=-1, keepdims=True)
    t = jnp.exp(m2 - m1)
    w1 = p_group / (1.0 + t)
    w2 = p_group * t / (1.0 + t)
    return jnp.where(lane == i1 - lo, w1, jnp.where(lane == i2 - lo, w2, 0.0))


def _router_kernel(h_ref, wr_ref, br_ref, tril_ref, meta_ref, cnt_ref, carry):
    @pl.when(pl.program_id(0) == 0)
    def _():
        carry[...] = jnp.zeros_like(carry)

    lg, lane = _router_logits(_load_token_rows(h_ref), wr_ref, br_ref)
    gsel, _ = _top_group(lg, lane)
    onehot = jnp.where(lane == gsel, 1.0, 0.0)
    incl = jnp.dot(tril_ref[...], onehot.astype(BF16), preferred_element_type=F32)
    rank = jnp.sum(onehot * (incl - 1.0 + carry[...]), axis=-1, keepdims=True)
    carry[...] = carry[...] + jnp.sum(onehot, axis=0, keepdims=True)
    meta_ref[...] = jnp.where(lane == 0, gsel, jnp.where(lane == 1, rank, 0.0)).astype(I32)
    cnt_ref[...] = carry[...].astype(I32)


def _router_weights(w_rg, b_rg, w_re, b_re):
    ne = N_GROUPS * EXPERTS_PER_GROUP
    pad = ((0, 0), (0, LANE - N_GROUPS - ne))
    wr = jnp.pad(jnp.concatenate([w_rg, w_re], axis=1).astype(F32), pad)
    br = jnp.pad(jnp.concatenate([b_rg, b_re]).astype(F32).reshape(1, -1), pad)
    return wr, br


def _router(h2t, wr, br):
    t = h2t.shape[0] // TOK_TILES
    rb = ROW_BLOCK
    tril = jnp.asarray(np.tril(np.ones((rb, rb), np.float32)), BF16)
    full = lambda a: pl.BlockSpec(a.shape, lambda i: (0,) * a.ndim)
    return pl.pallas_call(
        _router_kernel,
        grid=(t // rb,),
        in_specs=[pl.BlockSpec((rb * TOK_TILES, LANE), lambda i: (i, 0)), full(wr), full(br), full(tril)],
        out_specs=[pl.BlockSpec((rb, LANE), lambda i: (i, 0)), pl.BlockSpec((1, LANE), lambda i: (0, 0))],
        out_shape=[jax.ShapeDtypeStruct((t, LANE), I32), jax.ShapeDtypeStruct((1, LANE), I32)],
        scratch_shapes=[pltpu.VMEM((1, LANE), F32)],
        compiler_params=_cparams(("arbitrary",)),
        name="moe_router",
    )(h2t, wr, br, tril)


def _invert_kernel(dest_ref, inv_ref, *, n_tok):
    def spare(s, c):
        inv_ref[s] = n_tok + (s & (ROW_BLOCK - 1))
        return c

    lax.fori_loop(0, inv_ref.shape[0], spare, 0, unroll=8)

    def put(t, c):
        inv_ref[dest_ref[t]] = t
        return c

    lax.fori_loop(0, n_tok, put, 0, unroll=8)


def _invert(dest, n_slots):
    smem = pl.BlockSpec(memory_space=pltpu.SMEM)
    return pl.pallas_call(
        functools.partial(_invert_kernel, n_tok=dest.shape[0]), in_specs=[smem], out_specs=smem,
        out_shape=jax.ShapeDtypeStruct((n_slots,), I32), name="moe_invert",
    )(dest)


def _token_copy(src_ref, dst_ref, s, d, sem):
    s8 = pl.multiple_of(s * TOK_TILES, TOK_TILES)
    d8 = pl.multiple_of(d * TOK_TILES, TOK_TILES)
    return pltpu.make_async_copy(src_ref.at[pl.ds(s8, TOK_TILES), :], dst_ref.at[pl.ds(d8, TOK_TILES), :], sem)


def _experts_kernel(bg_ref, inv_ref, h_ref, wr_ref, br_ref, wgu_ref, wdn_ref, y_ref, xbuf, ybuf, gsem, ssem, *, n_tok):
    i = pl.program_id(0)
    n_steps = pl.num_programs(0)
    rb, d = ROW_BLOCK, D_MODEL
    slot = i % 2
    other = 1 - slot
    nxt = jnp.minimum(i + 1, n_steps - 1)
    prev = jnp.maximum(i - 1, 0)
    block_rows = rb * TOK_TILES

    def dma_priority(r):
        return r % 2 if isinstance(r, int) else 0

    def gather_start(blk, sl, r):
        tok = jnp.minimum(inv_ref[blk * rb + r], n_tok - 1)
        _token_copy(h_ref, xbuf.at[sl], tok, r, gsem.at[sl]).start(priority=dma_priority(r))

    def scatter_start(blk, sl, r, to_spare):
        dst = jnp.where(to_spare, n_tok + r, inv_ref[blk * rb + r])
        _token_copy(ybuf.at[sl], y_ref, r, dst, ssem.at[sl]).start(priority=dma_priority(r))

    def gather_wait(sl):
        pltpu.make_async_copy(h_ref.at[pl.ds(0, block_rows), :], xbuf.at[sl], gsem.at[sl]).wait()

    def scatter_wait(sl):
        pltpu.make_async_copy(ybuf.at[sl], y_ref.at[pl.ds(0, block_rows), :], ssem.at[sl]).wait()

    @pl.when(i == 0)
    def _():
        ybuf[...] = jnp.zeros_like(ybuf)

        def one(r, c):
            gather_start(0, 0, r)
            return c
        lax.fori_loop(0, rb, one, 0, unroll=8)

    gather_wait(slot)

    @pl.when(i >= 1)
    def _():
        scatter_wait(slot)

    xf = _load_token_rows(xbuf.at[slot])
    lg, lane = _router_logits(xf, wr_ref, br_ref)
    _, p_group = _top_group(lg, lane)
    lo = (N_GROUPS + bg_ref[i] * EXPERTS_PER_GROUP).astype(F32)
    gates = _expert_gates(lg, lane, lo, p_group)
    x = xf.astype(BF16)
    per_expert = rb // EXPERTS_PER_GROUP
    hidden = []
    for e in range(EXPERTS_PER_GROUP):
        gu = jnp.dot(x, wgu_ref[0, e], preferred_element_type=F32)
        g = gu[:, 0:D_EXPERT]
        hidden.append((g * _sigmoid(g) * gu[:, D_EXPERT:] * gates[:, e:e + 1]).astype(BF16))
        for r in range(e * per_expert, (e + 1) * per_expert):
            gather_start(nxt, other, r)
            scatter_start(prev, other, r, i == 0)
    y = jnp.dot(jnp.concatenate(hidden, axis=1), wdn_ref[0], preferred_element_type=F32)
    _store_token_rows(ybuf.at[slot], y)

    @pl.when(i == n_steps - 1)
    def _():
        scatter_wait(other)

        def one(r, c):
            scatter_start(i, slot, r, False)
            return c
        lax.fori_loop(0, rb, one, 0, unroll=8)
        scatter_wait(slot)
        gather_wait(other)


def _experts(block_group, inv, h2t, wr, br, w_gu, w_dn):
    n_blocks = block_group.shape[0]
    rb, d = ROW_BLOCK, D_MODEL
    n_tok = h2t.shape[0] // TOK_TILES
    wgu = w_gu.reshape(N_GROUPS, EXPERTS_PER_GROUP, d, 2 * D_EXPERT)
    wdn = w_dn.reshape(N_GROUPS, EXPERTS_PER_GROUP * D_EXPERT, d)
    any_spec = pl.BlockSpec(memory_space=pl.ANY)
    return pl.pallas_call(
        functools.partial(_experts_kernel, n_tok=n_tok),
        grid_spec=pltpu.PrefetchScalarGridSpec(
            num_scalar_prefetch=2, grid=(n_blocks,),
            in_specs=[
                any_spec,
                pl.BlockSpec(wr.shape, lambda i, bg, inv_: (0, 0)),
                pl.BlockSpec(br.shape, lambda i, bg, inv_: (0, 0)),
                pl.BlockSpec((1, EXPERTS_PER_GROUP, d, 2 * D_EXPERT), lambda i, bg, inv_: (bg[i], 0, 0, 0)),
                pl.BlockSpec((1, EXPERTS_PER_GROUP * D_EXPERT, d), lambda i, bg, inv_: (bg[i], 0, 0)),
            ],
            out_specs=any_spec,
            scratch_shapes=[pltpu.VMEM((2, rb * TOK_TILES, LANE), F32), pltpu.VMEM((2, rb * TOK_TILES, LANE), F32),
                            pltpu.SemaphoreType.DMA((2,)), pltpu.SemaphoreType.DMA((2,))]),
        out_shape=jax.ShapeDtypeStruct(((n_tok + rb) * TOK_TILES, LANE), F32),
        compiler_params=pltpu.CompilerParams(dimension_semantics=("arbitrary",), vmem_limit_bytes=VMEM_LIMIT,
                                             has_side_effects=True),
        name="moe_experts",
    )(block_group, inv, h2t, wr, br, wgu, wdn)


def _moe(h2t, w_rg, b_rg, w_re, b_re, w_gu_bf16, w_dn_bf16):
    t = h2t.shape[0] // TOK_TILES
    rb = ROW_BLOCK
    wr, br = _router_weights(w_rg, b_rg, w_re, b_re)
    meta, counts = _router(h2t, wr, br)
    group, rank = meta[:, 0], meta[:, 1]
    cnt = counts[0, 0:N_GROUPS]
    padded = (cnt + rb - 1) // rb * rb
    seg_end = jnp.cumsum(padded)
    seg_start = seg_end - padded
    dest = seg_start[group] + rank
    n_blocks = t // rb + N_GROUPS
    block_start = jnp.arange(n_blocks, dtype=I32) * rb
    block_group = jnp.minimum(jnp.sum((block_start[:, None] >= seg_end[None, :]).astype(I32), axis=1), N_GROUPS - 1)
    inv = _invert(dest, n_blocks * rb)
    return _experts(block_group, inv, h2t, wr, br, w_gu_bf16, w_dn_bf16)


def _final_kernel(x_ref, y_ref, m_ref, g_ref, o_ref, *, nblk):
    b = pl.program_id(0) // nblk
    d = D_MODEL
    m = _mod_row(m_ref, b)
    x = x_ref[...] + m[:, 5 * d:6 * d] * _load_token_rows(y_ref)
    o_ref[...] = _rms(x, d) * g_ref[...]


def _final(x1, y, m, g, nblk):
    t, d = x1.shape
    rb = ROW_BLOCK
    row = pl.BlockSpec((rb, d), lambda i: (i, 0))
    tok = pl.BlockSpec((rb * TOK_TILES, LANE), lambda i: (i, 0))
    full = lambda a: pl.BlockSpec(a.shape, lambda i: (0,) * a.ndim)
    return pl.pallas_call(
        functools.partial(_final_kernel, nblk=nblk),
        grid=(t // rb,),
        in_specs=[row, tok, full(m), full(g)],
        out_specs=row,
        out_shape=jax.ShapeDtypeStruct((t, d), F32),
        compiler_params=_cparams(("arbitrary",)),
        name="final_norm",
    )(x1, y, m, g)


def _inproj_weight(w_in):
    d = w_in.shape[0]
    z = lambda n: jnp.zeros((d, n), w_in.dtype)
    perm = _rope_swap_perm()
    kr = w_in[:, 3168:3200]
    cols = [w_in[:, 0:2048], w_in[:, 2048:2848], z(W_C - 800),
            w_in[:, 2848:3040], z(256 - D_Q_RANK), w_in[:, 3040:3168], kr, kr[:, perm], z(LANE - 2 * D_ROPE)]
    return jnp.concatenate(cols, axis=1).astype(BF16)


def _hgrn_lower_bounds(logits):
    cum = jnp.cumsum(jax.nn.softmax(logits.astype(F32), axis=0), axis=0)
    return cum - cum[0]


def kernel(x, c, ctx, c_ctx, w_mod, b_mod, norm1_g, norm2_g, w_in, w_out, hgrn_lb_logits, hgrn_norm_g, na_rpb, gla_wg_f, gla_bg_f, gla_wg_b, gla_bg_b, gla_norm_g, mla_q_norm_g, mla_w_uq, mla_kv_norm_g, mla_w_ukv, moe_w_rg, moe_b_rg, moe_w_re, moe_b_re, moe_w_gu, moe_w_dn, final_norm_g):
    batch, n, d = x.shape
    l_ctx = ctx.shape[1]
    assert d == D_MODEL and l_ctx == ROW_BLOCK and n % ROW_BLOCK == 0 and batch <= 4
    s_len = l_ctx + n
    nblk = s_len // ROW_BLOCK
    depth = w_mod.shape[0]

    c8 = jnp.zeros((8, d), F32).at[0:batch].set(c).at[4].set(c_ctx)
    mods = _mod_vectors(c8, w_mod, b_mod)
    lower_bounds = _hgrn_lower_bounds(hgrn_lb_logits)
    tables = _mla_tables(n, l_ctx)

    xa = y_prev = None
    for layer in range(depth):
        keep_ctx = layer < depth - 1
        m = mods[layer]
        stream = (xa, y_prev, mods[layer - 1]) if layer else (ctx, x)
        xa, (pa, pb, pc, pd) = _inproj(stream, m, norm1_g[layer].reshape(1, d), _inproj_weight(w_in[layer]), nblk)
        oaf, oab = _hgrn_scan(pa, lower_bounds[layer], batch, nblk)
        ob = _neighbourhood_attention(pb, na_rpb[layer], batch, s_len, l_ctx, keep_ctx)
        ogf, ogb = _gla_scan(pc, gla_wg_f[layer], gla_bg_f[layer], gla_wg_b[layer], gla_bg_b[layer], batch, nblk)
        mla_w = _mla_weights(mla_q_norm_g[layer], mla_w_uq[layer], mla_kv_norm_g[layer], mla_w_ukv[layer])
        q, k, v = _mla_prep(pd, mla_w, tables, nblk)
        od = _mla_attention(q, k, v, batch, s_len, l_ctx, keep_ctx)
        xa, h2 = _outproj(oaf, oab, pa, ob, ogf, ogb, pc, od, xa, m, hgrn_norm_g[layer].reshape(1, -1),
                          gla_norm_g[layer].reshape(1, -1), norm2_g[layer].reshape(1, d), w_out[layer],
                          batch, nblk, keep_ctx)
        y_prev = _moe(h2, moe_w_rg[layer], moe_b_rg[layer], moe_w_re[layer], moe_b_re[layer],
                      moe_w_gu[layer].astype(BF16), moe_w_dn[layer].astype(BF16))
    out = _final(xa, y_prev, mods[depth - 1], final_norm_g.reshape(1, d), n // ROW_BLOCK)
    return out.reshape(batch, n, d)
```
